```python
import jax, jax.numpy as jnp
from jax import lax
import numpy as np

D_MODEL = 1024
BATCH = 16
SEQ = 2048
DEPTH = 2

GRID_W = 64
CTX_LEN = 256
N_MIXERS = 2
POOL_WINDOWS = (2, 4, 8, 16)
N_POOL_GROUPS = 4
POOL_GC = D_MODEL // N_POOL_GROUPS
D_RNN = (4 * D_MODEL // 3) // 128 * 128
LRU_BLOCK = 128
LRU_HEADS = D_RNN // LRU_BLOCK
CONV_W = 4
LRU_C = 8.0
D_FF = (8 * D_MODEL // 3 + 255) // 256 * 256
N_EXPERTS = 8
TOP_K = 2
D_EXPERT = 7 * D_MODEL // 2
MOE_BLOCK = 256
EPS = 1e-6

kernel_name = "hybrid_pool_rglru_moe_dit_trunk"


def rmsnorm(x, g):
    xf = x.astype(jnp.float32)
    y = xf * lax.rsqrt(jnp.mean(xf * xf, axis=-1, keepdims=True) + EPS)
    return (y * g.astype(jnp.float32)).astype(x.dtype)


def modulate(h, shift, scale):
    return h * (1 + scale) + shift


def centred_window_mean(u, win, axis):
    n = u.shape[axis]
    cs = jnp.cumsum(u.astype(jnp.float32), axis=axis)
    pad = [(0, 0)] * u.ndim
    pad[axis] = (1, 0)
    cs = jnp.pad(cs, pad)
    t = np.arange(n)
    lo = np.clip(t - win // 2, 0, n)
    hi = np.clip(t - win // 2 + win, 0, n)
    shape = [1] * u.ndim
    shape[axis] = n
    count = jnp.asarray((hi - lo).astype(np.float32)).reshape(shape)
    s = jnp.take(cs, jnp.asarray(hi), axis=axis) - jnp.take(cs, jnp.asarray(lo), axis=axis)
    return (s / count).astype(u.dtype)


def pool_mixer(h, axis, w, scale):
    groups = jnp.split(h, N_POOL_GROUPS, axis=-1)
    pooled = jnp.stack([centred_window_mean(g, win, axis) - g
                        for g, win in zip(groups, POOL_WINDOWS)], axis=-2)
    y = jnp.einsum('...gc,gcd->...gd', pooled, w)
    return y.reshape(h.shape) * scale


def centred_depthwise_conv(u, w, b):
    L = u.shape[1]
    left = CONV_W // 2
    right = CONV_W - 1 - left
    up = jnp.pad(u, ((0, 0), (left, right), (0, 0)))
    return sum(up[:, k:k + L] * w[k] for k in range(CONV_W)) + b


def lru_inputs(h, w_in, conv_w, conv_b):
    gate, u = jnp.split(h @ w_in, 2, axis=-1)
    return gate, centred_depthwise_conv(u, conv_w, conv_b)


def _scan_combine(left, right):
    a_l, b_l = left
    a_r, b_r = right
    return a_l * a_r, a_r * b_l + b_r


def rglru_scan(u, w_r, b_r, w_i, b_i, lam, h0, reverse):
    bsz, L, _ = u.shape
    ub = u.reshape(bsz, L, LRU_HEADS, LRU_BLOCK)
    r = jax.nn.sigmoid(jnp.einsum('blhc,hcd->blhd', ub, w_r).reshape(bsz, L, D_RNN) + b_r)
    i = jax.nn.sigmoid(jnp.einsum('blhc,hcd->blhd', ub, w_i).reshape(bsz, L, D_RNN) + b_i)
    log_a = -LRU_C * r.astype(jnp.float32) * jax.nn.softplus(-lam.astype(jnp.float32))
    a = jnp.exp(log_a)
    b = jnp.sqrt(-jnp.expm1(2.0 * log_a)) * (i * u).astype(jnp.float32)
    if reverse:
        a, b = jnp.flip(a, 1), jnp.flip(b, 1)
    a_cum, h = lax.associative_scan(_scan_combine, (a, b), axis=1)
    h = h + a_cum * h0[:, None, :]
    if reverse:
        h = jnp.flip(h, 1)
    return h


def rglru_mixer(hx, hz, w_in, conv_w, conv_b, w_r, b_r, w_i, b_i, lam, w_out, need_ctx_out):
    gx, ux = lru_inputs(hx, w_in, conv_w, conv_b)
    gz, uz = lru_inputs(hz, w_in, conv_w, conv_b)
    h0 = jnp.zeros((uz.shape[0], D_RNN), jnp.float32)
    sum_x = 0.0
    sum_z = 0.0
    for d, reverse in enumerate((False, True)):
        hz_d = rglru_scan(uz, w_r[d], b_r[d], w_i[d], b_i[d], lam[d], h0, reverse)
        state = hz_d[:, 0] if reverse else hz_d[:, -1]
        hx_d = rglru_scan(ux, w_r[d], b_r[d], w_i[d], b_i[d], lam[d], state, reverse)
        sum_x = sum_x + hx_d
        sum_z = sum_z + hz_d
    yx = (sum_x.astype(hx.dtype) * jax.nn.gelu(gx)) @ w_out
    yz = (sum_z.astype(hz.dtype) * jax.nn.gelu(gz)) @ w_out if need_ctx_out else None
    return yx, yz


def swiglu(h, w_gu, w_down):
    g, u = jnp.split(h @ w_gu, 2, axis=-1)
    return (jax.nn.silu(g) * u) @ w_down


def moe_swiglu(h, w_router, w_gu, w_down):
    shp = h.shape
    t = h.reshape(-1, shp[-1])
    n = t.shape[0]
    logits = t.astype(jnp.float32) @ w_router.astype(jnp.float32)
    probs = jax.nn.softmax(logits, axis=-1)
    top_p, top_e = lax.top_k(probs, TOP_K)
    top_w = (top_p / jnp.sum(top_p, axis=-1, keepdims=True)).astype(h.dtype)
    flat_e = top_e.reshape(-1)
    flat_w = top_w.reshape(-1)
    flat_tok = jnp.repeat(jnp.arange(n, dtype=jnp.int32), TOP_K)
    order = jnp.argsort(flat_e)
    se = flat_e[order]
    counts = jnp.bincount(flat_e, length=N_EXPERTS)
    padded = (counts + MOE_BLOCK - 1) // MOE_BLOCK * MOE_BLOCK
    starts = jnp.cumsum(counts) - counts
    pends = jnp.cumsum(padded)
    pstarts = pends - padded
    rank = jnp.arange(n * TOP_K, dtype=jnp.int32) - starts[se]
    dest = pstarts[se] + rank
    n_slots = (-(-(n * TOP_K) // MOE_BLOCK) + N_EXPERTS) * MOE_BLOCK
    n_blocks = n_slots // MOE_BLOCK
    slot_tok = jnp.full((n_slots,), n, jnp.int32).at[dest].set(flat_tok[order])
    slot_w = jnp.zeros((n_slots,), h.dtype).at[dest].set(flat_w[order])
    block_e = jnp.minimum(jnp.searchsorted(pends, jnp.arange(n_blocks, dtype=jnp.int32) * MOE_BLOCK,
                                           side='right'), N_EXPERTS - 1)
    t_pad = jnp.concatenate([t, jnp.zeros((1, t.shape[1]), t.dtype)], axis=0)

    def expert_block(args):
        tok, e = args
        return swiglu(t_pad[tok], w_gu[e], w_down[e])

    yb = lax.map(expert_block, (slot_tok.reshape(n_blocks, MOE_BLOCK), block_e))
    y = jax.ops.segment_sum(yb.reshape(n_slots, -1) * slot_w[:, None], slot_tok, num_segments=n + 1)[:n]
    return y.reshape(shp)


def setup_inputs(seed: int = 0) -> dict:
    key = jax.random.key(seed)
    ks = iter(jax.random.split(key, 40))
    n_even = (DEPTH + 1) // 2
    n_odd = DEPTH // 2
    D = D_MODEL

    def nrm(shape, scale):
        return jax.random.normal(next(ks), shape, jnp.float32) * scale

    x = nrm((BATCH, SEQ, D), 1.0)
    c = nrm((BATCH, D), 1.0)
    ctx = nrm((BATCH, CTX_LEN, D), 1.0)
    c_ctx = nrm((D,), 1.0)
    ada_w = nrm((DEPTH, D, 6 * D), 0.5 * D ** -0.5)
    ada_b = nrm((DEPTH, 6 * D), 0.02)
    norm1_g = 1.0 + nrm((DEPTH, D), 0.05)
    norm2_g = 1.0 + nrm((DEPTH, D), 0.05)
    pool_w = nrm((n_even, N_POOL_GROUPS, POOL_GC, POOL_GC), POOL_GC ** -0.5)
    pool_scale = 1.0 + nrm((n_even, D), 0.05)
    ffn_w_gu = nrm((n_even, D, 2 * D_FF), D ** -0.5)
    ffn_w_down = nrm((n_even, D_FF, D), D_FF ** -0.5)
    lru_w_in = nrm((n_odd, D, 2 * D_RNN), D ** -0.5)
    lru_conv_w = nrm((n_odd, CONV_W, D_RNN), CONV_W ** -0.5)
    lru_conv_b = nrm((n_odd, D_RNN), 0.02)
    lru_w_r = nrm((n_odd, 2, LRU_HEADS, LRU_BLOCK, LRU_BLOCK), LRU_BLOCK ** -0.5)
    lru_b_r = nrm((n_odd, 2, D_RNN), 0.02)
    lru_w_i = nrm((n_odd, 2, LRU_HEADS, LRU_BLOCK, LRU_BLOCK), LRU_BLOCK ** -0.5)
    lru_b_i = nrm((n_odd, 2, D_RNN), 0.02)
    a8 = jax.random.uniform(next(ks), (n_odd, 2, D_RNN), jnp.float32, minval=0.9, maxval=0.999)
    a_base = a8 ** (1.0 / LRU_C)
    lru_lambda = jnp.log(a_base) - jnp.log1p(-a_base)
    lru_w_out = nrm((n_odd, D_RNN, D), D_RNN ** -0.5)
    moe_w_router = nrm((n_odd, D, N_EXPERTS), D ** -0.5)
    moe_w_gu = nrm((n_odd, N_EXPERTS, D, 2 * D_EXPERT), D ** -0.5)
    moe_w_down = nrm((n_odd, N_EXPERTS, D_EXPERT, D), D_EXPERT ** -0.5)
    final_g = 1.0 + nrm((D,), 0.05)
    return {"x": x, "c": c, "ctx": ctx, "c_ctx": c_ctx,
            "ada_w": ada_w, "ada_b": ada_b, "norm1_g": norm1_g, "norm2_g": norm2_g,
            "pool_w": pool_w, "pool_scale": pool_scale,
            "ffn_w_gu": ffn_w_gu, "ffn_w_down": ffn_w_down,
            "lru_w_in": lru_w_in, "lru_conv_w": lru_conv_w, "lru_conv_b": lru_conv_b,
            "lru_w_r": lru_w_r, "lru_b_r": lru_b_r, "lru_w_i": lru_w_i, "lru_b_i": lru_b_i,
            "lru_lambda": lru_lambda, "lru_w_out": lru_w_out,
            "moe_w_router": moe_w_router, "moe_w_gu": moe_w_gu, "moe_w_down": moe_w_down,
            "final_g": final_g}


def reference(x, c, ctx, c_ctx, ada_w, ada_b, norm1_g, norm2_g, pool_w, pool_scale,
              ffn_w_gu, ffn_w_down, lru_w_in, lru_conv_w, lru_conv_b, lru_w_r, lru_b_r,
              lru_w_i, lru_b_i, lru_lambda, lru_w_out, moe_w_router, moe_w_gu, moe_w_down,
              final_g):
    bsz, seq_len, d = x.shape
    rows = seq_len // GRID_W
    z = ctx
    for i in range(DEPTH):
        is_last = i == DEPTH - 1
        j = i // N_MIXERS
        mx = (jax.nn.silu(c) @ ada_w[i] + ada_b[i])[:, None, :]
        mz = jax.nn.silu(c_ctx) @ ada_w[i] + ada_b[i]
        sh1x, sc1x, g1x, sh2x, sc2x, g2x = jnp.split(mx, 6, axis=-1)
        sh1z, sc1z, g1z, sh2z, sc2z, g2z = jnp.split(mz, 6, axis=-1)
        hx = modulate(rmsnorm(x, norm1_g[i]), sh1x, sc1x)
        if i % N_MIXERS == 0:
            yx = pool_mixer(hx.reshape(bsz, rows, GRID_W, d), 2, pool_w[j], pool_scale[j])
            x = x + g1x * yx.reshape(bsz, seq_len, d)
            if not is_last:
                hz = modulate(rmsnorm(z, norm1_g[i]), sh1z, sc1z)
                z = z + g1z * pool_mixer(hz, 1, pool_w[j], pool_scale[j])
            x = x + g2x * swiglu(modulate(rmsnorm(x, norm2_g[i]), sh2x, sc2x), ffn_w_gu[j], ffn_w_down[j])
            if not is_last:
                z = z + g2z * swiglu(modulate(rmsnorm(z, norm2_g[i]), sh2z, sc2z), ffn_w_gu[j], ffn_w_down[j])
        else:
            hz = modulate(rmsnorm(z, norm1_g[i]), sh1z, sc1z)
            yx, yz = rglru_mixer(hx, hz, lru_w_in[j], lru_conv_w[j], lru_conv_b[j], lru_w_r[j], lru_b_r[j],
                                 lru_w_i[j], lru_b_i[j], lru_lambda[j], lru_w_out[j], not is_last)
            x = x + g1x * yx
            if not is_last:
                z = z + g1z * yz
            x = x + g2x * moe_swiglu(modulate(rmsnorm(x, norm2_g[i]), sh2x, sc2x),
                                     moe_w_router[j], moe_w_gu[j], moe_w_down[j])
            if not is_last:
                z = z + g2z * moe_swiglu(modulate(rmsnorm(z, norm2_g[i]), sh2z, sc2z),
                                         moe_w_router[j], moe_w_gu[j], moe_w_down[j])
    return rmsnorm(x, final_g)
```

```python
import functools

import numpy as np
import jax
import jax.numpy as jnp
from jax import lax
from jax.experimental import pallas as pl
from jax.experimental.pallas import tpu as pltpu

F32 = jnp.float32
BF16 = jnp.bfloat16

GRID_W = 64
POOL_WINDOWS = (2, 4, 8, 16)
LRU_BLOCK = 128
LRU_C = 8.0
N_EXPERTS = 8
EPS = 1e-6

LANES = 128
SUBLANES = 8
VMEM_LIMIT = 56 * 1024 * 1024

POOL_TILE = 256
TOK_TILE = 256
FF_CHUNK = 256
MOE_TILE = 1024
MOE_FCHUNK = 512
GATHER_CHUNK = 1024
COMBINE_TILE = 256


def _dot(a, b):
    return jnp.dot(a, b, preferred_element_type=F32)


def _params(*sem):
    return pltpu.CompilerParams(dimension_semantics=sem, vmem_limit_bytes=VMEM_LIMIT)


def _norm_mod(x, g, shift, scale):
    ms = jnp.mean(x * x, axis=-1, keepdims=True)
    return (x * lax.rsqrt(ms + EPS)) * (g * (1.0 + scale)) + shift


def _split_bf16(v):
    hi = v.astype(BF16)
    lo = (v - hi.astype(F32)).astype(BF16)
    return hi, lo


def _ada_kernel(cc_ref, w_ref, b_ref, o_ref):
    s = cc_ref[...]
    s = s * jax.nn.sigmoid(s)
    o_ref[0] = _dot(s.astype(BF16), w_ref[0].astype(BF16)) + b_ref[0]


def _adaln(cc, ada_w, ada_b):
    n_layers, d, n = ada_w.shape
    r = cc.shape[0]
    tn = 1536
    return pl.pallas_call(
        _ada_kernel,
        grid=(n_layers, n // tn),
        in_specs=[pl.BlockSpec((r, d), lambda l, j: (0, 0)),
                  pl.BlockSpec((1, d, tn), lambda l, j: (l, 0, j)),
                  pl.BlockSpec((1, 1, tn), lambda l, j: (l, 0, j))],
        out_specs=pl.BlockSpec((1, r, tn), lambda l, j: (l, 0, j)),
        out_shape=jax.ShapeDtypeStruct((n_layers, r, n), F32),
        compiler_params=_params("parallel", "parallel"),
        name="adaln",
    )(cc, ada_w, ada_b.reshape(n_layers, 1, n))


def _pool_tables(seg_len, gc):
    t = np.arange(POOL_TILE)
    seg, pos = t // seg_len, t % seg_len
    band = np.zeros((len(POOL_WINDOWS), POOL_TILE, POOL_TILE), np.float32)
    invc = np.zeros((len(POOL_WINDOWS), POOL_TILE, gc), np.float32)
    for gi, win in enumerate(POOL_WINDOWS):
        lo = np.clip(pos - win // 2, 0, seg_len)
        hi = np.clip(pos - win // 2 + win, 0, seg_len)
        inside = (pos[None, :] >= lo[:, None]) & (pos[None, :] < hi[:, None])
        band[gi] = (seg[:, None] == seg[None, :]) & inside
        invc[gi] = (1.0 / (hi - lo))[:, None]
    return jnp.asarray(band, BF16), jnp.asarray(invc, F32)


def _layer0_kernel(x_ref, mod_ref, g1_ref, g2_ref, band_ref, invc_ref, pw_ref, ps_ref,
                   wgu_ref, wd_ref, o_ref, act_ref, *, d_ff):
    m = mod_ref[0]
    x = x_ref[0]
    tm, d = x.shape
    n_groups = band_ref.shape[0]
    gc = d // n_groups
    h = _norm_mod(x, g1_ref[...], m[0:1], m[1:2])
    rows = []
    for s in range(tm // POOL_TILE):
        hs = h[s * POOL_TILE:(s + 1) * POOL_TILE]
        cols = []
        for gi in range(n_groups):
            hg = hs[:, gi * gc:(gi + 1) * gc]
            hi, lo = _split_bf16(hg)
            band = band_ref[gi]
            pooled = (_dot(band, hi) + _dot(band, lo)) * invc_ref[gi] - hg
            cols.append(_dot(pooled.astype(BF16), pw_ref[gi]))
        rows.append(jnp.concatenate(cols, axis=1))
    y = rows[0] if len(rows) == 1 else jnp.concatenate(rows, axis=0)
    x1 = x + m[2:3] * (y * ps_ref[...])
    h2 = _norm_mod(x1, g2_ref[...], m[3:4], m[4:5]).astype(BF16)
    for c in range(d_ff // FF_CHUNK):
        g = _dot(h2, wgu_ref[:, c * FF_CHUNK:(c + 1) * FF_CHUNK])
        u = _dot(h2, wgu_ref[:, d_ff + c * FF_CHUNK:d_ff + (c + 1) * FF_CHUNK])
        act_ref[:, c * FF_CHUNK:(c + 1) * FF_CHUNK] = ((g * jax.nn.sigmoid(g)) * u).astype(BF16)
    o_ref[0] = x1 + m[5:6] * _dot(act_ref[...], wd_ref[...])


def _layer0(x, mods, mod_row, g1, g2, band, invc, pw, ps, wgu, wd, name):
    bsz, t, d = x.shape
    d_ff = wd.shape[0]
    tm = TOK_TILE
    const2 = lambda b, i: (0, 0)
    const3 = lambda b, i: (0, 0, 0)
    return pl.pallas_call(
        functools.partial(_layer0_kernel, d_ff=d_ff),
        grid=(bsz, t // tm),
        in_specs=[pl.BlockSpec((1, tm, d), lambda b, i: (b, i, 0)),
                  pl.BlockSpec((1, 6, d), lambda b, i: (mod_row(b), 0, 0)),
                  pl.BlockSpec((1, d), const2),
                  pl.BlockSpec((1, d), const2),
                  pl.BlockSpec(band.shape, const3),
                  pl.BlockSpec(invc.shape, const3),
                  pl.BlockSpec(pw.shape, const3),
                  pl.BlockSpec((1, d), const2),
                  pl.BlockSpec(wgu.shape, const2),
                  pl.BlockSpec(wd.shape, const2)],
        out_specs=pl.BlockSpec((1, tm, d), lambda b, i: (b, i, 0)),
        out_shape=jax.ShapeDtypeStruct(x.shape, F32),
        scratch_shapes=[pltpu.VMEM((tm, d_ff), BF16)],
        compiler_params=_params("parallel", "parallel"),
        name=name,
    )(x, mods, g1, g2, band, invc, pw, ps, wgu, wd)


def _inproj_kernel(x_ref, mod_ref, g1_ref, w_ref, *out_refs):
    m = mod_ref[0]
    h = _norm_mod(x_ref[0], g1_ref[...], m[0:1], m[1:2]).astype(BF16)
    y = _dot(h, w_ref[...])
    n = out_refs[0].shape[-1]
    for k, o_ref in enumerate(out_refs):
        o_ref[0] = y[:, k * n:(k + 1) * n]


def _inproj(x, mods, mod_row, g1, w, n_out, name):
    bsz, t, d = x.shape
    tm = TOK_TILE
    n = w.shape[1] // n_out
    out_shape = jax.ShapeDtypeStruct((bsz, t, n), F32)
    out_spec = pl.BlockSpec((1, tm, n), lambda b, i: (b, i, 0))
    return pl.pallas_call(
        _inproj_kernel,
        grid=(bsz, t // tm),
        in_specs=[pl.BlockSpec((1, tm, d), lambda b, i: (b, i, 0)),
                  pl.BlockSpec((1, 6, d), lambda b, i: (mod_row(b), 0, 0)),
                  pl.BlockSpec((1, d), lambda b, i: (0, 0)),
                  pl.BlockSpec(w.shape, lambda b, i: (0, 0))],
        out_specs=[out_spec] * n_out,
        out_shape=[out_shape] * n_out,
        compiler_params=_params("parallel", "parallel"),
        name=name,
    )(x, mods, g1, w)


def _scan8(a, b, row, reverse):
    for s in (1, 2, 4):
        if reverse:
            keep = row < SUBLANES - s
            shift = SUBLANES - s
        else:
            keep = row >= s
            shift = s
        a_sh = jnp.where(keep, pltpu.roll(a, shift, 0), 1.0)
        b_sh = jnp.where(keep, pltpu.roll(b, shift, 0), 0.0)
        b = a * b_sh + b
        a = a * a_sh
    return a, b


def _lru_kernel(uz_ref, ux_ref, gate_ref, cw_ref, cb_ref, wc_ref, bc_ref, lam_ref, o_ref,
                af_ref, bf_ref, ab_ref, bb_ref, hf_ref, hb_ref):
    lz = uz_ref.shape[1]
    lx = ux_ref.shape[1]
    cw = cw_ref[...]
    cb = cb_ref[...]
    wc = wc_ref[0]
    bc = bc_ref[0]
    neg_lam = -lam_ref[...]
    softplus = jnp.maximum(neg_lam, 0.0) + jnp.log1p(jnp.exp(-jnp.abs(neg_lam)))
    kco = -LRU_C * softplus

    def prep(up, off):
        n = up.shape[0]
        row = lax.broadcasted_iota(jnp.int32, up.shape, 0)
        u = cb + cw[2:3] * up
        u = u + cw[0:1] * jnp.where(row >= 2, pltpu.roll(up, 2, 0), 0.0)
        u = u + cw[1:2] * jnp.where(row >= 1, pltpu.roll(up, 1, 0), 0.0)
        u = u + cw[3:4] * jnp.where(row < n - 1, pltpu.roll(up, n - 1, 0), 0.0)
        g = _dot(u.astype(BF16), wc) + bc
        for d, (a_ref, b_ref) in enumerate(((af_ref, bf_ref), (ab_ref, bb_ref))):
            r = jax.nn.sigmoid(g[:, 2 * d * LRU_BLOCK:(2 * d + 1) * LRU_BLOCK])
            i = jax.nn.sigmoid(g[:, (2 * d + 1) * LRU_BLOCK:(2 * d + 2) * LRU_BLOCK])
            log_a = r * kco[d:d + 1]
            a = jnp.exp(log_a)
            a_ref[pl.ds(off, n), :] = a
            b_ref[pl.ds(off, n), :] = jnp.sqrt(1.0 - a * a) * (i * u)

    prep(uz_ref[0], 0)
    prep(ux_ref[0], lz)

    nz = lz // SUBLANES
    nblk = (lz + lx) // SUBLANES
    row8 = lax.broadcasted_iota(jnp.int32, (SUBLANES, LANES), 0)

    def step(k, carry):
        hf, hb = carry
        rf = pl.multiple_of(k * SUBLANES, SUBLANES)
        a, b = _scan8(af_ref[pl.ds(rf, SUBLANES), :], bf_ref[pl.ds(rf, SUBLANES), :], row8, False)
        hf = b + a * jnp.broadcast_to(hf[SUBLANES - 1:SUBLANES], hf.shape)
        hf_ref[pl.ds(rf, SUBLANES), :] = hf
        kb = jnp.where(k < nz, nz - 1 - k, nblk - 1 - (k - nz))
        rb = pl.multiple_of(kb * SUBLANES, SUBLANES)
        a, b = _scan8(ab_ref[pl.ds(rb, SUBLANES), :], bb_ref[pl.ds(rb, SUBLANES), :], row8, True)
        hb = b + a * jnp.broadcast_to(hb[0:1], hb.shape)
        hb_ref[pl.ds(rb, SUBLANES), :] = hb
        return hf, hb

    zero = jnp.zeros((SUBLANES, LANES), F32)
    lax.fori_loop(0, nblk, step, (zero, zero))
    hsum = hf_ref[pl.ds(lz, lx), :] + hb_ref[pl.ds(lz, lx), :]
    o_ref[0] = (hsum * jax.nn.gelu(gate_ref[0])).astype(BF16)


def _lru(uz, ux, gate, conv_w, conv_b, wcat, bcat, lam):
    bsz, lz, d_rnn = uz.shape
    lx = ux.shape[1]
    heads = d_rnn // LRU_BLOCK
    seq = lambda n: pl.BlockSpec((1, n, LRU_BLOCK), lambda b, h: (b, 0, h))
    scratch = pltpu.VMEM((lz + lx, LRU_BLOCK), F32)
    return pl.pallas_call(
        _lru_kernel,
        grid=(bsz, heads),
        in_specs=[seq(lz), seq(lx), seq(lx),
                  pl.BlockSpec((conv_w.shape[0], LRU_BLOCK), lambda b, h: (0, h)),
                  pl.BlockSpec((1, LRU_BLOCK), lambda b, h: (0, h)),
                  pl.BlockSpec((1, LRU_BLOCK, 4 * LRU_BLOCK), lambda b, h: (h, 0, 0)),
                  pl.BlockSpec((1, 1, 4 * LRU_BLOCK), lambda b, h: (h, 0, 0)),
                  pl.BlockSpec((2, LRU_BLOCK), lambda b, h: (0, h))],
        out_specs=seq(lx),
        out_shape=jax.ShapeDtypeStruct((bsz, lx, d_rnn), BF16),
        scratch_shapes=[scratch] * 6,
        compiler_params=_params("parallel", "parallel"),
        name="rglru",
    )(uz, ux, gate, conv_w, conv_b, wcat, bcat, lam)


def _store_row_tiles(ref, v):
    tm, d = v.shape
    for c in range(d // LANES):
        ref[pl.ds(c, tm, stride=d // LANES), :] = v[:, c * LANES:(c + 1) * LANES]


def _load_row_tiles(ref, tm, d):
    return [ref[pl.ds(c, tm, stride=d // LANES), :] for c in range(d // LANES)]


def _post_kernel(y_ref, x_ref, mod_ref, g2_ref, wo_ref, wrh_ref, wrl_ref, x2_ref, h2t_ref, ri_ref):
    m = mod_ref[0]
    x2 = x_ref[0] + m[2:3] * _dot(y_ref[0], wo_ref[...])
    x2_ref[0] = x2
    h2 = _norm_mod(x2, g2_ref[...], m[3:4], m[4:5])
    _store_row_tiles(h2t_ref, h2)
    hi, lo = _split_bf16(h2)
    logits = _dot(hi, wrh_ref[...]) + (_dot(hi, wrl_ref[...]) + _dot(lo, wrh_ref[...]))
    lane = lax.broadcasted_iota(jnp.int32, logits.shape, 1)
    logits = jnp.where(lane < N_EXPERTS, logits, -1e30)
    e = jnp.exp(logits - jnp.max(logits, axis=-1, keepdims=True))
    p = e / jnp.sum(e, axis=-1, keepdims=True)
    p1 = jnp.max(p, axis=-1, keepdims=True)
    i1 = jnp.min(jnp.where(p == p1, lane, LANES), axis=-1, keepdims=True)
    rest = jnp.where(lane == i1, -1.0, p)
    p2 = jnp.max(rest, axis=-1, keepdims=True)
    i2 = jnp.min(jnp.where(rest == p2, lane, LANES), axis=-1, keepdims=True)
    den = p1 + p2
    ri_ref[...] = jnp.where(lane == 0, i1.astype(F32),
                            jnp.where(lane == 1, i2.astype(F32),
                                      jnp.where(lane == 2, p1 / den,
                                                jnp.where(lane == 3, p2 / den, 0.0))))


def _post(y, x, mods, g2, wo, wr_hi, wr_lo):
    bsz, t, d = x.shape
    d_rnn = y.shape[-1]
    tm = TOK_TILE
    nt = t // tm
    rt = d // LANES
    const2 = lambda b, i: (0, 0)
    return pl.pallas_call(
        _post_kernel,
        grid=(bsz, nt),
        in_specs=[pl.BlockSpec((1, tm, d_rnn), lambda b, i: (b, i, 0)),
                  pl.BlockSpec((1, tm, d), lambda b, i: (b, i, 0)),
                  pl.BlockSpec((1, 6, d), lambda b, i: (b, 0, 0)),
                  pl.BlockSpec((1, d), const2),
                  pl.BlockSpec(wo.shape, const2),
                  pl.BlockSpec(wr_hi.shape, const2),
                  pl.BlockSpec(wr_lo.shape, const2)],
        out_specs=[pl.BlockSpec((1, tm, d), lambda b, i: (b, i, 0)),
                   pl.BlockSpec((tm * rt, LANES), lambda b, i: (b * nt + i, 0)),
                   pl.BlockSpec((tm, LANES), lambda b, i: (b * nt + i, 0))],
        out_shape=[jax.ShapeDtypeStruct(x.shape, F32),
                   jax.ShapeDtypeStruct((bsz * t * rt, LANES), F32),
                   jax.ShapeDtypeStruct((bsz * t, LANES), F32)],
        compiler_params=_params("parallel", "parallel"),
        name="outproj_router",
    )(y, x, mods, g2, wo, wr_hi, wr_lo)


def _dispatch_kernel(idx_ref, src_ref, dst_ref, sem):
    base = pl.program_id(0) * GATHER_CHUNK

    def issue(r, carry):
        src_row = pl.multiple_of(idx_ref[r] * SUBLANES, SUBLANES)
        dst_row = pl.multiple_of((base + r) * SUBLANES, SUBLANES)
        pltpu.make_async_copy(src_ref.at[pl.ds(src_row, SUBLANES), :],
                              dst_ref.at[pl.ds(dst_row, SUBLANES), :], sem).start()
        return carry

    lax.fori_loop(0, GATHER_CHUNK, issue, 0)
    whole = dst_ref.at[pl.ds(pl.multiple_of(base * SUBLANES, SUBLANES), GATHER_CHUNK * SUBLANES), :]
    pltpu.make_async_copy(whole, whole, sem).wait()


def _dispatch(slot_tok, h2t):
    n_slots = slot_tok.shape[0]
    return pl.pallas_call(
        _dispatch_kernel,
        grid=(n_slots // GATHER_CHUNK,),
        in_specs=[pl.BlockSpec((GATHER_CHUNK,), lambda i: (i,), memory_space=pltpu.SMEM),
                  pl.BlockSpec(memory_space=pl.ANY)],
        out_specs=pl.BlockSpec(memory_space=pl.ANY),
        out_shape=jax.ShapeDtypeStruct((n_slots * SUBLANES, LANES), F32),
        scratch_shapes=[pltpu.SemaphoreType.DMA],
        compiler_params=_params("arbitrary"),
        name="moe_dispatch",
    )(slot_tok, h2t)


def _moe_kernel(be_ref, nu_ref, xs_ref, wg_ref, wu_ref, wd_ref, o_ref, xb_ref, acc_ref):
    i = pl.program_id(0)
    j = pl.program_id(1)
    nj = pl.num_programs(1)
    tm, d = xb_ref.shape
    used = i < nu_ref[0]

    @pl.when(used)
    def _():
        @pl.when(j == 0)
        def _():
            for c, v in enumerate(_load_row_tiles(xs_ref, tm, d)):
                xb_ref[:, c * LANES:(c + 1) * LANES] = v.astype(BF16)

        xb = xb_ref[...]
        g = _dot(xb, wg_ref[0])
        u = _dot(xb, wu_ref[0])
        part = _dot(((g * jax.nn.sigmoid(g)) * u).astype(BF16), wd_ref[0])

        @pl.when(j == 0)
        def _():
            acc_ref[...] = part

        @pl.when(j > 0)
        def _():
            acc_ref[...] += part

        @pl.when(j == nj - 1)
        def _():
            _store_row_tiles(o_ref, acc_ref[...])

    @pl.when(jnp.logical_and(jnp.logical_not(used), j == nj - 1))
    def _():
        o_ref[...] = jnp.zeros(o_ref.shape, o_ref.dtype)


def _moe(block_e, n_used, xs, wgu, wd):
    d = wgu.shape[1]
    d_e = wd.shape[1]
    rt = d // LANES
    n_tiles = xs.shape[0] // (MOE_TILE * rt)
    nj = d_e // MOE_FCHUNK

    def jf(i, j, nu):
        return jnp.where(i < nu[0], j, nj - 1)

    def it(i, nu):
        return jnp.minimum(i, jnp.maximum(nu[0] - 1, 0))

    return pl.pallas_call(
        _moe_kernel,
        grid_spec=pltpu.PrefetchScalarGridSpec(
            num_scalar_prefetch=2,
            grid=(n_tiles, nj),
            in_specs=[pl.BlockSpec((MOE_TILE * rt, LANES), lambda i, j, be, nu: (it(i, nu), 0)),
                      pl.BlockSpec((1, d, MOE_FCHUNK), lambda i, j, be, nu: (be[i], 0, jf(i, j, nu))),
                      pl.BlockSpec((1, d, MOE_FCHUNK), lambda i, j, be, nu: (be[i], 0, nj + jf(i, j, nu))),
                      pl.BlockSpec((1, MOE_FCHUNK, d), lambda i, j, be, nu: (be[i], jf(i, j, nu), 0))],
            out_specs=pl.BlockSpec((MOE_TILE * rt, LANES), lambda i, j, be, nu: (i, 0)),
            scratch_shapes=[pltpu.VMEM((MOE_TILE, d), BF16), pltpu.VMEM((MOE_TILE, d), F32)]),
        out_shape=jax.ShapeDtypeStruct(xs.shape, F32),
        compiler_params=_params("arbitrary", "arbitrary"),
        name="moe_experts",
    )(block_e, n_used, xs, wgu, wgu, wd)


def _combine_kernel(dest_ref, x_ref, ri_ref, mod_ref, fg_ref, ys_ref, o_ref, buf_a, buf_b, sem):
    tm, d = x_ref.shape[1], x_ref.shape[2]

    def issue(r, carry):
        dst_row = pl.multiple_of(r * SUBLANES, SUBLANES)
        for k, buf in enumerate((buf_a, buf_b)):
            src_row = pl.multiple_of(dest_ref[2 * r + k] * SUBLANES, SUBLANES)
            pltpu.make_async_copy(ys_ref.at[pl.ds(src_row, SUBLANES), :],
                                  buf.at[pl.ds(dst_row, SUBLANES), :], sem.at[k]).start()
        return carry

    lax.fori_loop(0, tm, issue, 0)
    for k, buf in enumerate((buf_a, buf_b)):
        pltpu.make_async_copy(buf, buf, sem.at[k]).wait()
    ri = ri_ref[...]
    w1 = ri[:, 2:3]
    w2 = ri[:, 3:4]
    ya = _load_row_tiles(buf_a, tm, d)
    yb = _load_row_tiles(buf_b, tm, d)
    moe = jnp.concatenate([w1 * a + w2 * b for a, b in zip(ya, yb)], axis=1)
    x3 = x_ref[0] + mod_ref[0][5:6] * moe
    ms = jnp.mean(x3 * x3, axis=-1, keepdims=True)
    o_ref[0] = (x3 * lax.rsqrt(ms + EPS)) * fg_ref[...]


def _combine(dest, x2, ri, mods, fg, ys):
    bsz, t, d = x2.shape
    tm = COMBINE_TILE
    nt = t // tm
    rt = d // LANES
    return pl.pallas_call(
        _combine_kernel,
        grid=(bsz, nt),
        in_specs=[pl.BlockSpec((2 * tm,), lambda b, i: (b * nt + i,), memory_space=pltpu.SMEM),
                  pl.BlockSpec((1, tm, d), lambda b, i: (b, i, 0)),
                  pl.BlockSpec((tm, LANES), lambda b, i: (b * nt + i, 0)),
                  pl.BlockSpec((1, 6, d), lambda b, i: (b, 0, 0)),
                  pl.BlockSpec((1, d), lambda b, i: (0, 0)),
                  pl.BlockSpec(memory_space=pl.ANY)],
        out_specs=pl.BlockSpec((1, tm, d), lambda b, i: (b, i, 0)),
        out_shape=jax.ShapeDtypeStruct(x2.shape, F32),
        scratch_shapes=[pltpu.VMEM((tm * rt, LANES), F32), pltpu.VMEM((tm * rt, LANES), F32),
                        pltpu.SemaphoreType.DMA((2,))],
        compiler_params=_params("arbitrary", "arbitrary"),
        name="moe_combine_norm",
    )(dest, x2, ri, mods, fg, ys)


def _routing(top_e, n_slots):
    n2 = top_e.size
    flat_e = top_e.reshape(-1)
    onehot = (flat_e[:, None] == jnp.arange(N_EXPERTS, dtype=jnp.int32)[None, :]).astype(jnp.int32)
    csum = jnp.cumsum(onehot, axis=0)
    counts = csum[-1]
    rank = jnp.take_along_axis(csum, flat_e[:, None], axis=1)[:, 0] - 1
    padded = (counts + MOE_TILE - 1) // MOE_TILE * MOE_TILE
    pends = jnp.cumsum(padded)
    pstarts = pends - padded
    starts = jnp.cumsum(counts) - counts
    dest = (pstarts[flat_e] + rank).astype(jnp.int32)
    order = jnp.argsort(flat_e, stable=True).astype(jnp.int32)
    slot = jnp.arange(n_slots, dtype=jnp.int32)
    slot_e = jnp.minimum(jnp.searchsorted(pends, slot, side='right'), N_EXPERTS - 1).astype(jnp.int32)
    r = slot - pstarts[slot_e]
    valid = r < counts[slot_e]
    src = order[jnp.clip(starts[slot_e] + r, 0, n2 - 1)] // top_e.shape[1]
    slot_tok = jnp.where(valid, src, 0).astype(jnp.int32)
    n_tiles = n_slots // MOE_TILE
    tile_e = jnp.minimum(jnp.searchsorted(pends, jnp.arange(n_tiles, dtype=jnp.int32) * MOE_TILE, side='right'),
                         N_EXPERTS - 1).astype(jnp.int32)
    n_used = (pends[-1] // MOE_TILE).astype(jnp.int32).reshape(1)
    return dest, slot_tok, tile_e, n_used


def kernel(x, c, ctx, c_ctx, ada_w, ada_b, norm1_g, norm2_g, pool_w, pool_scale, ffn_w_gu, ffn_w_down,
           lru_w_in, lru_conv_w, lru_conv_b, lru_w_r, lru_b_r, lru_w_i, lru_b_i, lru_lambda, lru_w_out,
           moe_w_router, moe_w_gu, moe_w_down, final_g):
    bsz, seq, d = x.shape
    assert ada_w.shape[0] == 2 and seq % TOK_TILE == 0 and ctx.shape[1] == POOL_TILE
    assert d == SUBLANES * LANES
    d_rnn = lru_w_out.shape[1]
    heads = d_rnn // LRU_BLOCK
    gc = d // len(POOL_WINDOWS)
    z_row = bsz

    n_rows = (bsz + 1 + SUBLANES - 1) // SUBLANES * SUBLANES
    cc = jnp.concatenate([c, c_ctx[None, :], jnp.zeros((n_rows - bsz - 1, d), F32)], axis=0)
    mods = _adaln(cc, ada_w, ada_b).reshape(2, n_rows, 6, d)
    x_row = lambda b: b
    ctx_row = lambda b: z_row

    band_x, invc_x = _pool_tables(GRID_W, gc)
    band_z, invc_z = _pool_tables(POOL_TILE, gc)
    l0 = (norm1_g[0:1], norm2_g[0:1])
    l0w = (pool_w[0].astype(BF16), pool_scale[0:1], ffn_w_gu[0].astype(BF16), ffn_w_down[0].astype(BF16))
    x1 = _layer0(x, mods[0], x_row, *l0, band_x, invc_x, *l0w, name="layer0_latent")
    z1 = _layer0(ctx, mods[0], ctx_row, *l0, band_z, invc_z, *l0w, name="layer0_context")

    w_in = lru_w_in[0].astype(BF16)
    gate_x, ux = _inproj(x1, mods[1], x_row, norm1_g[1:2], w_in, 2, name="lru_inproj_latent")
    (uz,) = _inproj(z1, mods[1], ctx_row, norm1_g[1:2], w_in[:, d_rnn:], 1, name="lru_inproj_context")
    wcat = jnp.concatenate([lru_w_r[0, 0], lru_w_i[0, 0], lru_w_r[0, 1], lru_w_i[0, 1]], axis=-1).astype(BF16)
    bcat = jnp.concatenate([v.reshape(heads, 1, LRU_BLOCK)
                            for v in (lru_b_r[0, 0], lru_b_i[0, 0], lru_b_r[0, 1], lru_b_i[0, 1])], axis=-1)
    y = _lru(uz, ux, gate_x, lru_conv_w[0], lru_conv_b[0:1], wcat, bcat, lru_lambda[0])

    wr = jnp.pad(moe_w_router[0], ((0, 0), (0, LANES - N_EXPERTS)))
    wr_hi = wr.astype(BF16)
    wr_lo = (wr - wr_hi.astype(F32)).astype(BF16)
    x2, h2t, ri = _post(y, x1, mods[1], norm2_g[1:2], lru_w_out[0].astype(BF16), wr_hi, wr_lo)

    n_tok = bsz * seq
    top_e = ri[:, 0:2].astype(jnp.int32)
    n_slots = 2 * n_tok + N_EXPERTS * MOE_TILE
    dest, slot_tok, tile_e, n_used = _routing(top_e, n_slots)
    xs = _dispatch(slot_tok, h2t)
    ys = _moe(tile_e, n_used, xs, moe_w_gu[0].astype(BF16), moe_w_down[0].astype(BF16))
    return _combine(dest, x2, ri, mods[1], final_g[None, :], ys)
```

```python
import functools

import numpy as np
import jax
import jax.numpy as jnp
from jax import lax
from jax.experimental import pallas as pl
from jax.experimental.pallas import tpu as pltpu

F32 = jnp.float32
BF16 = jnp.bfloat16

GRID_W = 64
POOL_WINDOWS = (2, 4, 8, 16)
LRU_BLOCK = 128
LRU_C = 8.0
N_EXPERTS = 8
EPS = 1e-6

LANES = 128
SUBLANES = 8
VMEM_LIMIT = 56 * 1024 * 1024

POOL_TILE = 256
TOK_TILE = 256
FF_CHUNK = 256
MOE_TILE = 1024
MOE_FCHUNK = 512
GATHER_CHUNK = 1024
COMBINE_TILE = 256
LRU_SEGS = 64


def _dot(a, b):
    return jnp.dot(a, b, preferred_element_type=F32)


def _params(*sem):
    return pltpu.CompilerParams(dimension_semantics=sem, vmem_limit_bytes=VMEM_LIMIT)


def _norm_mod(x, g, shift, scale):
    ms = jnp.mean(x * x, axis=-1, keepdims=True)
    return (x * lax.rsqrt(ms + EPS)) * (g * (1.0 + scale)) + shift


def _split_bf16(v):
    hi = v.astype(BF16)
    lo = (v - hi.astype(F32)).astype(BF16)
    return hi, lo


def _ada_kernel(cc_ref, w_ref, b_ref, o_ref):
    s = cc_ref[...]
    s = s * jax.nn.sigmoid(s)
    o_ref[0] = _dot(s.astype(BF16), w_ref[0].astype(BF16)) + b_ref[0]


def _adaln(cc, ada_w, ada_b):
    n_layers, d, n = ada_w.shape
    r = cc.shape[0]
    tn = 1536
    return pl.pallas_call(
        _ada_kernel,
        grid=(n_layers, n // tn),
        in_specs=[pl.BlockSpec((r, d), lambda l, j: (0, 0)),
                  pl.BlockSpec((1, d, tn), lambda l, j: (l, 0, j)),
                  pl.BlockSpec((1, 1, tn), lambda l, j: (l, 0, j))],
        out_specs=pl.BlockSpec((1, r, tn), lambda l, j: (l, 0, j)),
        out_shape=jax.ShapeDtypeStruct((n_layers, r, n), F32),
        compiler_params=_params("parallel", "parallel"),
        name="adaln",
    )(cc, ada_w, ada_b.reshape(n_layers, 1, n))


def _pool_tables(seg_len, gc):
    t = np.arange(POOL_TILE)
    seg, pos = t // seg_len, t % seg_len
    band = np.zeros((len(POOL_WINDOWS), POOL_TILE, POOL_TILE), np.float32)
    invc = np.zeros((len(POOL_WINDOWS), POOL_TILE, gc), np.float32)
    for gi, win in enumerate(POOL_WINDOWS):
        lo = np.clip(pos - win // 2, 0, seg_len)
        hi = np.clip(pos - win // 2 + win, 0, seg_len)
        inside = (pos[None, :] >= lo[:, None]) & (pos[None, :] < hi[:, None])
        band[gi] = (seg[:, None] == seg[None, :]) & inside
        invc[gi] = (1.0 / (hi - lo))[:, None]
    return jnp.asarray(band, BF16), jnp.asarray(invc, F32)


def _layer0_kernel(x_ref, mod_ref, g1_ref, g2_ref, band_ref, invc_ref, pw_ref, ps_ref,
                   wgu_ref, wd_ref, o_ref, act_ref, *, d_ff):
    m = mod_ref[0]
    x = x_ref[0]
    tm, d = x.shape
    n_groups = band_ref.shape[0]
    gc = d // n_groups
    h = _norm_mod(x, g1_ref[...], m[0:1], m[1:2])
    rows = []
    for s in range(tm // POOL_TILE):
        hs = h[s * POOL_TILE:(s + 1) * POOL_TILE]
        cols = []
        for gi in range(n_groups):
            hg = hs[:, gi * gc:(gi + 1) * gc]
            hi, lo = _split_bf16(hg)
            band = band_ref[gi]
            pooled = (_dot(band, hi) + _dot(band, lo)) * invc_ref[gi] - hg
            cols.append(_dot(pooled.astype(BF16), pw_ref[gi]))
        rows.append(jnp.concatenate(cols, axis=1))
    y = rows[0] if len(rows) == 1 else jnp.concatenate(rows, axis=0)
    x1 = x + m[2:3] * (y * ps_ref[...])
    h2 = _norm_mod(x1, g2_ref[...], m[3:4], m[4:5]).astype(BF16)
    for c in range(d_ff // FF_CHUNK):
        g = _dot(h2, wgu_ref[:, c * FF_CHUNK:(c + 1) * FF_CHUNK])
        u = _dot(h2, wgu_ref[:, d_ff + c * FF_CHUNK:d_ff + (c + 1) * FF_CHUNK])
        act_ref[:, c * FF_CHUNK:(c + 1) * FF_CHUNK] = ((g * jax.nn.sigmoid(g)) * u).astype(BF16)
    o_ref[0] = x1 + m[5:6] * _dot(act_ref[...], wd_ref[...])


def _layer0(x, mods, mod_row, g1, g2, band, invc, pw, ps, wgu, wd, name):
    bsz, t, d = x.shape
    d_ff = wd.shape[0]
    tm = TOK_TILE
    const2 = lambda b, i: (0, 0)
    const3 = lambda b, i: (0, 0, 0)
    return pl.pallas_call(
        functools.partial(_layer0_kernel, d_ff=d_ff),
        grid=(bsz, t // tm),
        in_specs=[pl.BlockSpec((1, tm, d), lambda b, i: (b, i, 0)),
                  pl.BlockSpec((1, 6, d), lambda b, i: (mod_row(b), 0, 0)),
                  pl.BlockSpec((1, d), const2),
                  pl.BlockSpec((1, d), const2),
                  pl.BlockSpec(band.shape, const3),
                  pl.BlockSpec(invc.shape, const3),
                  pl.BlockSpec(pw.shape, const3),
                  pl.BlockSpec((1, d), const2),
                  pl.BlockSpec(wgu.shape, const2),
                  pl.BlockSpec(wd.shape, const2)],
        out_specs=pl.BlockSpec((1, tm, d), lambda b, i: (b, i, 0)),
        out_shape=jax.ShapeDtypeStruct(x.shape, F32),
        scratch_shapes=[pltpu.VMEM((tm, d_ff), BF16)],
        compiler_params=_params("parallel", "parallel"),
        name=name,
    )(x, mods, g1, g2, band, invc, pw, ps, wgu, wd)


def _inproj_kernel(x_ref, mod_ref, g1_ref, w_ref, *out_refs):
    m = mod_ref[0]
    h = _norm_mod(x_ref[0], g1_ref[...], m[0:1], m[1:2]).astype(BF16)
    y = _dot(h, w_ref[...])
    n = out_refs[0].shape[-1]
    for k, o_ref in enumerate(out_refs):
        o_ref[0] = y[:, k * n:(k + 1) * n]


def _inproj(x, mods, mod_row, g1, w, n_out, name):
    bsz, t, d = x.shape
    tm = TOK_TILE
    n = w.shape[1] // n_out
    out_shape = jax.ShapeDtypeStruct((bsz, t, n), F32)
    out_spec = pl.BlockSpec((1, tm, n), lambda b, i: (b, i, 0))
    return pl.pallas_call(
        _inproj_kernel,
        grid=(bsz, t // tm),
        in_specs=[pl.BlockSpec((1, tm, d), lambda b, i: (b, i, 0)),
                  pl.BlockSpec((1, 6, d), lambda b, i: (mod_row(b), 0, 0)),
                  pl.BlockSpec((1, d), lambda b, i: (0, 0)),
                  pl.BlockSpec(w.shape, lambda b, i: (0, 0))],
        out_specs=[out_spec] * n_out,
        out_shape=[out_shape] * n_out,
        compiler_params=_params("parallel", "parallel"),
        name=name,
    )(x, mods, g1, w)


def _scan8(a, b, row, reverse):
    for s in (1, 2, 4):
        if reverse:
            keep = row < SUBLANES - s
            shift = SUBLANES - s
        else:
            keep = row >= s
            shift = s
        a_sh = jnp.where(keep, pltpu.roll(a, shift, 0), 1.0)
        b_sh = jnp.where(keep, pltpu.roll(b, shift, 0), 0.0)
        b = a * b_sh + b
        a = a * a_sh
    return a, b


def _sigmoid(v):
    return 0.5 * jnp.tanh(0.5 * v) + 0.5


def _segment_scan(a_ref, b_ref, h_ref, row8, seg_len, reverse):
    nv = LRU_SEGS // SUBLANES

    def rows(k, j):
        jj = seg_len - 1 - j if reverse else j
        return pl.ds(k * SUBLANES * seg_len + jj, SUBLANES, stride=seg_len)

    zero = jnp.zeros((SUBLANES, LANES), F32)
    h = [zero] * nv
    p = [jnp.ones((SUBLANES, LANES), F32)] * nv
    for j in range(seg_len):
        for k in range(nv):
            a = a_ref[rows(k, j), :]
            h[k] = a * h[k] + b_ref[rows(k, j), :]
            p[k] = a * p[k]
    carry = zero
    start = [None] * nv
    for k in (range(nv - 1, -1, -1) if reverse else range(nv)):
        pk, hk = _scan8(p[k], h[k], row8, reverse)
        ends = hk + pk * carry
        if reverse:
            start[k] = jnp.where(row8 == SUBLANES - 1, carry, pltpu.roll(ends, SUBLANES - 1, 0))
            carry = jnp.broadcast_to(ends[0:1], ends.shape)
        else:
            start[k] = jnp.where(row8 == 0, carry, pltpu.roll(ends, 1, 0))
            carry = jnp.broadcast_to(ends[SUBLANES - 1:SUBLANES], ends.shape)
    h = start
    for j in range(seg_len):
        for k in range(nv):
            h[k] = a_ref[rows(k, j), :] * h[k] + b_ref[rows(k, j), :]
            h_ref[rows(k, j), :] = h[k]


def _lru_kernel(uz_ref, ux_ref, gate_ref, cw_ref, cb_ref, wc_ref, bc_ref, lam_ref, o_ref,
                af_ref, bf_ref, ab_ref, bb_ref, hf_ref, hb_ref, *, seg_len):
    lz = uz_ref.shape[1]
    lx = ux_ref.shape[1]
    cw = cw_ref[...]
    cb = cb_ref[...]
    wc = wc_ref[0]
    bc = bc_ref[0]
    neg_lam = -lam_ref[...]
    softplus = jnp.maximum(neg_lam, 0.0) + jnp.log1p(jnp.exp(-jnp.abs(neg_lam)))
    kco = -LRU_C * softplus

    def prep(up, off_f, off_b):
        n = up.shape[0]
        row = lax.broadcasted_iota(jnp.int32, up.shape, 0)
        u = cb + cw[2:3] * up
        u = u + cw[0:1] * jnp.where(row >= 2, pltpu.roll(up, 2, 0), 0.0)
        u = u + cw[1:2] * jnp.where(row >= 1, pltpu.roll(up, 1, 0), 0.0)
        u = u + cw[3:4] * jnp.where(row < n - 1, pltpu.roll(up, n - 1, 0), 0.0)
        g = _dot(u.astype(BF16), wc) + bc
        for d, (a_ref, b_ref, off) in enumerate(((af_ref, bf_ref, off_f), (ab_ref, bb_ref, off_b))):
            r = _sigmoid(g[:, 2 * d * LRU_BLOCK:(2 * d + 1) * LRU_BLOCK])
            i = _sigmoid(g[:, (2 * d + 1) * LRU_BLOCK:(2 * d + 2) * LRU_BLOCK])
            a = jnp.exp(r * kco[d:d + 1])
            a_ref[pl.ds(off, n), :] = a
            b_ref[pl.ds(off, n), :] = jnp.sqrt(1.0 - a * a) * (i * u)

    prep(uz_ref[0], 0, lx)
    prep(ux_ref[0], lz, 0)
    row8 = lax.broadcasted_iota(jnp.int32, (SUBLANES, LANES), 0)
    _segment_scan(af_ref, bf_ref, hf_ref, row8, seg_len, False)
    _segment_scan(ab_ref, bb_ref, hb_ref, row8, seg_len, True)
    hsum = hf_ref[pl.ds(lz, lx), :] + hb_ref[pl.ds(0, lx), :]
    o_ref[0] = (hsum * jax.nn.gelu(gate_ref[0])).astype(BF16)


def _lru(uz, ux, gate, conv_w, conv_b, wcat, bcat, lam):
    bsz, lz, d_rnn = uz.shape
    lx = ux.shape[1]
    heads = d_rnn // LRU_BLOCK
    seg_len = (lz + lx) // LRU_SEGS
    assert seg_len * LRU_SEGS == lz + lx and seg_len % SUBLANES == SUBLANES // 2
    seq = lambda n: pl.BlockSpec((1, n, LRU_BLOCK), lambda b, h: (b, 0, h))
    scratch = pltpu.VMEM((lz + lx, LRU_BLOCK), F32)
    return pl.pallas_call(
        functools.partial(_lru_kernel, seg_len=seg_len),
        grid=(bsz, heads),
        in_specs=[seq(lz), seq(lx), seq(lx),
                  pl.BlockSpec((conv_w.shape[0], LRU_BLOCK), lambda b, h: (0, h)),
                  pl.BlockSpec((1, LRU_BLOCK), lambda b, h: (0, h)),
                  pl.BlockSpec((1, LRU_BLOCK, 4 * LRU_BLOCK), lambda b, h: (h, 0, 0)),
                  pl.BlockSpec((1, 1, 4 * LRU_BLOCK), lambda b, h: (h, 0, 0)),
                  pl.BlockSpec((2, LRU_BLOCK), lambda b, h: (0, h))],
        out_specs=seq(lx),
        out_shape=jax.ShapeDtypeStruct((bsz, lx, d_rnn), BF16),
        scratch_shapes=[scratch] * 6,
        compiler_params=_params("parallel", "parallel"),
        name="rglru",
    )(uz, ux, gate, conv_w, conv_b, wcat, bcat, lam)


def _store_row_tiles(ref, v):
    tm, d = v.shape
    for c in range(d // LANES):
        ref[pl.ds(c, tm, stride=d // LANES), :] = v[:, c * LANES:(c + 1) * LANES]


def _load_row_tiles(ref, tm, d):
    return [ref[pl.ds(c, tm, stride=d // LANES), :] for c in range(d // LANES)]


def _post_kernel(y_ref, x_ref, mod_ref, g2_ref, wo_ref, wrh_ref, wrl_ref, x2_ref, h2t_ref, ri_ref):
    m = mod_ref[0]
    x2 = x_ref[0] + m[2:3] * _dot(y_ref[0], wo_ref[...])
    x2_ref[0] = x2
    h2 = _norm_mod(x2, g2_ref[...], m[3:4], m[4:5])
    _store_row_tiles(h2t_ref, h2)
    hi, lo = _split_bf16(h2)
    logits = _dot(hi, wrh_ref[...]) + (_dot(hi, wrl_ref[...]) + _dot(lo, wrh_ref[...]))
    lane = lax.broadcasted_iota(jnp.int32, logits.shape, 1)
    logits = jnp.where(lane < N_EXPERTS, logits, -1e30)
    e = jnp.exp(logits - jnp.max(logits, axis=-1, keepdims=True))
    p = e / jnp.sum(e, axis=-1, keepdims=True)
    p1 = jnp.max(p, axis=-1, keepdims=True)
    i1 = jnp.min(jnp.where(p == p1, lane, LANES), axis=-1, keepdims=True)
    rest = jnp.where(lane == i1, -1.0, p)
    p2 = jnp.max(rest, axis=-1, keepdims=True)
    i2 = jnp.min(jnp.where(rest == p2, lane, LANES), axis=-1, keepdims=True)
    den = p1 + p2
    ri_ref[...] = jnp.where(lane == 0, i1.astype(F32),
                            jnp.where(lane == 1, i2.astype(F32),
                                      jnp.where(lane == 2, p1 / den,
                                                jnp.where(lane == 3, p2 / den, 0.0))))


def _post(y, x, mods, g2, wo, wr_hi, wr_lo):
    bsz, t, d = x.shape
    d_rnn = y.shape[-1]
    tm = TOK_TILE
    nt = t // tm
    rt = d // LANES
    const2 = lambda b, i: (0, 0)
    return pl.pallas_call(
        _post_kernel,
        grid=(bsz, nt),
        in_specs=[pl.BlockSpec((1, tm, d_rnn), lambda b, i: (b, i, 0)),
                  pl.BlockSpec((1, tm, d), lambda b, i: (b, i, 0)),
                  pl.BlockSpec((1, 6, d), lambda b, i: (b, 0, 0)),
                  pl.BlockSpec((1, d), const2),
                  pl.BlockSpec(wo.shape, const2),
                  pl.BlockSpec(wr_hi.shape, const2),
                  pl.BlockSpec(wr_lo.shape, const2)],
        out_specs=[pl.BlockSpec((1, tm, d), lambda b, i: (b, i, 0)),
                   pl.BlockSpec((tm * rt, LANES), lambda b, i: (b * nt + i, 0)),
                   pl.BlockSpec((tm, LANES), lambda b, i: (b * nt + i, 0))],
        out_shape=[jax.ShapeDtypeStruct(x.shape, F32),
                   jax.ShapeDtypeStruct((bsz * t * rt, LANES), F32),
                   jax.ShapeDtypeStruct((bsz * t, LANES), F32)],
        compiler_params=_params("parallel", "parallel"),
        name="outproj_router",
    )(y, x, mods, g2, wo, wr_hi, wr_lo)


def _dispatch_kernel(idx_ref, src_ref, dst_ref, sem):
    def issue(r, carry):
        src_row = pl.multiple_of(idx_ref[r] * SUBLANES, SUBLANES)
        dst_row = pl.multiple_of(r * SUBLANES, SUBLANES)
        pltpu.make_async_copy(src_ref.at[pl.ds(src_row, SUBLANES), :],
                              dst_ref.at[pl.ds(dst_row, SUBLANES), :], sem).start()
        return carry

    lax.fori_loop(0, GATHER_CHUNK, issue, 0)
    pltpu.make_async_copy(dst_ref, dst_ref, sem).wait()


def _dispatch(slot_tok, h2t):
    n_slots = slot_tok.shape[0]
    return pl.pallas_call(
        _dispatch_kernel,
        grid=(n_slots // GATHER_CHUNK,),
        in_specs=[pl.BlockSpec((GATHER_CHUNK,), lambda i: (i,), memory_space=pltpu.SMEM),
                  pl.BlockSpec(memory_space=pl.ANY)],
        out_specs=pl.BlockSpec((GATHER_CHUNK * SUBLANES, LANES), lambda i: (i, 0)),
        out_shape=jax.ShapeDtypeStruct((n_slots * SUBLANES, LANES), F32),
        scratch_shapes=[pltpu.SemaphoreType.DMA],
        compiler_params=_params("arbitrary"),
        name="moe_dispatch",
    )(slot_tok, h2t)


def _moe_kernel(be_ref, nu_ref, xs_ref, wg_ref, wu_ref, wd_ref, o_ref, xb_ref, acc_ref):
    i = pl.program_id(0)
    j = pl.program_id(1)
    nj = pl.num_programs(1)
    tm, d = xb_ref.shape
    used = i < nu_ref[0]

    @pl.when(used)
    def _():
        @pl.when(j == 0)
        def _():
            for c, v in enumerate(_load_row_tiles(xs_ref, tm, d)):
                xb_ref[:, c * LANES:(c + 1) * LANES] = v.astype(BF16)

        xb = xb_ref[...]
        g = _dot(xb, wg_ref[0])
        u = _dot(xb, wu_ref[0])
        part = _dot(((g * jax.nn.sigmoid(g)) * u).astype(BF16), wd_ref[0])

        @pl.when(j == 0)
        def _():
            acc_ref[...] = part

        @pl.when(j > 0)
        def _():
            acc_ref[...] += part

        @pl.when(j == nj - 1)
        def _():
            _store_row_tiles(o_ref, acc_ref[...])

    @pl.when(jnp.logical_and(jnp.logical_not(used), j == nj - 1))
    def _():
        o_ref[...] = jnp.zeros(o_ref.shape, o_ref.dtype)


def _moe(block_e, n_used, xs, wgu, wd):
    d = wgu.shape[1]
    d_e = wd.shape[1]
    rt = d // LANES
    n_tiles = xs.shape[0] // (MOE_TILE * rt)
    nj = d_e // MOE_FCHUNK

    def jf(i, j, nu):
        return jnp.where(i < nu[0], j, nj - 1)

    def it(i, nu):
        return jnp.minimum(i, jnp.maximum(nu[0] - 1, 0))

    return pl.pallas_call(
        _moe_kernel,
        grid_spec=pltpu.PrefetchScalarGridSpec(
            num_scalar_prefetch=2,
            grid=(n_tiles, nj),
            in_specs=[pl.BlockSpec((MOE_TILE * rt, LANES), lambda i, j, be, nu: (it(i, nu), 0)),
                      pl.BlockSpec((1, d, MOE_FCHUNK), lambda i, j, be, nu: (be[i], 0, jf(i, j, nu))),
                      pl.BlockSpec((1, d, MOE_FCHUNK), lambda i, j, be, nu: (be[i], 0, nj + jf(i, j, nu))),
                      pl.BlockSpec((1, MOE_FCHUNK, d), lambda i, j, be, nu: (be[i], jf(i, j, nu), 0))],
            out_specs=pl.BlockSpec((MOE_TILE * rt, LANES), lambda i, j, be, nu: (i, 0)),
            scratch_shapes=[pltpu.VMEM((MOE_TILE, d), BF16), pltpu.VMEM((MOE_TILE, d), F32)]),
        out_shape=jax.ShapeDtypeStruct(xs.shape, F32),
        compiler_params=_params("arbitrary", "arbitrary"),
        name="moe_experts",
    )(block_e, n_used, xs, wgu, wgu, wd)


def _combine_kernel(dest_ref, x_ref, ri_ref, mod_ref, fg_ref, ys_ref, o_ref, buf_a, buf_b, sem):
    tm, d = x_ref.shape[1], x_ref.shape[2]

    def issue(r, carry):
        dst_row = pl.multiple_of(r * SUBLANES, SUBLANES)
        for k, buf in enumerate((buf_a, buf_b)):
            src_row = pl.multiple_of(dest_ref[2 * r + k] * SUBLANES, SUBLANES)
            pltpu.make_async_copy(ys_ref.at[pl.ds(src_row, SUBLANES), :],
                                  buf.at[pl.ds(dst_row, SUBLANES), :], sem.at[k]).start()
        return carry

    lax.fori_loop(0, tm, issue, 0)
    for k, buf in enumerate((buf_a, buf_b)):
        pltpu.make_async_copy(buf, buf, sem.at[k]).wait()
    ri = ri_ref[...]
    w1 = ri[:, 2:3]
    w2 = ri[:, 3:4]
    ya = _load_row_tiles(buf_a, tm, d)
    yb = _load_row_tiles(buf_b, tm, d)
    moe = jnp.concatenate([w1 * a + w2 * b for a, b in zip(ya, yb)], axis=1)
    x3 = x_ref[0] + mod_ref[0][5:6] * moe
    ms = jnp.mean(x3 * x3, axis=-1, keepdims=True)
    o_ref[0] = (x3 * lax.rsqrt(ms + EPS)) * fg_ref[...]


def _combine(dest, x2, ri, mods, fg, ys):
    bsz, t, d = x2.shape
    tm = COMBINE_TILE
    nt = t // tm
    rt = d // LANES
    return pl.pallas_call(
        _combine_kernel,
        grid=(bsz, nt),
        in_specs=[pl.BlockSpec((2 * tm,), lambda b, i: (b * nt + i,), memory_space=pltpu.SMEM),
                  pl.BlockSpec((1, tm, d), lambda b, i: (b, i, 0)),
                  pl.BlockSpec((tm, LANES), lambda b, i: (b * nt + i, 0)),
                  pl.BlockSpec((1, 6, d), lambda b, i: (b, 0, 0)),
                  pl.BlockSpec((1, d), lambda b, i: (0, 0)),
                  pl.BlockSpec(memory_space=pl.ANY)],
        out_specs=pl.BlockSpec((1, tm, d), lambda b, i: (b, i, 0)),
        out_shape=jax.ShapeDtypeStruct(x2.shape, F32),
        scratch_shapes=[pltpu.VMEM((tm * rt, LANES), F32), pltpu.VMEM((tm * rt, LANES), F32),
                        pltpu.SemaphoreType.DMA((2,))],
        compiler_params=_params("arbitrary", "arbitrary"),
        name="moe_combine_norm",
    )(dest, x2, ri, mods, fg, ys)


def _routing(top_e, n_slots):
    n2 = top_e.size
    flat_e = top_e.reshape(-1)
    onehot = (flat_e[:, None] == jnp.arange(N_EXPERTS, dtype=jnp.int32)[None, :]).astype(jnp.int32)
    csum = jnp.cumsum(onehot, axis=0)
    counts = csum[-1]
    rank = jnp.take_along_axis(csum, flat_e[:, None], axis=1)[:, 0] - 1
    padded = (counts + MOE_TILE - 1) // MOE_TILE * MOE_TILE
    pends = jnp.cumsum(padded)
    pstarts = pends - padded
    starts = jnp.cumsum(counts) - counts
    dest = (pstarts[flat_e] + rank).astype(jnp.int32)
    order = jnp.argsort(flat_e, stable=True).astype(jnp.int32)
    slot = jnp.arange(n_slots, dtype=jnp.int32)
    slot_e = jnp.minimum(jnp.searchsorted(pends, slot, side='right'), N_EXPERTS - 1).astype(jnp.int32)
    r = slot - pstarts[slot_e]
    valid = r < counts[slot_e]
    src = order[jnp.clip(starts[slot_e] + r, 0, n2 - 1)] // top_e.shape[1]
    slot_tok = jnp.where(valid, src, 0).astype(jnp.int32)
    n_tiles = n_slots // MOE_TILE
    tile_e = jnp.minimum(jnp.searchsorted(pends, jnp.arange(n_tiles, dtype=jnp.int32) * MOE_TILE, side='right'),
                         N_EXPERTS - 1).astype(jnp.int32)
    n_used = (pends[-1] // MOE_TILE).astype(jnp.int32).reshape(1)
    return dest, slot_tok, tile_e, n_used


def kernel(x, c, ctx, c_ctx, ada_w, ada_b, norm1_g, norm2_g, pool_w, pool_scale, ffn_w_gu, ffn_w_down,
           lru_w_in, lru_conv_w, lru_conv_b, lru_w_r, lru_b_r, lru_w_i, lru_b_i, lru_lambda, lru_w_out,
           moe_w_router, moe_w_gu, moe_w_down, final_g):
    bsz, seq, d = x.shape
    assert ada_w.shape[0] == 2 and seq % TOK_TILE == 0 and ctx.shape[1] == POOL_TILE
    assert d == SUBLANES * LANES
    d_rnn = lru_w_out.shape[1]
    heads = d_rnn // LRU_BLOCK
    gc = d // len(POOL_WINDOWS)
    z_row = bsz

    n_rows = (bsz + 1 + SUBLANES - 1) // SUBLANES * SUBLANES
    cc = jnp.concatenate([c, c_ctx[None, :], jnp.zeros((n_rows - bsz - 1, d), F32)], axis=0)
    mods = _adaln(cc, ada_w, ada_b).reshape(2, n_rows, 6, d)
    x_row = lambda b: b
    ctx_row = lambda b: z_row

    band_x, invc_x = _pool_tables(GRID_W, gc)
    band_z, invc_z = _pool_tables(POOL_TILE, gc)
    l0 = (norm1_g[0:1], norm2_g[0:1])
    l0w = (pool_w[0].astype(BF16), pool_scale[0:1], ffn_w_gu[0].astype(BF16), ffn_w_down[0].astype(BF16))
    x1 = _layer0(x, mods[0], x_row, *l0, band_x, invc_x, *l0w, name="layer0_latent")
    z1 = _layer0(ctx, mods[0], ctx_row, *l0, band_z, invc_z, *l0w, name="layer0_context")

    w_in = lru_w_in[0].astype(BF16)
    gate_x, ux = _inproj(x1, mods[1], x_row, norm1_g[1:2], w_in, 2, name="lru_inproj_latent")
    (uz,) = _inproj(z1, mods[1], ctx_row, norm1_g[1:2], w_in[:, d_rnn:], 1, name="lru_inproj_context")
    wcat = jnp.concatenate([lru_w_r[0, 0], lru_w_i[0, 0], lru_w_r[0, 1], lru_w_i[0, 1]], axis=-1).astype(BF16)
    bcat = jnp.concatenate([v.reshape(heads, 1, LRU_BLOCK)
                            for v in (lru_b_r[0, 0], lru_b_i[0, 0], lru_b_r[0, 1], lru_b_i[0, 1])], axis=-1)
    y = _lru(uz, ux, gate_x, lru_conv_w[0], lru_conv_b[0:1], wcat, bcat, lru_lambda[0])

    wr = jnp.pad(moe_w_router[0], ((0, 0), (0, LANES - N_EXPERTS)))
    wr_hi = wr.astype(BF16)
    wr_lo = (wr - wr_hi.astype(F32)).astype(BF16)
    x2, h2t, ri = _post(y, x1, mods[1], norm2_g[1:2], lru_w_out[0].astype(BF16), wr_hi, wr_lo)

    n_tok = bsz * seq
    top_e = ri[:, 0:2].astype(jnp.int32)
    n_slots = 2 * n_tok + N_EXPERTS * MOE_TILE
    dest, slot_tok, tile_e, n_used = _routing(top_e, n_slots)
    xs = _dispatch(slot_tok, h2t)
    ys = _moe(tile_e, n_used, xs, moe_w_gu[0].astype(BF16), moe_w_down[0].astype(BF16))
    return _combine(dest, x2, ri, mods[1], final_g[None, :], ys)
```

```python
import functools

import numpy as np
import jax
import jax.numpy as jnp
from jax import lax
from jax.experimental import pallas as pl
from jax.experimental.pallas import tpu as pltpu

F32 = jnp.float32
BF16 = jnp.bfloat16

GRID_W = 64
POOL_WINDOWS = (2, 4, 8, 16)
LRU_BLOCK = 128
LRU_C = 8.0
N_EXPERTS = 8
EPS = 1e-6
LOG2_E = 1.4426950408889634

LANES = 128
SUBLANES = 8
VMEM_LIMIT = 56 * 1024 * 1024

POOL_TILE = 256
TOK_TILE = 256
FF_CHUNK = 256
MOE_TILE = 1024
MOE_FCHUNK = 1792
MOE_SUBCHUNK = 256
DMA_UNROLL = 8
COMBINE_TILE = 256
LRU_SEGS = 64


def _dot(a, b):
    return jnp.dot(a, b, preferred_element_type=F32)


def _params(*sem):
    return pltpu.CompilerParams(dimension_semantics=sem, vmem_limit_bytes=VMEM_LIMIT)


def _norm_mod(x, g, shift, scale):
    ms = jnp.mean(x * x, axis=-1, keepdims=True)
    return (x * lax.rsqrt(ms + EPS)) * (g * (1.0 + scale)) + shift


def _split_bf16(v):
    hi = v.astype(BF16)
    lo = (v - hi.astype(F32)).astype(BF16)
    return hi, lo


def _ada_kernel(cc_ref, w_ref, b_ref, o_ref):
    s = cc_ref[...]
    s = s * jax.nn.sigmoid(s)
    o_ref[0] = _dot(s.astype(BF16), w_ref[0].astype(BF16)) + b_ref[0]


def _adaln(cc, ada_w, ada_b):
    n_layers, d, n = ada_w.shape
    r = cc.shape[0]
    tn = 1536
    return pl.pallas_call(
        _ada_kernel,
        grid=(n_layers, n // tn),
        in_specs=[pl.BlockSpec((r, d), lambda l, j: (0, 0)),
                  pl.BlockSpec((1, d, tn), lambda l, j: (l, 0, j)),
                  pl.BlockSpec((1, 1, tn), lambda l, j: (l, 0, j))],
        out_specs=pl.BlockSpec((1, r, tn), lambda l, j: (l, 0, j)),
        out_shape=jax.ShapeDtypeStruct((n_layers, r, n), F32),
        compiler_params=_params("parallel", "parallel"),
        name="adaln",
    )(cc, ada_w, ada_b.reshape(n_layers, 1, n))


def _pool_tables(seg_len, gc):
    t = np.arange(POOL_TILE)
    seg, pos = t // seg_len, t % seg_len
    band = np.zeros((len(POOL_WINDOWS), POOL_TILE, POOL_TILE), np.float32)
    invc = np.zeros((len(POOL_WINDOWS), POOL_TILE, gc), np.float32)
    for gi, win in enumerate(POOL_WINDOWS):
        lo = np.clip(pos - win // 2, 0, seg_len)
        hi = np.clip(pos - win // 2 + win, 0, seg_len)
        inside = (pos[None, :] >= lo[:, None]) & (pos[None, :] < hi[:, None])
        band[gi] = (seg[:, None] == seg[None, :]) & inside
        invc[gi] = (1.0 / (hi - lo))[:, None]
    return jnp.asarray(band, BF16), jnp.asarray(invc, F32)


def _layer0_kernel(x_ref, mod_ref, g1_ref, g2_ref, band_ref, invc_ref, pw_ref, ps_ref,
                   wgu_ref, wd_ref, o_ref, act_ref, *, d_ff):
    m = mod_ref[0]
    x = x_ref[0]
    tm, d = x.shape
    n_groups = band_ref.shape[0]
    gc = d // n_groups
    h = _norm_mod(x, g1_ref[...], m[0:1], m[1:2])
    rows = []
    for s in range(tm // POOL_TILE):
        hs = h[s * POOL_TILE:(s + 1) * POOL_TILE]
        cols = []
        for gi in range(n_groups):
            hg = hs[:, gi * gc:(gi + 1) * gc]
            hi, lo = _split_bf16(hg)
            band = band_ref[gi]
            pooled = (_dot(band, hi) + _dot(band, lo)) * invc_ref[gi] - hg
            cols.append(_dot(pooled.astype(BF16), pw_ref[gi]))
        rows.append(jnp.concatenate(cols, axis=1))
    y = rows[0] if len(rows) == 1 else jnp.concatenate(rows, axis=0)
    x1 = x + m[2:3] * (y * ps_ref[...])
    h2 = _norm_mod(x1, g2_ref[...], m[3:4], m[4:5]).astype(BF16)
    for c in range(d_ff // FF_CHUNK):
        g = _dot(h2, wgu_ref[:, c * FF_CHUNK:(c + 1) * FF_CHUNK])
        u = _dot(h2, wgu_ref[:, d_ff + c * FF_CHUNK:d_ff + (c + 1) * FF_CHUNK])
        act_ref[:, c * FF_CHUNK:(c + 1) * FF_CHUNK] = ((g * jax.nn.sigmoid(g)) * u).astype(BF16)
    o_ref[0] = x1 + m[5:6] * _dot(act_ref[...], wd_ref[...])


def _layer0(x, mods, mod_row, g1, g2, band, invc, pw, ps, wgu, wd, name):
    bsz, t, d = x.shape
    d_ff = wd.shape[0]
    tm = TOK_TILE
    const2 = lambda b, i: (0, 0)
    const3 = lambda b, i: (0, 0, 0)
    return pl.pallas_call(
        functools.partial(_layer0_kernel, d_ff=d_ff),
        grid=(bsz, t // tm),
        in_specs=[pl.BlockSpec((1, tm, d), lambda b, i: (b, i, 0)),
                  pl.BlockSpec((1, 6, d), lambda b, i: (mod_row(b), 0, 0)),
                  pl.BlockSpec((1, d), const2),
                  pl.BlockSpec((1, d), const2),
                  pl.BlockSpec(band.shape, const3),
                  pl.BlockSpec(invc.shape, const3),
                  pl.BlockSpec(pw.shape, const3),
                  pl.BlockSpec((1, d), const2),
                  pl.BlockSpec(wgu.shape, const2),
                  pl.BlockSpec(wd.shape, const2)],
        out_specs=pl.BlockSpec((1, tm, d), lambda b, i: (b, i, 0)),
        out_shape=jax.ShapeDtypeStruct(x.shape, F32),
        scratch_shapes=[pltpu.VMEM((tm, d_ff), BF16)],
        compiler_params=_params("parallel", "parallel"),
        name=name,
    )(x, mods, g1, g2, band, invc, pw, ps, wgu, wd)


def _inproj_kernel(x_ref, mod_ref, g1_ref, w_ref, *out_refs, with_gate):
    m = mod_ref[0]
    h = _norm_mod(x_ref[0], g1_ref[...], m[0:1], m[1:2]).astype(BF16)
    y = _dot(h, w_ref[...])
    n = out_refs[0].shape[-1]
    if with_gate:
        out_refs[0][0] = jax.nn.gelu(y[:, :n])
        out_refs[1][0] = y[:, n:]
    else:
        out_refs[0][0] = y


def _inproj(x, mods, mod_row, g1, w, with_gate, name):
    bsz, t, d = x.shape
    tm = TOK_TILE
    n_out = 2 if with_gate else 1
    n = w.shape[1] // n_out
    out_shape = jax.ShapeDtypeStruct((bsz, t, n), F32)
    out_spec = pl.BlockSpec((1, tm, n), lambda b, i: (b, i, 0))
    return pl.pallas_call(
        functools.partial(_inproj_kernel, with_gate=with_gate),
        grid=(bsz, t // tm),
        in_specs=[pl.BlockSpec((1, tm, d), lambda b, i: (b, i, 0)),
                  pl.BlockSpec((1, 6, d), lambda b, i: (mod_row(b), 0, 0)),
                  pl.BlockSpec((1, d), lambda b, i: (0, 0)),
                  pl.BlockSpec(w.shape, lambda b, i: (0, 0))],
        out_specs=[out_spec] * n_out,
        out_shape=[out_shape] * n_out,
        compiler_params=_params("parallel", "parallel"),
        name=name,
    )(x, mods, g1, w)


def _scan8(a, b, row, reverse):
    for s in (1, 2, 4):
        if reverse:
            keep = row < SUBLANES - s
            shift = SUBLANES - s
        else:
            keep = row >= s
            shift = s
        a_sh = jnp.where(keep, pltpu.roll(a, shift, 0), 1.0)
        b_sh = jnp.where(keep, pltpu.roll(b, shift, 0), 0.0)
        b = a * b_sh + b
        a = a * a_sh
    return a, b


def _segment_scan(a_ref, b_ref, h_ref, row8, seg_len, reverse):
    nv = LRU_SEGS // SUBLANES

    def rows(k, j):
        jj = seg_len - 1 - j if reverse else j
        return pl.ds(k * SUBLANES * seg_len + jj, SUBLANES, stride=seg_len)

    zero = jnp.zeros((SUBLANES, LANES), F32)
    h = [zero] * nv
    p = [jnp.ones((SUBLANES, LANES), F32)] * nv
    for j in range(seg_len):
        for k in range(nv):
            a = a_ref[rows(k, j), :]
            h[k] = a * h[k] + b_ref[rows(k, j), :]
            p[k] = a * p[k]
    carry = zero
    start = [None] * nv
    for k in (range(nv - 1, -1, -1) if reverse else range(nv)):
        pk, hk = _scan8(p[k], h[k], row8, reverse)
        ends = hk + pk * carry
        if reverse:
            start[k] = jnp.where(row8 == SUBLANES - 1, carry, pltpu.roll(ends, SUBLANES - 1, 0))
            carry = jnp.broadcast_to(ends[0:1], ends.shape)
        else:
            start[k] = jnp.where(row8 == 0, carry, pltpu.roll(ends, 1, 0))
            carry = jnp.broadcast_to(ends[SUBLANES - 1:SUBLANES], ends.shape)
    h = start
    for j in range(seg_len):
        for k in range(nv):
            h[k] = a_ref[rows(k, j), :] * h[k] + b_ref[rows(k, j), :]
            h_ref[rows(k, j), :] = h[k]


def _lru_kernel(uz_ref, ux_ref, gate_ref, cw_ref, cb_ref, wc_ref, bc_ref, lam_ref, o_ref,
                af_ref, bf_ref, ab_ref, bb_ref, hf_ref, hb_ref, pad_ref, *, seg_len):
    lz = uz_ref.shape[1]
    lx = ux_ref.shape[1]
    cw = cw_ref[...]
    cb = cb_ref[...]
    wc = wc_ref[0]
    bc = bc_ref[0]
    neg_lam = -lam_ref[...]
    softplus = jnp.maximum(neg_lam, 0.0) + jnp.log1p(jnp.exp(-jnp.abs(neg_lam)))
    kh = (-0.5 * LRU_C * LOG2_E) * softplus
    zeros8 = jnp.zeros((SUBLANES, LANES), F32)

    def prep(src_ref, off_f, off_b):
        n = src_ref.shape[1]
        pad_ref[pl.ds(0, SUBLANES), :] = zeros8
        pad_ref[pl.ds(SUBLANES, n), :] = src_ref[0]
        pad_ref[pl.ds(SUBLANES + n, SUBLANES), :] = zeros8
        u = cb + cw[2:3] * src_ref[0]
        for k in (0, 1, 3):
            u = u + cw[k:k + 1] * pad_ref[pl.ds(SUBLANES + k - 2, n), :]
        t = jnp.tanh(_dot(u.astype(BF16), wc) + bc)
        hu = 0.5 * u
        for d, (a_ref, b_ref, off) in enumerate(((af_ref, bf_ref, off_f), (ab_ref, bb_ref, off_b))):
            t_r = t[:, 2 * d * LRU_BLOCK:(2 * d + 1) * LRU_BLOCK]
            t_i = t[:, (2 * d + 1) * LRU_BLOCK:(2 * d + 2) * LRU_BLOCK]
            a = jnp.exp2(kh[d:d + 1] * t_r + kh[d:d + 1])
            y = 1.0 - a * a
            root = jnp.where(y > 0.0, y * lax.rsqrt(y), 0.0)
            a_ref[pl.ds(off, n), :] = a
            b_ref[pl.ds(off, n), :] = root * (hu * t_i + hu)

    prep(uz_ref, 0, lx)
    prep(ux_ref, lz, 0)
    row8 = lax.broadcasted_iota(jnp.int32, (SUBLANES, LANES), 0)
    _segment_scan(af_ref, bf_ref, hf_ref, row8, seg_len, False)
    _segment_scan(ab_ref, bb_ref, hb_ref, row8, seg_len, True)
    hsum = hf_ref[pl.ds(lz, lx), :] + hb_ref[pl.ds(0, lx), :]
    o_ref[0] = (hsum * gate_ref[0]).astype(BF16)


def _lru(uz, ux, gate, conv_w, conv_b, wcat, bcat, lam):
    bsz, lz, d_rnn = uz.shape
    lx = ux.shape[1]
    heads = d_rnn // LRU_BLOCK
    seg_len = (lz + lx) // LRU_SEGS
    assert seg_len * LRU_SEGS == lz + lx and seg_len % SUBLANES == SUBLANES // 2
    seq = lambda n: pl.BlockSpec((1, n, LRU_BLOCK), lambda b, h: (b, 0, h))
    scratch = pltpu.VMEM((lz + lx, LRU_BLOCK), F32)
    return pl.pallas_call(
        functools.partial(_lru_kernel, seg_len=seg_len),
        grid=(bsz, heads),
        in_specs=[seq(lz), seq(lx), seq(lx),
                  pl.BlockSpec((conv_w.shape[0], LRU_BLOCK), lambda b, h: (0, h)),
                  pl.BlockSpec((1, LRU_BLOCK), lambda b, h: (0, h)),
                  pl.BlockSpec((1, LRU_BLOCK, 4 * LRU_BLOCK), lambda b, h: (h, 0, 0)),
                  pl.BlockSpec((1, 1, 4 * LRU_BLOCK), lambda b, h: (h, 0, 0)),
                  pl.BlockSpec((2, LRU_BLOCK), lambda b, h: (0, h))],
        out_specs=seq(lx),
        out_shape=jax.ShapeDtypeStruct((bsz, lx, d_rnn), BF16),
        scratch_shapes=[scratch] * 6 + [pltpu.VMEM((lx + 2 * SUBLANES, LRU_BLOCK), F32)],
        compiler_params=_params("parallel", "parallel"),
        name="rglru",
    )(uz, ux, gate, conv_w, conv_b, wcat, bcat, lam)


def _store_row_tiles(ref, v):
    tm, d = v.shape
    for c in range(d // LANES):
        ref[pl.ds(c, tm, stride=d // LANES), :] = v[:, c * LANES:(c + 1) * LANES]


def _load_row_tiles(ref, tm, d):
    return [ref[pl.ds(c, tm, stride=d // LANES), :] for c in range(d // LANES)]


def _post_kernel(y_ref, x_ref, mod_ref, g2_ref, wo_ref, wrh_ref, wrl_ref, x2_ref, h2t_ref, ri_ref, rt_ref):
    m = mod_ref[0]
    x2 = x_ref[0] + m[2:3] * _dot(y_ref[0], wo_ref[...])
    x2_ref[0] = x2
    h2 = _norm_mod(x2, g2_ref[...], m[3:4], m[4:5])
    _store_row_tiles(h2t_ref, h2)
    hi, lo = _split_bf16(h2)
    logits = _dot(hi, wrh_ref[...]) + (_dot(hi, wrl_ref[...]) + _dot(lo, wrh_ref[...]))
    lane = lax.broadcasted_iota(jnp.int32, logits.shape, 1)
    logits = jnp.where(lane < N_EXPERTS, logits, -1e30)
    e = jnp.exp(logits - jnp.max(logits, axis=-1, keepdims=True))
    p = e / jnp.sum(e, axis=-1, keepdims=True)
    p1 = jnp.max(p, axis=-1, keepdims=True)
    i1 = jnp.min(jnp.where(p == p1, lane, LANES), axis=-1, keepdims=True)
    rest = jnp.where(lane == i1, -1.0, p)
    p2 = jnp.max(rest, axis=-1, keepdims=True)
    i2 = jnp.min(jnp.where(rest == p2, lane, LANES), axis=-1, keepdims=True)
    den = p1 + p2
    ri = jnp.where(lane == 0, i1.astype(F32),
                   jnp.where(lane == 1, i2.astype(F32),
                             jnp.where(lane == 2, p1 / den,
                                       jnp.where(lane == 3, p2 / den, 0.0))))
    ri_ref[...] = ri
    rt_ref[...] = jnp.transpose(ri)[0:SUBLANES, :]


def _post(y, x, mods, g2, wo, wr_hi, wr_lo):
    bsz, t, d = x.shape
    d_rnn = y.shape[-1]
    tm = TOK_TILE
    nt = t // tm
    rt = d // LANES
    const2 = lambda b, i: (0, 0)
    return pl.pallas_call(
        _post_kernel,
        grid=(bsz, nt),
        in_specs=[pl.BlockSpec((1, tm, d_rnn), lambda b, i: (b, i, 0)),
                  pl.BlockSpec((1, tm, d), lambda b, i: (b, i, 0)),
                  pl.BlockSpec((1, 6, d), lambda b, i: (b, 0, 0)),
                  pl.BlockSpec((1, d), const2),
                  pl.BlockSpec(wo.shape, const2),
                  pl.BlockSpec(wr_hi.shape, const2),
                  pl.BlockSpec(wr_lo.shape, const2)],
        out_specs=[pl.BlockSpec((1, tm, d), lambda b, i: (b, i, 0)),
                   pl.BlockSpec((tm * rt, LANES), lambda b, i: (b * nt + i, 0)),
                   pl.BlockSpec((tm, LANES), lambda b, i: (b * nt + i, 0)),
                   pl.BlockSpec((SUBLANES, tm), lambda b, i: (0, b * nt + i))],
        out_shape=[jax.ShapeDtypeStruct(x.shape, F32),
                   jax.ShapeDtypeStruct((bsz * t * rt, LANES), F32),
                   jax.ShapeDtypeStruct((bsz * t, LANES), F32),
                   jax.ShapeDtypeStruct((SUBLANES, bsz * t), F32)],
        compiler_params=_params("parallel", "parallel"),
        name="outproj_router",
    )(y, x, mods, g2, wo, wr_hi, wr_lo)


def _moe_kernel(te_ref, nu_ref, cur_ref, nxt_ref, dst_ref, h2t_ref, wg_ref, wu_ref, wd_ref, y2_ref,
                xs_buf, xb_ref, act_ref, acc_ref, out_buf, gsem, ssem):
    i = pl.program_id(0)
    j = pl.program_id(1)
    n_tiles = pl.num_programs(0)
    nj = pl.num_programs(1)
    tm, d = xb_ref.shape
    tf = act_ref.shape[1]
    nu = nu_ref[0]
    used = i < nu
    first = jnp.logical_and(i == 0, j == 0)

    def gather(idx_ref, slot):
        def issue(r, carry):
            src_row = pl.multiple_of(idx_ref[r] * SUBLANES, SUBLANES)
            dst_row = pl.multiple_of(r * SUBLANES, SUBLANES)
            pltpu.make_async_copy(h2t_ref.at[pl.ds(src_row, SUBLANES), :],
                                  xs_buf.at[slot, pl.ds(dst_row, SUBLANES), :], gsem.at[slot]).start()
            return carry
        lax.fori_loop(0, tm, issue, 0, unroll=DMA_UNROLL)

    def scatter_copy_all():
        return pltpu.make_async_copy(out_buf, out_buf, ssem)

    @pl.when(first)
    def _():
        out_buf[...] = jnp.zeros(out_buf.shape, out_buf.dtype)
        junk = y2_ref.at[pl.ds(y2_ref.shape[0] - tm * SUBLANES, tm * SUBLANES), :]
        init = pltpu.make_async_copy(out_buf, junk, ssem)
        init.start()
        init.wait()
        gather(cur_ref, 0)

    for slot in range(2):
        @pl.when(jnp.logical_and(jnp.logical_and(used, j == 0), i % 2 == slot))
        def _(slot=slot):
            pltpu.make_async_copy(xs_buf.at[slot], xs_buf.at[slot], gsem.at[slot]).wait()
            for c in range(d // LANES):
                xb_ref[:, c * LANES:(c + 1) * LANES] = (
                    xs_buf[slot, pl.ds(c, tm, stride=d // LANES), :].astype(BF16))

            @pl.when(i + 1 < nu)
            def _():
                gather(nxt_ref, 1 - slot)

    @pl.when(used)
    def _():
        xb = xb_ref[...]
        for c in range(tf // MOE_SUBCHUNK):
            cs = slice(c * MOE_SUBCHUNK, (c + 1) * MOE_SUBCHUNK)
            g = _dot(xb, wg_ref[0, :, cs])
            u = _dot(xb, wu_ref[0, :, cs])
            act_ref[:, cs] = ((g * jax.nn.sigmoid(g)) * u).astype(BF16)
        part = _dot(act_ref[...], wd_ref[0])

        @pl.when(j == 0)
        def _():
            acc_ref[...] = part

        @pl.when(j > 0)
        def _():
            acc_ref[...] += part

        @pl.when(j == nj - 1)
        def _():
            @pl.when(i > 0)
            def _():
                scatter_copy_all().wait()
            _store_row_tiles(out_buf, acc_ref[...])

            def issue(r, carry):
                src_row = pl.multiple_of(r * SUBLANES, SUBLANES)
                dst_row = pl.multiple_of(dst_ref[r] * SUBLANES, SUBLANES)
                pltpu.make_async_copy(out_buf.at[pl.ds(src_row, SUBLANES), :],
                                      y2_ref.at[pl.ds(dst_row, SUBLANES), :], ssem).start()
                return carry
            lax.fori_loop(0, tm, issue, 0, unroll=DMA_UNROLL)

    @pl.when(jnp.logical_and(i == n_tiles - 1, j == nj - 1))
    def _():
        scatter_copy_all().wait()


def _moe(tile_e, n_used, slot_tok, slot_dst, h2t, wgu, wd, n_tok):
    d = wgu.shape[1]
    d_e = wd.shape[1]
    rt = d // LANES
    tm = MOE_TILE
    n_tiles = slot_tok.shape[0] // tm
    nj = d_e // MOE_FCHUNK

    def jf(i, j, nu):
        return jnp.where(i < nu[0], j, nj - 1)

    smem_tile = lambda f: pl.BlockSpec((tm,), f, memory_space=pltpu.SMEM)
    return pl.pallas_call(
        _moe_kernel,
        grid_spec=pltpu.PrefetchScalarGridSpec(
            num_scalar_prefetch=2,
            grid=(n_tiles, nj),
            in_specs=[smem_tile(lambda i, j, te, nu: (i,)),
                      smem_tile(lambda i, j, te, nu: (jnp.minimum(i + 1, n_tiles - 1),)),
                      smem_tile(lambda i, j, te, nu: (i,)),
                      pl.BlockSpec(memory_space=pl.ANY),
                      pl.BlockSpec((1, d, MOE_FCHUNK), lambda i, j, te, nu: (te[i], 0, jf(i, j, nu))),
                      pl.BlockSpec((1, d, MOE_FCHUNK), lambda i, j, te, nu: (te[i], 0, nj + jf(i, j, nu))),
                      pl.BlockSpec((1, MOE_FCHUNK, d), lambda i, j, te, nu: (te[i], jf(i, j, nu), 0))],
            out_specs=pl.BlockSpec(memory_space=pl.ANY),
            scratch_shapes=[pltpu.VMEM((2, tm * rt, LANES), F32),
                            pltpu.VMEM((tm, d), BF16),
                            pltpu.VMEM((tm, MOE_FCHUNK), BF16),
                            pltpu.VMEM((tm, d), F32),
                            pltpu.VMEM((tm * rt, LANES), F32),
                            pltpu.SemaphoreType.DMA((2,)),
                            pltpu.SemaphoreType.DMA]),
        out_shape=jax.ShapeDtypeStruct(((2 * n_tok + tm) * rt, LANES), F32),
        compiler_params=_params("arbitrary", "arbitrary"),
        name="moe_experts",
    )(tile_e, n_used, slot_tok, slot_tok, slot_dst, h2t, wgu, wgu, wd)


def _combine_kernel(x_ref, ri_ref, mod_ref, fg_ref, ya_ref, yb_ref, o_ref):
    tm, d = x_ref.shape[1], x_ref.shape[2]
    ri = ri_ref[...]
    w1 = ri[:, 2:3]
    w2 = ri[:, 3:4]
    ya = _load_row_tiles(ya_ref, tm, d)
    yb = _load_row_tiles(yb_ref, tm, d)
    moe = jnp.concatenate([w1 * a + w2 * b for a, b in zip(ya, yb)], axis=1)
    x3 = x_ref[0] + mod_ref[0][5:6] * moe
    ms = jnp.mean(x3 * x3, axis=-1, keepdims=True)
    o_ref[0] = (x3 * lax.rsqrt(ms + EPS)) * fg_ref[...]


def _combine(x2, ri, mods, fg, y2):
    bsz, t, d = x2.shape
    tm = COMBINE_TILE
    nt = t // tm
    rt = d // LANES
    n_blocks = bsz * nt
    return pl.pallas_call(
        _combine_kernel,
        grid=(bsz, nt),
        in_specs=[pl.BlockSpec((1, tm, d), lambda b, i: (b, i, 0)),
                  pl.BlockSpec((tm, LANES), lambda b, i: (b * nt + i, 0)),
                  pl.BlockSpec((1, 6, d), lambda b, i: (b, 0, 0)),
                  pl.BlockSpec((1, d), lambda b, i: (0, 0)),
                  pl.BlockSpec((tm * rt, LANES), lambda b, i: (b * nt + i, 0)),
                  pl.BlockSpec((tm * rt, LANES), lambda b, i: (n_blocks + b * nt + i, 0))],
        out_specs=pl.BlockSpec((1, tm, d), lambda b, i: (b, i, 0)),
        out_shape=jax.ShapeDtypeStruct(x2.shape, F32),
        compiler_params=_params("parallel", "parallel"),
        name="moe_combine_norm",
    )(x2, ri, mods, fg, y2, y2)


def _routing(flat_e, n_tok):
    n2 = flat_e.shape[0]
    experts = jnp.arange(N_EXPERTS, dtype=jnp.int32)

    def pick(vec, idx):
        return jnp.sum(jnp.where(idx[:, None] == experts[None, :], vec[None, :], 0), axis=1)

    counts = jnp.sum((flat_e[:, None] == experts[None, :]).astype(jnp.int32), axis=0)
    padded = (counts + MOE_TILE - 1) // MOE_TILE * MOE_TILE
    pends = jnp.cumsum(padded)
    pstarts = pends - padded
    starts = jnp.cumsum(counts) - counts
    order = jnp.argsort(flat_e, stable=True).astype(jnp.int32)
    n_tiles = n2 // MOE_TILE + N_EXPERTS
    slot = jnp.arange(n_tiles * MOE_TILE, dtype=jnp.int32)
    slot_e = jnp.minimum(jnp.sum((slot[:, None] >= pends[None, :]).astype(jnp.int32), axis=1), N_EXPERTS - 1)
    r = slot - pick(pstarts, slot_e)
    valid = r < pick(counts, slot_e)
    f = order[jnp.clip(pick(starts, slot_e) + r, 0, n2 - 1)]
    slot_tok = jnp.where(valid, f % n_tok, 0).astype(jnp.int32)
    slot_dst = jnp.where(valid, f, n2 + slot % MOE_TILE).astype(jnp.int32)
    tile0 = jnp.arange(n_tiles, dtype=jnp.int32) * MOE_TILE
    tile_e = jnp.minimum(jnp.sum((tile0[:, None] >= pends[None, :]).astype(jnp.int32), axis=1), N_EXPERTS - 1)
    n_used = (pends[-1] // MOE_TILE).astype(jnp.int32).reshape(1)
    return slot_tok, slot_dst, tile_e.astype(jnp.int32), n_used


def kernel(x, c, ctx, c_ctx, ada_w, ada_b, norm1_g, norm2_g, pool_w, pool_scale, ffn_w_gu, ffn_w_down,
           lru_w_in, lru_conv_w, lru_conv_b, lru_w_r, lru_b_r, lru_w_i, lru_b_i, lru_lambda, lru_w_out,
           moe_w_router, moe_w_gu, moe_w_down, final_g):
    bsz, seq, d = x.shape
    assert ada_w.shape[0] == 2 and seq % TOK_TILE == 0 and ctx.shape[1] == POOL_TILE
    assert d == SUBLANES * LANES
    d_rnn = lru_w_out.shape[1]
    heads = d_rnn // LRU_BLOCK
    gc = d // len(POOL_WINDOWS)
    z_row = bsz

    n_rows = (bsz + 1 + SUBLANES - 1) // SUBLANES * SUBLANES
    cc = jnp.concatenate([c, c_ctx[None, :], jnp.zeros((n_rows - bsz - 1, d), F32)], axis=0)
    mods = _adaln(cc, ada_w, ada_b).reshape(2, n_rows, 6, d)
    x_row = lambda b: b
    ctx_row = lambda b: z_row

    band_x, invc_x = _pool_tables(GRID_W, gc)
    band_z, invc_z = _pool_tables(POOL_TILE, gc)
    l0 = (norm1_g[0:1], norm2_g[0:1])
    l0w = (pool_w[0].astype(BF16), pool_scale[0:1], ffn_w_gu[0].astype(BF16), ffn_w_down[0].astype(BF16))
    x1 = _layer0(x, mods[0], x_row, *l0, band_x, invc_x, *l0w, name="layer0_latent")
    z1 = _layer0(ctx, mods[0], ctx_row, *l0, band_z, invc_z, *l0w, name="layer0_context")

    w_in = lru_w_in[0].astype(BF16)
    gate_x, ux = _inproj(x1, mods[1], x_row, norm1_g[1:2], w_in, True, name="lru_inproj_latent")
    (uz,) = _inproj(z1, mods[1], ctx_row, norm1_g[1:2], w_in[:, d_rnn:], False, name="lru_inproj_context")
    wcat = (0.5 * jnp.concatenate([lru_w_r[0, 0], lru_w_i[0, 0], lru_w_r[0, 1], lru_w_i[0, 1]], axis=-1)).astype(BF16)
    bcat = 0.5 * jnp.concatenate([v.reshape(heads, 1, LRU_BLOCK)
                                  for v in (lru_b_r[0, 0], lru_b_i[0, 0], lru_b_r[0, 1], lru_b_i[0, 1])], axis=-1)
    y = _lru(uz, ux, gate_x, lru_conv_w[0], lru_conv_b[0:1], wcat, bcat, lru_lambda[0])

    wr = jnp.pad(moe_w_router[0], ((0, 0), (0, LANES - N_EXPERTS)))
    wr_hi = wr.astype(BF16)
    wr_lo = (wr - wr_hi.astype(F32)).astype(BF16)
    x2, h2t, ri, rt = _post(y, x1, mods[1], norm2_g[1:2], lru_w_out[0].astype(BF16), wr_hi, wr_lo)

    n_tok = bsz * seq
    flat_e = rt[0:2].reshape(-1).astype(jnp.int32)
    slot_tok, slot_dst, tile_e, n_used = _routing(flat_e, n_tok)
    y2 = _moe(tile_e, n_used, slot_tok, slot_dst, h2t, moe_w_gu[0].astype(BF16), moe_w_down[0].astype(BF16), n_tok)
    return _combine(x2, ri, mods[1], final_g[None, :], y2)
```

```python
import functools

import numpy as np
import jax
import jax.numpy as jnp
from jax import lax
from jax.experimental import pallas as pl
from jax.experimental.pallas import tpu as pltpu

F32 = jnp.float32
BF16 = jnp.bfloat16

GRID_W = 64
POOL_WINDOWS = (2, 4, 8, 16)
LRU_BLOCK = 128
LRU_C = 8.0
N_EXPERTS = 8
EPS = 1e-6
LOG2_E = 1.4426950408889634

LANES = 128
SUBLANES = 8
VMEM_LIMIT = 56 * 1024 * 1024

POOL_TILE = 256
TOK_TILE = 512
FF_CHUNK = 256
MOE_TILE = 1024
MOE_FCHUNK = 1792
MOE_SUBCHUNK = 256
DMA_UNROLL = 8
POST_TILE = 256
COMBINE_TILE = 256
LRU_SEGS = 64


def _dot(a, b):
    return jnp.dot(a, b, preferred_element_type=F32)


def _params(*sem):
    return pltpu.CompilerParams(dimension_semantics=sem, vmem_limit_bytes=VMEM_LIMIT)


def _resident(shape):
    return pl.BlockSpec(shape, lambda *_: (0,) * len(shape), pipeline_mode=pl.Buffered(1))


def _norm_mod(x, g, shift, scale):
    ms = jnp.mean(x * x, axis=-1, keepdims=True)
    return (x * lax.rsqrt(ms + EPS)) * (g * (1.0 + scale)) + shift


def _split_bf16(v):
    hi = v.astype(BF16)
    lo = (v - hi.astype(F32)).astype(BF16)
    return hi, lo


def _ada_kernel(cc_ref, w_ref, b_ref, o_ref):
    s = cc_ref[...]
    s = s * jax.nn.sigmoid(s)
    o_ref[0] = _dot(s.astype(BF16), w_ref[0].astype(BF16)) + b_ref[0]


def _adaln(cc, ada_w, ada_b):
    n_layers, d, n = ada_w.shape
    r = cc.shape[0]
    tn = 1536
    return pl.pallas_call(
        _ada_kernel,
        grid=(n_layers, n // tn),
        in_specs=[pl.BlockSpec((r, d), lambda l, j: (0, 0)),
                  pl.BlockSpec((1, d, tn), lambda l, j: (l, 0, j)),
                  pl.BlockSpec((1, 1, tn), lambda l, j: (l, 0, j))],
        out_specs=pl.BlockSpec((1, r, tn), lambda l, j: (l, 0, j)),
        out_shape=jax.ShapeDtypeStruct((n_layers, r, n), F32),
        compiler_params=_params("parallel", "parallel"),
        name="adaln",
    )(cc, ada_w, ada_b.reshape(n_layers, 1, n))


def _pool_tables(seg_len, gc):
    t = np.arange(POOL_TILE)
    seg, pos = t // seg_len, t % seg_len
    band = np.zeros((len(POOL_WINDOWS), POOL_TILE, POOL_TILE), np.float32)
    invc = np.zeros((len(POOL_WINDOWS), POOL_TILE, gc), np.float32)
    for gi, win in enumerate(POOL_WINDOWS):
        lo = np.clip(pos - win // 2, 0, seg_len)
        hi = np.clip(pos - win // 2 + win, 0, seg_len)
        inside = (pos[None, :] >= lo[:, None]) & (pos[None, :] < hi[:, None])
        band[gi] = (seg[:, None] == seg[None, :]) & inside
        invc[gi] = (1.0 / (hi - lo))[:, None]
    return jnp.asarray(band, BF16), jnp.asarray(invc, F32)


def _layer0_kernel(x_ref, mod_ref, g1_ref, g2_ref, band_ref, invc_ref, pw_ref, ps_ref,
                   wgu_ref, wd_ref, o_ref, act_ref, *, d_ff):
    m = mod_ref[0]
    x = x_ref[0]
    tm, d = x.shape
    n_groups = band_ref.shape[0]
    gc = d // n_groups
    h = _norm_mod(x, g1_ref[...], m[0:1], m[1:2])
    rows = []
    for s in range(tm // POOL_TILE):
        hs = h[s * POOL_TILE:(s + 1) * POOL_TILE]
        cols = []
        for gi in range(n_groups):
            hg = hs[:, gi * gc:(gi + 1) * gc]
            hi, lo = _split_bf16(hg)
            band = band_ref[gi]
            pooled = (_dot(band, hi) + _dot(band, lo)) * invc_ref[gi] - hg
            cols.append(_dot(pooled.astype(BF16), pw_ref[gi]))
        rows.append(jnp.concatenate(cols, axis=1))
    y = rows[0] if len(rows) == 1 else jnp.concatenate(rows, axis=0)
    x1 = x + m[2:3] * (y * ps_ref[...])
    h2 = _norm_mod(x1, g2_ref[...], m[3:4], m[4:5]).astype(BF16)
    for c in range(d_ff // FF_CHUNK):
        g = _dot(h2, wgu_ref[:, c * FF_CHUNK:(c + 1) * FF_CHUNK])
        u = _dot(h2, wgu_ref[:, d_ff + c * FF_CHUNK:d_ff + (c + 1) * FF_CHUNK])
        act_ref[:, c * FF_CHUNK:(c + 1) * FF_CHUNK] = ((g * jax.nn.sigmoid(g)) * u).astype(BF16)
    o_ref[0] = x1 + m[5:6] * _dot(act_ref[...], wd_ref[...])


def _layer0(x, mods, mod_row, g1, g2, band, invc, pw, ps, wgu, wd, name):
    bsz, t, d = x.shape
    d_ff = wd.shape[0]
    tm = min(TOK_TILE, t)
    return pl.pallas_call(
        functools.partial(_layer0_kernel, d_ff=d_ff),
        grid=(bsz, t // tm),
        in_specs=[pl.BlockSpec((1, tm, d), lambda b, i: (b, i, 0)),
                  pl.BlockSpec((1, 6, d), lambda b, i: (mod_row(b), 0, 0)),
                  _resident((1, d)), _resident((1, d)),
                  _resident(band.shape), _resident(invc.shape), _resident(pw.shape), _resident((1, d)),
                  _resident(wgu.shape), _resident(wd.shape)],
        out_specs=pl.BlockSpec((1, tm, d), lambda b, i: (b, i, 0)),
        out_shape=jax.ShapeDtypeStruct(x.shape, F32),
        scratch_shapes=[pltpu.VMEM((tm, d_ff), BF16)],
        compiler_params=_params("parallel", "parallel"),
        name=name,
    )(x, mods, g1, g2, band, invc, pw, ps, wgu, wd)


def _inproj_kernel(x_ref, mod_ref, g1_ref, w_ref, *out_refs, with_gate):
    m = mod_ref[0]
    h = _norm_mod(x_ref[0], g1_ref[...], m[0:1], m[1:2]).astype(BF16)
    y = _dot(h, w_ref[...])
    n = out_refs[0].shape[-1]
    if with_gate:
        out_refs[0][0] = jax.nn.gelu(y[:, :n])
        out_refs[1][0] = y[:, n:]
    else:
        out_refs[0][0] = y


def _inproj(x, mods, mod_row, g1, w, with_gate, name):
    bsz, t, d = x.shape
    tm = min(TOK_TILE, t)
    n_out = 2 if with_gate else 1
    n = w.shape[1] // n_out
    out_shape = jax.ShapeDtypeStruct((bsz, t, n), F32)
    out_spec = pl.BlockSpec((1, tm, n), lambda b, i: (b, i, 0))
    return pl.pallas_call(
        functools.partial(_inproj_kernel, with_gate=with_gate),
        grid=(bsz, t // tm),
        in_specs=[pl.BlockSpec((1, tm, d), lambda b, i: (b, i, 0)),
                  pl.BlockSpec((1, 6, d), lambda b, i: (mod_row(b), 0, 0)),
                  _resident((1, d)), _resident(w.shape)],
        out_specs=[out_spec] * n_out,
        out_shape=[out_shape] * n_out,
        compiler_params=_params("parallel", "parallel"),
        name=name,
    )(x, mods, g1, w)


def _scan8(a, b, row, reverse):
    for s in (1, 2, 4):
        if reverse:
            keep = row < SUBLANES - s
            shift = SUBLANES - s
        else:
            keep = row >= s
            shift = s
        a_sh = jnp.where(keep, pltpu.roll(a, shift, 0), 1.0)
        b_sh = jnp.where(keep, pltpu.roll(b, shift, 0), 0.0)
        b = a * b_sh + b
        a = a * a_sh
    return a, b


def _segment_scan(a_ref, b_ref, h_ref, row8, seg_len, reverse):
    nv = LRU_SEGS // SUBLANES

    def rows(k, j):
        jj = seg_len - 1 - j if reverse else j
        return pl.ds(k * SUBLANES * seg_len + jj, SUBLANES, stride=seg_len)

    zero = jnp.zeros((SUBLANES, LANES), F32)
    h = [zero] * nv
    p = [jnp.ones((SUBLANES, LANES), F32)] * nv
    for j in range(seg_len):
        for k in range(nv):
            a = a_ref[rows(k, j), :]
            h[k] = a * h[k] + b_ref[rows(k, j), :]
            p[k] = a * p[k]
    carry = zero
    start = [None] * nv
    for k in (range(nv - 1, -1, -1) if reverse else range(nv)):
        pk, hk = _scan8(p[k], h[k], row8, reverse)
        ends = hk + pk * carry
        if reverse:
            start[k] = jnp.where(row8 == SUBLANES - 1, carry, pltpu.roll(ends, SUBLANES - 1, 0))
            carry = jnp.broadcast_to(ends[0:1], ends.shape)
        else:
            start[k] = jnp.where(row8 == 0, carry, pltpu.roll(ends, 1, 0))
            carry = jnp.broadcast_to(ends[SUBLANES - 1:SUBLANES], ends.shape)
    h = start
    for j in range(seg_len):
        for k in range(nv):
            h[k] = a_ref[rows(k, j), :] * h[k] + b_ref[rows(k, j), :]
            h_ref[rows(k, j), :] = h[k]


def _lru_kernel(uz_ref, ux_ref, gate_ref, cw_ref, cb_ref, wc_ref, bc_ref, lam_ref, o_ref,
                af_ref, bf_ref, ab_ref, bb_ref, hf_ref, hb_ref, pad_ref, *, seg_len):
    lz = uz_ref.shape[1]
    lx = ux_ref.shape[1]
    cw = cw_ref[...]
    cb = cb_ref[...]
    wc = wc_ref[0]
    bc = bc_ref[0]
    neg_lam = -lam_ref[...]
    softplus = jnp.maximum(neg_lam, 0.0) + jnp.log1p(jnp.exp(-jnp.abs(neg_lam)))
    kh = (-0.5 * LRU_C * LOG2_E) * softplus
    zeros8 = jnp.zeros((SUBLANES, LANES), F32)

    def prep(src_ref, off_f, off_b):
        n = src_ref.shape[1]
        pad_ref[pl.ds(0, SUBLANES), :] = zeros8
        pad_ref[pl.ds(SUBLANES, n), :] = src_ref[0]
        pad_ref[pl.ds(SUBLANES + n, SUBLANES), :] = zeros8
        u = cb + cw[2:3] * src_ref[0]
        for k in (0, 1, 3):
            u = u + cw[k:k + 1] * pad_ref[pl.ds(SUBLANES + k - 2, n), :]
        t = jnp.tanh(_dot(u.astype(BF16), wc) + bc)
        hu = 0.5 * u
        for d, (a_ref, b_ref, off) in enumerate(((af_ref, bf_ref, off_f), (ab_ref, bb_ref, off_b))):
            t_r = t[:, 2 * d * LRU_BLOCK:(2 * d + 1) * LRU_BLOCK]
            t_i = t[:, (2 * d + 1) * LRU_BLOCK:(2 * d + 2) * LRU_BLOCK]
            a = jnp.exp2(kh[d:d + 1] * t_r + kh[d:d + 1])
            y = 1.0 - a * a
            root = jnp.where(y > 0.0, y * lax.rsqrt(y), 0.0)
            a_ref[pl.ds(off, n), :] = a
            b_ref[pl.ds(off, n), :] = root * (hu * t_i + hu)

    prep(uz_ref, 0, lx)
    prep(ux_ref, lz, 0)
    row8 = lax.broadcasted_iota(jnp.int32, (SUBLANES, LANES), 0)
    _segment_scan(af_ref, bf_ref, hf_ref, row8, seg_len, False)
    _segment_scan(ab_ref, bb_ref, hb_ref, row8, seg_len, True)
    hsum = hf_ref[pl.ds(lz, lx), :] + hb_ref[pl.ds(0, lx), :]
    o_ref[0] = (hsum * gate_ref[0]).astype(BF16)


def _lru(uz, ux, gate, conv_w, conv_b, wcat, bcat, lam):
    bsz, lz, d_rnn = uz.shape
    lx = ux.shape[1]
    heads = d_rnn // LRU_BLOCK
    seg_len = (lz + lx) // LRU_SEGS
    assert seg_len * LRU_SEGS == lz + lx and seg_len % SUBLANES == SUBLANES // 2
    seq = lambda n: pl.BlockSpec((1, n, LRU_BLOCK), lambda b, h: (b, 0, h))
    scratch = pltpu.VMEM((lz + lx, LRU_BLOCK), F32)
    return pl.pallas_call(
        functools.partial(_lru_kernel, seg_len=seg_len),
        grid=(bsz, heads),
        in_specs=[seq(lz), seq(lx), seq(lx),
                  pl.BlockSpec((conv_w.shape[0], LRU_BLOCK), lambda b, h: (0, h)),
                  pl.BlockSpec((1, LRU_BLOCK), lambda b, h: (0, h)),
                  pl.BlockSpec((1, LRU_BLOCK, 4 * LRU_BLOCK), lambda b, h: (h, 0, 0)),
                  pl.BlockSpec((1, 1, 4 * LRU_BLOCK), lambda b, h: (h, 0, 0)),
                  pl.BlockSpec((2, LRU_BLOCK), lambda b, h: (0, h))],
        out_specs=seq(lx),
        out_shape=jax.ShapeDtypeStruct((bsz, lx, d_rnn), BF16),
        scratch_shapes=[scratch] * 6 + [pltpu.VMEM((lx + 2 * SUBLANES, LRU_BLOCK), F32)],
        compiler_params=_params("parallel", "parallel"),
        name="rglru",
    )(uz, ux, gate, conv_w, conv_b, wcat, bcat, lam)


def _store_row_tiles(ref, v):
    tm, d = v.shape
    for c in range(d // LANES):
        ref[pl.ds(c, tm, stride=d // LANES), :] = v[:, c * LANES:(c + 1) * LANES]


def _load_row_tiles(ref, tm, d):
    return [ref[pl.ds(c, tm, stride=d // LANES), :] for c in range(d // LANES)]


def _post_kernel(y_ref, x_ref, mod_ref, g2_ref, wo_ref, wrh_ref, wrl_ref, x2_ref, h2t_ref, ri_ref, rt_ref):
    m = mod_ref[0]
    x2 = x_ref[0] + m[2:3] * _dot(y_ref[0], wo_ref[...])
    x2_ref[0] = x2
    h2 = _norm_mod(x2, g2_ref[...], m[3:4], m[4:5])
    _store_row_tiles(h2t_ref, h2)
    hi, lo = _split_bf16(h2)
    logits = _dot(hi, wrh_ref[...]) + (_dot(hi, wrl_ref[...]) + _dot(lo, wrh_ref[...]))
    lane = lax.broadcasted_iota(jnp.int32, logits.shape, 1)
    logits = jnp.where(lane < N_EXPERTS, logits, -1e30)
    e = jnp.exp(logits - jnp.max(logits, axis=-1, keepdims=True))
    p = e / jnp.sum(e, axis=-1, keepdims=True)
    p1 = jnp.max(p, axis=-1, keepdims=True)
    i1 = jnp.min(jnp.where(p == p1, lane, LANES), axis=-1, keepdims=True)
    rest = jnp.where(lane == i1, -1.0, p)
    p2 = jnp.max(rest, axis=-1, keepdims=True)
    i2 = jnp.min(jnp.where(rest == p2, lane, LANES), axis=-1, keepdims=True)
    den = p1 + p2
    ri = jnp.where(lane == 0, i1.astype(F32),
                   jnp.where(lane == 1, i2.astype(F32),
                             jnp.where(lane == 2, p1 / den,
                                       jnp.where(lane == 3, p2 / den, 0.0))))
    ri_ref[...] = ri
    rt_ref[...] = jnp.transpose(ri)[0:SUBLANES, :]


def _post(y, x, mods, g2, wo, wr_hi, wr_lo):
    bsz, t, d = x.shape
    d_rnn = y.shape[-1]
    tm = min(POST_TILE, t)
    nt = t // tm
    rt = d // LANES
    return pl.pallas_call(
        _post_kernel,
        grid=(bsz, nt),
        in_specs=[pl.BlockSpec((1, tm, d_rnn), lambda b, i: (b, i, 0)),
                  pl.BlockSpec((1, tm, d), lambda b, i: (b, i, 0)),
                  pl.BlockSpec((1, 6, d), lambda b, i: (b, 0, 0)),
                  _resident((1, d)), _resident(wo.shape), _resident(wr_hi.shape), _resident(wr_lo.shape)],
        out_specs=[pl.BlockSpec((1, tm, d), lambda b, i: (b, i, 0)),
                   pl.BlockSpec((tm * rt, LANES), lambda b, i: (b * nt + i, 0)),
                   pl.BlockSpec((tm, LANES), lambda b, i: (b * nt + i, 0)),
                   pl.BlockSpec((SUBLANES, tm), lambda b, i: (0, b * nt + i))],
        out_shape=[jax.ShapeDtypeStruct(x.shape, F32),
                   jax.ShapeDtypeStruct((bsz * t * rt, LANES), F32),
                   jax.ShapeDtypeStruct((bsz * t, LANES), F32),
                   jax.ShapeDtypeStruct((SUBLANES, bsz * t), F32)],
        compiler_params=_params("parallel", "parallel"),
        name="outproj_router",
    )(y, x, mods, g2, wo, wr_hi, wr_lo)


def _moe_kernel(te_ref, nu_ref, first_ref, nxt_ref, prev_ref, h2t_ref, wg_ref, wu_ref, wd_ref, y2_ref,
                xs_buf, xb_ref, act_ref, acc_ref, out_buf, gsem, ssem, *, n_steps):
    i = pl.program_id(0)
    j = pl.program_id(1)
    tm, d = xb_ref.shape
    tf = act_ref.shape[1]
    rt = d // LANES
    nu = nu_ref[0]
    used = i < nu
    slot = i % 2

    def gather_row(idx_ref, buf_slot, r):
        src_row = pl.multiple_of(idx_ref[r] * rt, rt)
        dst_row = pl.multiple_of(r * rt, rt)
        pltpu.make_async_copy(h2t_ref.at[pl.ds(src_row, rt), :],
                              xs_buf.at[buf_slot, pl.ds(dst_row, rt), :], gsem.at[buf_slot]).start()

    def scatter_row(r):
        src_row = pl.multiple_of(r * rt, rt)
        dst_row = pl.multiple_of(prev_ref[r] * rt, rt)
        pltpu.make_async_copy(out_buf.at[pl.ds(src_row, rt), :],
                              y2_ref.at[pl.ds(dst_row, rt), :], ssem).start()

    def wait_scatter():
        pltpu.make_async_copy(out_buf, out_buf, ssem).wait()

    @pl.when(jnp.logical_and(i == 0, j == 0))
    def _():
        out_buf[...] = jnp.zeros(out_buf.shape, out_buf.dtype)

        def issue(r, carry):
            gather_row(first_ref, 0, r)
            return carry
        lax.fori_loop(0, tm, issue, 0, unroll=DMA_UNROLL)

    @pl.when(jnp.logical_and(j == 0, i <= nu))
    def _():
        for s in range(2):
            @pl.when(slot == s)
            def _(s=s):
                pltpu.make_async_copy(xs_buf.at[s], xs_buf.at[s], gsem.at[s]).wait()

                @pl.when(used)
                def _():
                    for c in range(rt):
                        xb_ref[:, c * LANES:(c + 1) * LANES] = (
                            xs_buf[s, pl.ds(c, tm, stride=rt), :].astype(BF16))

        @pl.when(i == nu)
        def _():
            def issue(r, carry):
                scatter_row(r)
                return carry
            lax.fori_loop(0, tm, issue, 0, unroll=DMA_UNROLL)
            wait_scatter()

    @pl.when(used)
    def _():
        xb = xb_ref[...]
        n_sub = tf // MOE_SUBCHUNK
        rows_step = tm // n_steps
        per = -(-rows_step // n_sub)
        base = j * rows_step
        for c in range(n_sub):
            cs = slice(c * MOE_SUBCHUNK, (c + 1) * MOE_SUBCHUNK)
            g = _dot(xb, wg_ref[0, :, cs])
            u = _dot(xb, wu_ref[0, :, cs])
            act_ref[:, cs] = ((g * jax.nn.sigmoid(g)) * u).astype(BF16)
            for r in range(c * per, min((c + 1) * per, rows_step)):
                gather_row(nxt_ref, 1 - slot, base + r)
                scatter_row(base + r)
        part = _dot(act_ref[...], wd_ref[0])

        @pl.when(j == 0)
        def _():
            acc_ref[...] = part

        @pl.when(j > 0)
        def _():
            acc_ref[...] += part

        @pl.when(j == n_steps - 1)
        def _():
            wait_scatter()
            _store_row_tiles(out_buf, acc_ref[...])


def _moe(tile_e, n_used, slot_tok, prev_dst, h2t, wgu, wd, n_tok):
    d = wgu.shape[1]
    d_e = wd.shape[1]
    rt = d // LANES
    tm = MOE_TILE
    n_tiles = slot_tok.shape[0] // tm
    nj = d_e // MOE_FCHUNK

    def jf(i, j, nu):
        return jnp.where(i < nu[0], j, nj - 1)

    smem_tile = lambda f: pl.BlockSpec((tm,), f, memory_space=pltpu.SMEM)
    return pl.pallas_call(
        functools.partial(_moe_kernel, n_steps=nj),
        grid_spec=pltpu.PrefetchScalarGridSpec(
            num_scalar_prefetch=2,
            grid=(n_tiles, nj),
            in_specs=[smem_tile(lambda i, j, te, nu: (0,)),
                      smem_tile(lambda i, j, te, nu: (jnp.minimum(i + 1, n_tiles - 1),)),
                      smem_tile(lambda i, j, te, nu: (i,)),
                      pl.BlockSpec(memory_space=pl.ANY),
                      pl.BlockSpec((1, d, MOE_FCHUNK), lambda i, j, te, nu: (te[i], 0, jf(i, j, nu))),
                      pl.BlockSpec((1, d, MOE_FCHUNK), lambda i, j, te, nu: (te[i], 0, nj + jf(i, j, nu))),
                      pl.BlockSpec((1, MOE_FCHUNK, d), lambda i, j, te, nu: (te[i], jf(i, j, nu), 0))],
            out_specs=pl.BlockSpec(memory_space=pl.ANY),
            scratch_shapes=[pltpu.VMEM((2, tm * rt, LANES), F32),
                            pltpu.VMEM((tm, d), BF16),
                            pltpu.VMEM((tm, MOE_FCHUNK), BF16),
                            pltpu.VMEM((tm, d), F32),
                            pltpu.VMEM((tm * rt, LANES), F32),
                            pltpu.SemaphoreType.DMA((2,)),
                            pltpu.SemaphoreType.DMA]),
        out_shape=jax.ShapeDtypeStruct(((2 * n_tok + tm) * rt, LANES), F32),
        compiler_params=_params("arbitrary", "arbitrary"),
        name="moe_experts",
    )(tile_e, n_used, slot_tok, slot_tok, prev_dst, h2t, wgu, wgu, wd)


def _combine_kernel(x_ref, ri_ref, mod_ref, fg_ref, ya_ref, yb_ref, o_ref):
    tm, d = x_ref.shape[1], x_ref.shape[2]
    ri = ri_ref[...]
    w1 = ri[:, 2:3]
    w2 = ri[:, 3:4]
    ya = _load_row_tiles(ya_ref, tm, d)
    yb = _load_row_tiles(yb_ref, tm, d)
    moe = jnp.concatenate([w1 * a + w2 * b for a, b in zip(ya, yb)], axis=1)
    x3 = x_ref[0] + mod_ref[0][5:6] * moe
    ms = jnp.mean(x3 * x3, axis=-1, keepdims=True)
    o_ref[0] = (x3 * lax.rsqrt(ms + EPS)) * fg_ref[...]


def _combine(x2, ri, mods, fg, y2):
    bsz, t, d = x2.shape
    tm = COMBINE_TILE
    nt = t // tm
    rt = d // LANES
    n_blocks = bsz * nt
    return pl.pallas_call(
        _combine_kernel,
        grid=(bsz, nt),
        in_specs=[pl.BlockSpec((1, tm, d), lambda b, i: (b, i, 0)),
                  pl.BlockSpec((tm, LANES), lambda b, i: (b * nt + i, 0)),
                  pl.BlockSpec((1, 6, d), lambda b, i: (b, 0, 0)),
                  pl.BlockSpec((1, d), lambda b, i: (0, 0)),
                  pl.BlockSpec((tm * rt, LANES), lambda b, i: (b * nt + i, 0)),
                  pl.BlockSpec((tm * rt, LANES), lambda b, i: (n_blocks + b * nt + i, 0))],
        out_specs=pl.BlockSpec((1, tm, d), lambda b, i: (b, i, 0)),
        out_shape=jax.ShapeDtypeStruct(x2.shape, F32),
        compiler_params=_params("parallel", "parallel"),
        name="moe_combine_norm",
    )(x2, ri, mods, fg, y2, y2)


def _routing(flat_e, n_tok):
    n2 = flat_e.shape[0]
    experts = jnp.arange(N_EXPERTS, dtype=jnp.int32)

    def pick(vec, idx):
        return jnp.sum(jnp.where(idx[:, None] == experts[None, :], vec[None, :], 0), axis=1)

    counts = jnp.sum((flat_e[:, None] == experts[None, :]).astype(jnp.int32), axis=0)
    padded = (counts + MOE_TILE - 1) // MOE_TILE * MOE_TILE
    pends = jnp.cumsum(padded)
    pstarts = pends - padded
    starts = jnp.cumsum(counts) - counts
    order = jnp.argsort(flat_e, stable=True).astype(jnp.int32)
    n_tiles = n2 // MOE_TILE + N_EXPERTS + 1
    slot = jnp.arange(n_tiles * MOE_TILE, dtype=jnp.int32)
    slot_e = jnp.minimum(jnp.sum((slot[:, None] >= pends[None, :]).astype(jnp.int32), axis=1), N_EXPERTS - 1)
    r = slot - pick(pstarts, slot_e)
    valid = r < pick(counts, slot_e)
    f = order[jnp.clip(pick(starts, slot_e) + r, 0, n2 - 1)]
    slot_tok = jnp.where(valid, f % n_tok, 0).astype(jnp.int32)
    slot_dst = jnp.where(valid, f, n2 + slot % MOE_TILE).astype(jnp.int32)
    prev_dst = jnp.concatenate([n2 + jnp.arange(MOE_TILE, dtype=jnp.int32), slot_dst[:-MOE_TILE]])
    tile0 = jnp.arange(n_tiles, dtype=jnp.int32) * MOE_TILE
    tile_e = jnp.minimum(jnp.sum((tile0[:, None] >= pends[None, :]).astype(jnp.int32), axis=1), N_EXPERTS - 1)
    n_used = (pends[-1] // MOE_TILE).astype(jnp.int32).reshape(1)
    return slot_tok, prev_dst, tile_e.astype(jnp.int32), n_used


def kernel(x, c, ctx, c_ctx, ada_w, ada_b, norm1_g, norm2_g, pool_w, pool_scale, ffn_w_gu, ffn_w_down,
           lru_w_in, lru_conv_w, lru_conv_b, lru_w_r, lru_b_r, lru_w_i, lru_b_i, lru_lambda, lru_w_out,
           moe_w_router, moe_w_gu, moe_w_down, final_g):
    bsz, seq, d = x.shape
    assert ada_w.shape[0] == 2 and seq % TOK_TILE == 0 and TOK_TILE % POOL_TILE == 0 and ctx.shape[1] == POOL_TILE
    assert d == SUBLANES * LANES
    d_rnn = lru_w_out.shape[1]
    heads = d_rnn // LRU_BLOCK
    gc = d // len(POOL_WINDOWS)
    z_row = bsz

    n_rows = (bsz + 1 + SUBLANES - 1) // SUBLANES * SUBLANES
    cc = jnp.concatenate([c, c_ctx[None, :], jnp.zeros((n_rows - bsz - 1, d), F32)], axis=0)
    mods = _adaln(cc, ada_w, ada_b).reshape(2, n_rows, 6, d)
    x_row = lambda b: b
    ctx_row = lambda b: z_row

    band_x, invc_x = _pool_tables(GRID_W, gc)
    band_z, invc_z = _pool_tables(POOL_TILE, gc)
    l0 = (norm1_g[0:1], norm2_g[0:1])
    l0w = (pool_w[0].astype(BF16), pool_scale[0:1], ffn_w_gu[0].astype(BF16), ffn_w_down[0].astype(BF16))
    x1 = _layer0(x, mods[0], x_row, *l0, band_x, invc_x, *l0w, name="layer0_latent")
    z1 = _layer0(ctx, mods[0], ctx_row, *l0, band_z, invc_z, *l0w, name="layer0_context")

    w_in = lru_w_in[0].astype(BF16)
    gate_x, ux = _inproj(x1, mods[1], x_row, norm1_g[1:2], w_in, True, name="lru_inproj_latent")
    (uz,) = _inproj(z1, mods[1], ctx_row, norm1_g[1:2], w_in[:, d_rnn:], False, name="lru_inproj_context")
    wcat = (0.5 * jnp.concatenate([lru_w_r[0, 0], lru_w_i[0, 0], lru_w_r[0, 1], lru_w_i[0, 1]], axis=-1)).astype(BF16)
    bcat = 0.5 * jnp.concatenate([v.reshape(heads, 1, LRU_BLOCK)
                                  for v in (lru_b_r[0, 0], lru_b_i[0, 0], lru_b_r[0, 1], lru_b_i[0, 1])], axis=-1)
    y = _lru(uz, ux, gate_x, lru_conv_w[0], lru_conv_b[0:1], wcat, bcat, lru_lambda[0])

    wr = jnp.pad(moe_w_router[0], ((0, 0), (0, LANES - N_EXPERTS)))
    wr_hi = wr.astype(BF16)
    wr_lo = (wr - wr_hi.astype(F32)).astype(BF16)
    x2, h2t, ri, rt = _post(y, x1, mods[1], norm2_g[1:2], lru_w_out[0].astype(BF16), wr_hi, wr_lo)

    n_tok = bsz * seq
    flat_e = rt[0:2].reshape(-1).astype(jnp.int32)
    slot_tok, prev_dst, tile_e, n_used = _routing(flat_e, n_tok)
    y2 = _moe(tile_e, n_used, slot_tok, prev_dst, h2t, moe_w_gu[0].astype(BF16), moe_w_down[0].astype(BF16), n_tok)
    return _combine(x2, ri, mods[1], final_g[None, :], y2)
```

```python
import functools

import numpy as np
import jax
import jax.numpy as jnp
from jax import lax
from jax.experimental import pallas as pl
from jax.experimental.pallas import tpu as pltpu

F32 = jnp.float32
BF16 = jnp.bfloat16

GRID_W = 64
POOL_WINDOWS = (2, 4, 8, 16)
LRU_BLOCK = 128
LRU_C = 8.0
N_EXPERTS = 8
EPS = 1e-6
LOG2_E = 1.4426950408889634

LANES = 128
SUBLANES = 8
VMEM_LIMIT = 56 * 1024 * 1024

POOL_TILE = 256
TOK_TILE = 512
FF_CHUNK = 256
MOE_TILE = 1024
MOE_FCHUNK = 1792
MOE_SUBCHUNK = 256
DMA_UNROLL = 8
POST_TILE = 1024
POST_SUB = 256
COMBINE_TILE = 256
LRU_SEGS = 64


def _dot(a, b):
    return jnp.dot(a, b, preferred_element_type=F32)


def _params(*sem):
    return pltpu.CompilerParams(dimension_semantics=sem, vmem_limit_bytes=VMEM_LIMIT)


def _resident(shape):
    return pl.BlockSpec(shape, lambda *_: (0,) * len(shape), pipeline_mode=pl.Buffered(1))


def _norm_mod(x, g, shift, scale):
    ms = jnp.mean(x * x, axis=-1, keepdims=True)
    return (x * lax.rsqrt(ms + EPS)) * (g * (1.0 + scale)) + shift


def _split_bf16(v):
    hi = v.astype(BF16)
    lo = (v - hi.astype(F32)).astype(BF16)
    return hi, lo


def _ada_kernel(cc_ref, w_ref, b_ref, o_ref):
    s = cc_ref[...]
    s = s * jax.nn.sigmoid(s)
    o_ref[0] = _dot(s.astype(BF16), w_ref[0].astype(BF16)) + b_ref[0]


def _adaln(cc, ada_w, ada_b):
    n_layers, d, n = ada_w.shape
    r = cc.shape[0]
    tn = 1536
    return pl.pallas_call(
        _ada_kernel,
        grid=(n_layers, n // tn),
        in_specs=[pl.BlockSpec((r, d), lambda l, j: (0, 0)),
                  pl.BlockSpec((1, d, tn), lambda l, j: (l, 0, j)),
                  pl.BlockSpec((1, 1, tn), lambda l, j: (l, 0, j))],
        out_specs=pl.BlockSpec((1, r, tn), lambda l, j: (l, 0, j)),
        out_shape=jax.ShapeDtypeStruct((n_layers, r, n), F32),
        compiler_params=_params("parallel", "parallel"),
        name="adaln",
    )(cc, ada_w, ada_b.reshape(n_layers, 1, n))


def _pool_tables(seg_len, gc):
    t = np.arange(POOL_TILE)
    seg, pos = t // seg_len, t % seg_len
    band = np.zeros((len(POOL_WINDOWS), POOL_TILE, POOL_TILE), np.float32)
    invc = np.zeros((len(POOL_WINDOWS), POOL_TILE, gc), np.float32)
    for gi, win in enumerate(POOL_WINDOWS):
        lo = np.clip(pos - win // 2, 0, seg_len)
        hi = np.clip(pos - win // 2 + win, 0, seg_len)
        inside = (pos[None, :] >= lo[:, None]) & (pos[None, :] < hi[:, None])
        band[gi] = (seg[:, None] == seg[None, :]) & inside
        invc[gi] = (1.0 / (hi - lo))[:, None]
    return jnp.asarray(band, BF16), jnp.asarray(invc, F32)


def _layer0_kernel(x_ref, mod_ref, g1_ref, g2_ref, band_ref, invc_ref, pw_ref, ps_ref,
                   wgu_ref, wd_ref, o_ref, act_ref, *, d_ff):
    m = mod_ref[0]
    x = x_ref[0]
    tm, d = x.shape
    n_groups = band_ref.shape[0]
    gc = d // n_groups
    h = _norm_mod(x, g1_ref[...], m[0:1], m[1:2])
    rows = []
    for s in range(tm // POOL_TILE):
        hs = h[s * POOL_TILE:(s + 1) * POOL_TILE]
        cols = []
        for gi in range(n_groups):
            hg = hs[:, gi * gc:(gi + 1) * gc]
            hi, lo = _split_bf16(hg)
            band = band_ref[gi]
            pooled = (_dot(band, hi) + _dot(band, lo)) * invc_ref[gi] - hg
            cols.append(_dot(pooled.astype(BF16), pw_ref[gi]))
        rows.append(jnp.concatenate(cols, axis=1))
    y = rows[0] if len(rows) == 1 else jnp.concatenate(rows, axis=0)
    x1 = x + m[2:3] * (y * ps_ref[...])
    h2 = _norm_mod(x1, g2_ref[...], m[3:4], m[4:5]).astype(BF16)
    for c in range(d_ff // FF_CHUNK):
        g = _dot(h2, wgu_ref[:, c * FF_CHUNK:(c + 1) * FF_CHUNK])
        u = _dot(h2, wgu_ref[:, d_ff + c * FF_CHUNK:d_ff + (c + 1) * FF_CHUNK])
        act_ref[:, c * FF_CHUNK:(c + 1) * FF_CHUNK] = ((g * jax.nn.sigmoid(g)) * u).astype(BF16)
    o_ref[0] = x1 + m[5:6] * _dot(act_ref[...], wd_ref[...])


def _layer0(x, mods, mod_row, g1, g2, band, invc, pw, ps, wgu, wd, name):
    bsz, t, d = x.shape
    d_ff = wd.shape[0]
    tm = min(TOK_TILE, t)
    return pl.pallas_call(
        functools.partial(_layer0_kernel, d_ff=d_ff),
        grid=(bsz, t // tm),
        in_specs=[pl.BlockSpec((1, tm, d), lambda b, i: (b, i, 0)),
                  pl.BlockSpec((1, 6, d), lambda b, i: (mod_row(b), 0, 0)),
                  _resident((1, d)), _resident((1, d)),
                  _resident(band.shape), _resident(invc.shape), _resident(pw.shape), _resident((1, d)),
                  _resident(wgu.shape), _resident(wd.shape)],
        out_specs=pl.BlockSpec((1, tm, d), lambda b, i: (b, i, 0)),
        out_shape=jax.ShapeDtypeStruct(x.shape, F32),
        scratch_shapes=[pltpu.VMEM((tm, d_ff), BF16)],
        compiler_params=_params("parallel", "parallel"),
        name=name,
    )(x, mods, g1, g2, band, invc, pw, ps, wgu, wd)


def _inproj_kernel(x_ref, mod_ref, g1_ref, w_ref, *out_refs, with_gate):
    m = mod_ref[0]
    h = _norm_mod(x_ref[0], g1_ref[...], m[0:1], m[1:2]).astype(BF16)
    y = _dot(h, w_ref[...])
    n = out_refs[0].shape[-1]
    if with_gate:
        out_refs[0][0] = jax.nn.gelu(y[:, :n])
        out_refs[1][0] = y[:, n:]
    else:
        out_refs[0][0] = y


def _inproj(x, mods, mod_row, g1, w, with_gate, name):
    bsz, t, d = x.shape
    tm = min(TOK_TILE, t)
    n_out = 2 if with_gate else 1
    n = w.shape[1] // n_out
    out_shape = jax.ShapeDtypeStruct((bsz, t, n), F32)
    out_spec = pl.BlockSpec((1, tm, n), lambda b, i: (b, i, 0))
    return pl.pallas_call(
        functools.partial(_inproj_kernel, with_gate=with_gate),
        grid=(bsz, t // tm),
        in_specs=[pl.BlockSpec((1, tm, d), lambda b, i: (b, i, 0)),
                  pl.BlockSpec((1, 6, d), lambda b, i: (mod_row(b), 0, 0)),
                  _resident((1, d)), _resident(w.shape)],
        out_specs=[out_spec] * n_out,
        out_shape=[out_shape] * n_out,
        compiler_params=_params("parallel", "parallel"),
        name=name,
    )(x, mods, g1, w)


def _scan8(a, b, row, reverse):
    for s in (1, 2, 4):
        if reverse:
            keep = row < SUBLANES - s
            shift = SUBLANES - s
        else:
            keep = row >= s
            shift = s
        a_sh = jnp.where(keep, pltpu.roll(a, shift, 0), 1.0)
        b_sh = jnp.where(keep, pltpu.roll(b, shift, 0), 0.0)
        b = a * b_sh + b
        a = a * a_sh
    return a, b


def _segment_scan(a_ref, b_ref, h_ref, row8, seg_len, reverse):
    nv = LRU_SEGS // SUBLANES

    def rows(k, j):
        jj = seg_len - 1 - j if reverse else j
        return pl.ds(k * SUBLANES * seg_len + jj, SUBLANES, stride=seg_len)

    zero = jnp.zeros((SUBLANES, LANES), F32)
    h = [zero] * nv
    p = [jnp.ones((SUBLANES, LANES), F32)] * nv
    for j in range(seg_len):
        for k in range(nv):
            a = a_ref[rows(k, j), :]
            h[k] = a * h[k] + b_ref[rows(k, j), :]
            p[k] = a * p[k]
    carry = zero
    start = [None] * nv
    for k in (range(nv - 1, -1, -1) if reverse else range(nv)):
        pk, hk = _scan8(p[k], h[k], row8, reverse)
        ends = hk + pk * carry
        if reverse:
            start[k] = jnp.where(row8 == SUBLANES - 1, carry, pltpu.roll(ends, SUBLANES - 1, 0))
            carry = jnp.broadcast_to(ends[0:1], ends.shape)
        else:
            start[k] = jnp.where(row8 == 0, carry, pltpu.roll(ends, 1, 0))
            carry = jnp.broadcast_to(ends[SUBLANES - 1:SUBLANES], ends.shape)
    h = start
    for j in range(seg_len):
        for k in range(nv):
            h[k] = a_ref[rows(k, j), :] * h[k] + b_ref[rows(k, j), :]
            h_ref[rows(k, j), :] = h[k]


def _lru_kernel(uz_ref, ux_ref, gate_ref, cw_ref, cb_ref, wc_ref, bc_ref, lam_ref, o_ref,
                af_ref, bf_ref, ab_ref, bb_ref, hf_ref, hb_ref, pad_ref, *, seg_len):
    lz = uz_ref.shape[1]
    lx = ux_ref.shape[1]
    cw = cw_ref[...]
    cb = cb_ref[...]
    wc = wc_ref[0]
    bc = bc_ref[0]
    neg_lam = -lam_ref[...]
    softplus = jnp.maximum(neg_lam, 0.0) + jnp.log1p(jnp.exp(-jnp.abs(neg_lam)))
    kh = (-0.5 * LRU_C * LOG2_E) * softplus
    zeros8 = jnp.zeros((SUBLANES, LANES), F32)

    def prep(src_ref, off_f, off_b):
        n = src_ref.shape[1]
        pad_ref[pl.ds(0, SUBLANES), :] = zeros8
        pad_ref[pl.ds(SUBLANES, n), :] = src_ref[0]
        pad_ref[pl.ds(SUBLANES + n, SUBLANES), :] = zeros8
        u = cb + cw[2:3] * src_ref[0]
        for k in (0, 1, 3):
            u = u + cw[k:k + 1] * pad_ref[pl.ds(SUBLANES + k - 2, n), :]
        t = jnp.tanh(_dot(u.astype(BF16), wc) + bc)
        hu = 0.5 * u
        for d, (a_ref, b_ref, off) in enumerate(((af_ref, bf_ref, off_f), (ab_ref, bb_ref, off_b))):
            t_r = t[:, 2 * d * LRU_BLOCK:(2 * d + 1) * LRU_BLOCK]
            t_i = t[:, (2 * d + 1) * LRU_BLOCK:(2 * d + 2) * LRU_BLOCK]
            a = jnp.exp2(kh[d:d + 1] * t_r + kh[d:d + 1])
            y = 1.0 - a * a
            root = jnp.where(y > 0.0, y * lax.rsqrt(y), 0.0)
            a_ref[pl.ds(off, n), :] = a
            b_ref[pl.ds(off, n), :] = root * (hu * t_i + hu)

    prep(uz_ref, 0, lx)
    prep(ux_ref, lz, 0)
    row8 = lax.broadcasted_iota(jnp.int32, (SUBLANES, LANES), 0)
    _segment_scan(af_ref, bf_ref, hf_ref, row8, seg_len, False)
    _segment_scan(ab_ref, bb_ref, hb_ref, row8, seg_len, True)
    hsum = hf_ref[pl.ds(lz, lx), :] + hb_ref[pl.ds(0, lx), :]
    o_ref[0] = (hsum * gate_ref[0]).astype(BF16)


def _lru(uz, ux, gate, conv_w, conv_b, wcat, bcat, lam):
    bsz, lz, d_rnn = uz.shape
    lx = ux.shape[1]
    heads = d_rnn // LRU_BLOCK
    seg_len = (lz + lx) // LRU_SEGS
    assert seg_len * LRU_SEGS == lz + lx and seg_len % SUBLANES == SUBLANES // 2
    seq = lambda n: pl.BlockSpec((1, n, LRU_BLOCK), lambda b, h: (b, 0, h))
    scratch = pltpu.VMEM((lz + lx, LRU_BLOCK), F32)
    return pl.pallas_call(
        functools.partial(_lru_kernel, seg_len=seg_len),
        grid=(bsz, heads),
        in_specs=[seq(lz), seq(lx), seq(lx),
                  pl.BlockSpec((conv_w.shape[0], LRU_BLOCK), lambda b, h: (0, h)),
                  pl.BlockSpec((1, LRU_BLOCK), lambda b, h: (0, h)),
                  pl.BlockSpec((1, LRU_BLOCK, 4 * LRU_BLOCK), lambda b, h: (h, 0, 0)),
                  pl.BlockSpec((1, 1, 4 * LRU_BLOCK), lambda b, h: (h, 0, 0)),
                  pl.BlockSpec((2, LRU_BLOCK), lambda b, h: (0, h))],
        out_specs=seq(lx),
        out_shape=jax.ShapeDtypeStruct((bsz, lx, d_rnn), BF16),
        scratch_shapes=[scratch] * 6 + [pltpu.VMEM((lx + 2 * SUBLANES, LRU_BLOCK), F32)],
        compiler_params=_params("parallel", "parallel"),
        name="rglru",
    )(uz, ux, gate, conv_w, conv_b, wcat, bcat, lam)


def _store_row_tiles(ref, v):
    tm, d = v.shape
    for c in range(d // LANES):
        ref[pl.ds(c, tm, stride=d // LANES), :] = v[:, c * LANES:(c + 1) * LANES]


def _load_row_tiles(ref, tm, d):
    return [ref[pl.ds(c, tm, stride=d // LANES), :] for c in range(d // LANES)]


def _post_kernel(y_ref, x_ref, mod_ref, g2_ref, wo_ref, wr2_ref, x2_ref, h2t_ref, ri_ref, rt_ref):
    m = mod_ref[0]
    tm = x_ref.shape[1]
    rt = x_ref.shape[2] // LANES
    for q in range(tm // POST_SUB):
        rows = pl.ds(q * POST_SUB, POST_SUB)
        x2 = x_ref[0, rows, :] + m[2:3] * _dot(y_ref[0, rows, :], wo_ref[...])
        x2_ref[0, rows, :] = x2
        h2 = _norm_mod(x2, g2_ref[...], m[3:4], m[4:5])
        for c in range(rt):
            h2t_ref[pl.ds(q * POST_SUB * rt + c, POST_SUB, stride=rt), :] = h2[:, c * LANES:(c + 1) * LANES]
        hi, lo = _split_bf16(h2)
        both = _dot(hi, wr2_ref[...]) + _dot(lo, wr2_ref[...])
        logits = both + pltpu.roll(both, LANES - N_EXPERTS, 1)
        lane = lax.broadcasted_iota(jnp.int32, logits.shape, 1)
        logits = jnp.where(lane < N_EXPERTS, logits, -1e30)
        e = jnp.exp(logits - jnp.max(logits, axis=-1, keepdims=True))
        p = e / jnp.sum(e, axis=-1, keepdims=True)
        p1 = jnp.max(p, axis=-1, keepdims=True)
        i1 = jnp.min(jnp.where(p == p1, lane, LANES), axis=-1, keepdims=True)
        rest = jnp.where(lane == i1, -1.0, p)
        p2 = jnp.max(rest, axis=-1, keepdims=True)
        i2 = jnp.min(jnp.where(rest == p2, lane, LANES), axis=-1, keepdims=True)
        den = p1 + p2
        ri = jnp.where(lane == 0, i1.astype(F32),
                       jnp.where(lane == 1, i2.astype(F32),
                                 jnp.where(lane == 2, p1 / den,
                                           jnp.where(lane == 3, p2 / den, 0.0))))
        ri_ref[rows, :] = ri
        rt_ref[:, rows] = jnp.transpose(ri)[0:SUBLANES, :]


def _post(y, x, mods, g2, wo, wr2):
    bsz, t, d = x.shape
    d_rnn = y.shape[-1]
    tm = min(POST_TILE, t)
    nt = t // tm
    rt = d // LANES
    return pl.pallas_call(
        _post_kernel,
        grid=(bsz, nt),
        in_specs=[pl.BlockSpec((1, tm, d_rnn), lambda b, i: (b, i, 0)),
                  pl.BlockSpec((1, tm, d), lambda b, i: (b, i, 0)),
                  pl.BlockSpec((1, 6, d), lambda b, i: (b, 0, 0)),
                  _resident((1, d)), _resident(wo.shape), _resident(wr2.shape)],
        out_specs=[pl.BlockSpec((1, tm, d), lambda b, i: (b, i, 0)),
                   pl.BlockSpec((tm * rt, LANES), lambda b, i: (b * nt + i, 0)),
                   pl.BlockSpec((tm, LANES), lambda b, i: (b * nt + i, 0)),
                   pl.BlockSpec((SUBLANES, tm), lambda b, i: (0, b * nt + i))],
        out_shape=[jax.ShapeDtypeStruct(x.shape, F32),
                   jax.ShapeDtypeStruct((bsz * t * rt, LANES), F32),
                   jax.ShapeDtypeStruct((bsz * t, LANES), F32),
                   jax.ShapeDtypeStruct((SUBLANES, bsz * t), F32)],
        compiler_params=_params("parallel", "parallel"),
        name="outproj_router",
    )(y, x, mods, g2, wo, wr2)


def _moe_kernel(te_ref, nu_ref, first_ref, nxt_ref, prev_ref, h2t_ref, wg_ref, wu_ref, wd_ref, y2_ref,
                xs_buf, xb_ref, act_ref, acc_ref, out_buf, gsem, ssem, *, n_steps):
    i = pl.program_id(0)
    j = pl.program_id(1)
    tm, d = xb_ref.shape
    tf = act_ref.shape[1]
    rt = d // LANES
    nu = nu_ref[0]
    used = i < nu
    slot = i % 2

    def gather_row(idx_ref, buf_slot, r):
        src_row = pl.multiple_of(idx_ref[r] * rt, rt)
        dst_row = pl.multiple_of(r * rt, rt)
        pltpu.make_async_copy(h2t_ref.at[pl.ds(src_row, rt), :],
                              xs_buf.at[buf_slot, pl.ds(dst_row, rt), :], gsem.at[buf_slot]).start()

    def scatter_row(r):
        src_row = pl.multiple_of(r * rt, rt)
        dst_row = pl.multiple_of(prev_ref[r] * rt, rt)
        pltpu.make_async_copy(out_buf.at[pl.ds(src_row, rt), :],
                              y2_ref.at[pl.ds(dst_row, rt), :], ssem).start()

    def wait_scatter():
        pltpu.make_async_copy(out_buf, out_buf, ssem).wait()

    @pl.when(jnp.logical_and(i == 0, j == 0))
    def _():
        out_buf[...] = jnp.zeros(out_buf.shape, out_buf.dtype)

        def issue(r, carry):
            gather_row(first_ref, 0, r)
            return carry
        lax.fori_loop(0, tm, issue, 0, unroll=DMA_UNROLL)

    @pl.when(jnp.logical_and(j == 0, i <= nu))
    def _():
        for s in range(2):
            @pl.when(slot == s)
            def _(s=s):
                pltpu.make_async_copy(xs_buf.at[s], xs_buf.at[s], gsem.at[s]).wait()

                @pl.when(used)
                def _():
                    for c in range(rt):
                        xb_ref[:, c * LANES:(c + 1) * LANES] = (
                            xs_buf[s, pl.ds(c, tm, stride=rt), :].astype(BF16))

        @pl.when(i == nu)
        def _():
            def issue(r, carry):
                scatter_row(r)
                return carry
            lax.fori_loop(0, tm, issue, 0, unroll=DMA_UNROLL)
            wait_scatter()

    @pl.when(used)
    def _():
        xb = xb_ref[...]
        n_sub = tf // MOE_SUBCHUNK
        rows_step = tm // n_steps
        per = -(-rows_step // n_sub)
        base = j * rows_step
        for c in range(n_sub):
            cs = slice(c * MOE_SUBCHUNK, (c + 1) * MOE_SUBCHUNK)
            g = _dot(xb, wg_ref[0, :, cs])
            u = _dot(xb, wu_ref[0, :, cs])
            act_ref[:, cs] = ((g * jax.nn.sigmoid(g)) * u).astype(BF16)
            for r in range(c * per, min((c + 1) * per, rows_step)):
                gather_row(nxt_ref, 1 - slot, base + r)
                scatter_row(base + r)
        part = _dot(act_ref[...], wd_ref[0])

        @pl.when(j == 0)
        def _():
            acc_ref[...] = part

        @pl.when(j > 0)
        def _():
            acc_ref[...] += part

        @pl.when(j == n_steps - 1)
        def _():
            wait_scatter()
            _store_row_tiles(out_buf, acc_ref[...])


def _moe(tile_e, n_used, slot_tok, prev_dst, h2t, wgu, wd, n_tok):
    d = wgu.shape[1]
    d_e = wd.shape[1]
    rt = d // LANES
    tm = MOE_TILE
    n_tiles = slot_tok.shape[0] // tm
    nj = d_e // MOE_FCHUNK

    def jf(i, j, nu):
        return jnp.where(i < nu[0], j, nj - 1)

    smem_tile = lambda f: pl.BlockSpec((tm,), f, memory_space=pltpu.SMEM)
    return pl.pallas_call(
        functools.partial(_moe_kernel, n_steps=nj),
        grid_spec=pltpu.PrefetchScalarGridSpec(
            num_scalar_prefetch=2,
            grid=(n_tiles, nj),
            in_specs=[smem_tile(lambda i, j, te, nu: (0,)),
                      smem_tile(lambda i, j, te, nu: (jnp.minimum(i + 1, n_tiles - 1),)),
                      smem_tile(lambda i, j, te, nu: (i,)),
                      pl.BlockSpec(memory_space=pl.ANY),
                      pl.BlockSpec((1, d, MOE_FCHUNK), lambda i, j, te, nu: (te[i], 0, jf(i, j, nu))),
                      pl.BlockSpec((1, d, MOE_FCHUNK), lambda i, j, te, nu: (te[i], 0, nj + jf(i, j, nu))),
                      pl.BlockSpec((1, MOE_FCHUNK, d), lambda i, j, te, nu: (te[i], jf(i, j, nu), 0))],
            out_specs=pl.BlockSpec(memory_space=pl.ANY),
            scratch_shapes=[pltpu.VMEM((2, tm * rt, LANES), F32),
                            pltpu.VMEM((tm, d), BF16),
                            pltpu.VMEM((tm, MOE_FCHUNK), BF16),
                            pltpu.VMEM((tm, d), F32),
                            pltpu.VMEM((tm * rt, LANES), F32),
                            pltpu.SemaphoreType.DMA((2,)),
                            pltpu.SemaphoreType.DMA]),
        out_shape=jax.ShapeDtypeStruct(((2 * n_tok + tm) * rt, LANES), F32),
        compiler_params=_params("arbitrary", "arbitrary"),
        name="moe_experts",
    )(tile_e, n_used, slot_tok, slot_tok, prev_dst, h2t, wgu, wgu, wd)


def _combine_kernel(x_ref, ri_ref, mod_ref, fg_ref, ya_ref, yb_ref, o_ref):
    tm, d = x_ref.shape[1], x_ref.shape[2]
    ri = ri_ref[...]
    w1 = ri[:, 2:3]
    w2 = ri[:, 3:4]
    ya = _load_row_tiles(ya_ref, tm, d)
    yb = _load_row_tiles(yb_ref, tm, d)
    moe = jnp.concatenate([w1 * a + w2 * b for a, b in zip(ya, yb)], axis=1)
    x3 = x_ref[0] + mod_ref[0][5:6] * moe
    ms = jnp.mean(x3 * x3, axis=-1, keepdims=True)
    o_ref[0] = (x3 * lax.rsqrt(ms + EPS)) * fg_ref[...]


def _combine(x2, ri, mods, fg, y2):
    bsz, t, d = x2.shape
    tm = COMBINE_TILE
    nt = t // tm
    rt = d // LANES
    n_blocks = bsz * nt
    return pl.pallas_call(
        _combine_kernel,
        grid=(bsz, nt),
        in_specs=[pl.BlockSpec((1, tm, d), lambda b, i: (b, i, 0)),
                  pl.BlockSpec((tm, LANES), lambda b, i: (b * nt + i, 0)),
                  pl.BlockSpec((1, 6, d), lambda b, i: (b, 0, 0)),
                  pl.BlockSpec((1, d), lambda b, i: (0, 0)),
                  pl.BlockSpec((tm * rt, LANES), lambda b, i: (b * nt + i, 0)),
                  pl.BlockSpec((tm * rt, LANES), lambda b, i: (n_blocks + b * nt + i, 0))],
        out_specs=pl.BlockSpec((1, tm, d), lambda b, i: (b, i, 0)),
        out_shape=jax.ShapeDtypeStruct(x2.shape, F32),
        compiler_params=_params("parallel", "parallel"),
        name="moe_combine_norm",
    )(x2, ri, mods, fg, y2, y2)


def _routing(flat_e, n_tok):
    n2 = flat_e.shape[0]
    experts = jnp.arange(N_EXPERTS, dtype=jnp.int32)
    counts = jnp.sum((flat_e[:, None] == experts[None, :]).astype(jnp.int32), axis=0)
    padded = (counts + MOE_TILE - 1) // MOE_TILE * MOE_TILE
    pends = jnp.cumsum(padded)
    n_tiles = n2 // MOE_TILE + N_EXPERTS + 1
    idx_bits = max(n2, MOE_TILE).bit_length()
    r = jnp.arange(MOE_TILE, dtype=jnp.int32)
    pad_e = jnp.where(r[None, :] < (padded - counts)[:, None], experts[:, None], N_EXPERTS)
    pad_e = jnp.concatenate([pad_e.reshape(-1), jnp.full((MOE_TILE,), N_EXPERTS, jnp.int32)])
    keys = jnp.concatenate([
        (flat_e << (idx_bits + 1)) | jnp.arange(n2, dtype=jnp.int32),
        (pad_e << (idx_bits + 1)) | (1 << idx_bits) | jnp.tile(r, N_EXPERTS + 1)])
    keys = jnp.sort(keys)
    valid = ((keys >> idx_bits) & 1) == 0
    f = keys & ((1 << idx_bits) - 1)
    slot = jnp.arange(n_tiles * MOE_TILE, dtype=jnp.int32)
    slot_tok = jnp.where(valid, f % n_tok, 0).astype(jnp.int32)
    slot_dst = jnp.where(valid, f, n2 + slot % MOE_TILE).astype(jnp.int32)
    prev_dst = jnp.concatenate([n2 + jnp.arange(MOE_TILE, dtype=jnp.int32), slot_dst[:-MOE_TILE]])
    tile0 = jnp.arange(n_tiles, dtype=jnp.int32) * MOE_TILE
    tile_e = jnp.minimum(jnp.sum((tile0[:, None] >= pends[None, :]).astype(jnp.int32), axis=1), N_EXPERTS - 1)
    n_used = (pends[-1] // MOE_TILE).astype(jnp.int32).reshape(1)
    return slot_tok, prev_dst, tile_e.astype(jnp.int32), n_used


def kernel(x, c, ctx, c_ctx, ada_w, ada_b, norm1_g, norm2_g, pool_w, pool_scale, ffn_w_gu, ffn_w_down,
           lru_w_in, lru_conv_w, lru_conv_b, lru_w_r, lru_b_r, lru_w_i, lru_b_i, lru_lambda, lru_w_out,
           moe_w_router, moe_w_gu, moe_w_down, final_g):
    bsz, seq, d = x.shape
    assert ada_w.shape[0] == 2 and seq % TOK_TILE == 0 and TOK_TILE % POOL_TILE == 0 and ctx.shape[1] == POOL_TILE
    assert d == SUBLANES * LANES
    d_rnn = lru_w_out.shape[1]
    heads = d_rnn // LRU_BLOCK
    gc = d // len(POOL_WINDOWS)
    z_row = bsz

    n_rows = (bsz + 1 + SUBLANES - 1) // SUBLANES * SUBLANES
    cc = jnp.concatenate([c, c_ctx[None, :], jnp.zeros((n_rows - bsz - 1, d), F32)], axis=0)
    mods = _adaln(cc, ada_w, ada_b).reshape(2, n_rows, 6, d)
    x_row = lambda b: b
    ctx_row = lambda b: z_row

    band_x, invc_x = _pool_tables(GRID_W, gc)
    band_z, invc_z = _pool_tables(POOL_TILE, gc)
    l0 = (norm1_g[0:1], norm2_g[0:1])
    l0w = (pool_w[0].astype(BF16), pool_scale[0:1], ffn_w_gu[0].astype(BF16), ffn_w_down[0].astype(BF16))
    x1 = _layer0(x, mods[0], x_row, *l0, band_x, invc_x, *l0w, name="layer0_latent")
    z1 = _layer0(ctx, mods[0], ctx_row, *l0, band_z, invc_z, *l0w, name="layer0_context")

    w_in = lru_w_in[0].astype(BF16)
    gate_x, ux = _inproj(x1, mods[1], x_row, norm1_g[1:2], w_in, True, name="lru_inproj_latent")
    (uz,) = _inproj(z1, mods[1], ctx_row, norm1_g[1:2], w_in[:, d_rnn:], False, name="lru_inproj_context")
    wcat = (0.5 * jnp.concatenate([lru_w_r[0, 0], lru_w_i[0, 0], lru_w_r[0, 1], lru_w_i[0, 1]], axis=-1)).astype(BF16)
    bcat = 0.5 * jnp.concatenate([v.reshape(heads, 1, LRU_BLOCK)
                                  for v in (lru_b_r[0, 0], lru_b_i[0, 0], lru_b_r[0, 1], lru_b_i[0, 1])], axis=-1)
    y = _lru(uz, ux, gate_x, lru_conv_w[0], lru_conv_b[0:1], wcat, bcat, lru_lambda[0])

    wr_hi = moe_w_router[0].astype(BF16)
    wr_lo = (moe_w_router[0] - wr_hi.astype(F32)).astype(BF16)
    wr2 = jnp.pad(jnp.concatenate([wr_hi, wr_lo], axis=1), ((0, 0), (0, LANES - 2 * N_EXPERTS)))
    x2, h2t, ri, rt = _post(y, x1, mods[1], norm2_g[1:2], lru_w_out[0].astype(BF16), wr2)

    n_tok = bsz * seq
    flat_e = rt[0:2].reshape(-1).astype(jnp.int32)
    slot_tok, prev_dst, tile_e, n_used = _routing(flat_e, n_tok)
    y2 = _moe(tile_e, n_used, slot_tok, prev_dst, h2t, moe_w_gu[0].astype(BF16), moe_w_down[0].astype(BF16), n_tok)
    return _combine(x2, ri, mods[1], final_g[None, :], y2)
```

```python
import functools

import numpy as np
import jax
import jax.numpy as jnp
from jax import lax
from jax.experimental import pallas as pl
from jax.experimental.pallas import tpu as pltpu

F32 = jnp.float32
BF16 = jnp.bfloat16

GRID_W = 64
POOL_WINDOWS = (2, 4, 8, 16)
LRU_BLOCK = 128
LRU_C = 8.0
N_EXPERTS = 8
EPS = 1e-6
LOG2_E = 1.4426950408889634

LANES = 128
SUBLANES = 8
VMEM_LIMIT = 56 * 1024 * 1024

POOL_TILE = 256
TOK_TILE = 512
FF_CHUNK = 256
MOE_TILE = 1024
MOE_FCHUNK = 1792
MOE_SUBCHUNK = 256
DMA_UNROLL = 8
POST_TILE = 1024
POST_SUB = 256
COMBINE_TILE = 256
LRU_SEGS = 64


def _dot(a, b):
    return jnp.dot(a, b, preferred_element_type=F32)


def _params(*sem):
    return pltpu.CompilerParams(dimension_semantics=sem, vmem_limit_bytes=VMEM_LIMIT)


def _resident(shape):
    return pl.BlockSpec(shape, lambda *_: (0,) * len(shape), pipeline_mode=pl.Buffered(1))


def _norm_mod(x, g, shift, scale):
    ms = jnp.mean(x * x, axis=-1, keepdims=True)
    return (x * lax.rsqrt(ms + EPS)) * (g * (1.0 + scale)) + shift


def _split_bf16(v):
    hi = v.astype(BF16)
    lo = (v - hi.astype(F32)).astype(BF16)
    return hi, lo


def _ada_kernel(cc_ref, w_ref, b_ref, o_ref):
    s = cc_ref[...]
    s = s * jax.nn.sigmoid(s)
    o_ref[0] = _dot(s.astype(BF16), w_ref[0].astype(BF16)) + b_ref[0]


def _adaln(cc, ada_w, ada_b):
    n_layers, d, n = ada_w.shape
    r = cc.shape[0]
    tn = 1536
    return pl.pallas_call(
        _ada_kernel,
        grid=(n_layers, n // tn),
        in_specs=[pl.BlockSpec((r, d), lambda l, j: (0, 0)),
                  pl.BlockSpec((1, d, tn), lambda l, j: (l, 0, j)),
                  pl.BlockSpec((1, 1, tn), lambda l, j: (l, 0, j))],
        out_specs=pl.BlockSpec((1, r, tn), lambda l, j: (l, 0, j)),
        out_shape=jax.ShapeDtypeStruct((n_layers, r, n), F32),
        compiler_params=_params("parallel", "parallel"),
        name="adaln",
    )(cc, ada_w, ada_b.reshape(n_layers, 1, n))


def _pool_tables(seg_len, gc):
    t = np.arange(POOL_TILE)
    seg, pos = t // seg_len, t % seg_len
    band = np.zeros((len(POOL_WINDOWS), POOL_TILE, POOL_TILE), np.float32)
    invc = np.zeros((len(POOL_WINDOWS), POOL_TILE, gc), np.float32)
    for gi, win in enumerate(POOL_WINDOWS):
        lo = np.clip(pos - win // 2, 0, seg_len)
        hi = np.clip(pos - win // 2 + win, 0, seg_len)
        inside = (pos[None, :] >= lo[:, None]) & (pos[None, :] < hi[:, None])
        band[gi] = (seg[:, None] == seg[None, :]) & inside
        invc[gi] = (1.0 / (hi - lo))[:, None]
    return jnp.asarray(band, BF16), jnp.asarray(invc, F32)


def _layer0_kernel(x_ref, mod_ref, g1_ref, g2_ref, band_ref, invc_ref, pw_ref, ps_ref,
                   wgu_ref, wd_ref, o_ref, act_ref, *, d_ff):
    m = mod_ref[0]
    x = x_ref[0]
    tm, d = x.shape
    n_groups = band_ref.shape[0]
    gc = d // n_groups
    h = _norm_mod(x, g1_ref[...], m[0:1], m[1:2])
    rows = []
    for s in range(tm // POOL_TILE):
        hs = h[s * POOL_TILE:(s + 1) * POOL_TILE]
        cols = []
        for gi in range(n_groups):
            hg = hs[:, gi * gc:(gi + 1) * gc]
            hi, lo = _split_bf16(hg)
            band = band_ref[gi]
            pooled = (_dot(band, hi) + _dot(band, lo)) * invc_ref[gi] - hg
            cols.append(_dot(pooled.astype(BF16), pw_ref[gi]))
        rows.append(jnp.concatenate(cols, axis=1))
    y = rows[0] if len(rows) == 1 else jnp.concatenate(rows, axis=0)
    x1 = x + m[2:3] * (y * ps_ref[...])
    h2 = _norm_mod(x1, g2_ref[...], m[3:4], m[4:5]).astype(BF16)
    for c in range(d_ff // FF_CHUNK):
        g = _dot(h2, wgu_ref[:, c * FF_CHUNK:(c + 1) * FF_CHUNK])
        u = _dot(h2, wgu_ref[:, d_ff + c * FF_CHUNK:d_ff + (c + 1) * FF_CHUNK])
        act_ref[:, c * FF_CHUNK:(c + 1) * FF_CHUNK] = ((g * jax.nn.sigmoid(g)) * u).astype(BF16)
    o_ref[0] = x1 + m[5:6] * _dot(act_ref[...], wd_ref[...])


def _layer0(x, mods, mod_row, g1, g2, band, invc, pw, ps, wgu, wd, name):
    bsz, t, d = x.shape
    d_ff = wd.shape[0]
    tm = min(TOK_TILE, t)
    return pl.pallas_call(
        functools.partial(_layer0_kernel, d_ff=d_ff),
        grid=(bsz, t // tm),
        in_specs=[pl.BlockSpec((1, tm, d), lambda b, i: (b, i, 0)),
                  pl.BlockSpec((1, 6, d), lambda b, i: (mod_row(b), 0, 0)),
                  _resident((1, d)), _resident((1, d)),
                  _resident(band.shape), _resident(invc.shape), _resident(pw.shape), _resident((1, d)),
                  _resident(wgu.shape), _resident(wd.shape)],
        out_specs=pl.BlockSpec((1, tm, d), lambda b, i: (b, i, 0)),
        out_shape=jax.ShapeDtypeStruct(x.shape, F32),
        scratch_shapes=[pltpu.VMEM((tm, d_ff), BF16)],
        compiler_params=_params("parallel", "parallel"),
        name=name,
    )(x, mods, g1, g2, band, invc, pw, ps, wgu, wd)


def _inproj_kernel(x_ref, mod_ref, g1_ref, w_ref, *out_refs, with_gate):
    m = mod_ref[0]
    h = _norm_mod(x_ref[0], g1_ref[...], m[0:1], m[1:2]).astype(BF16)
    y = _dot(h, w_ref[...])
    n = out_refs[0].shape[-1]
    if with_gate:
        out_refs[0][0] = jax.nn.gelu(y[:, :n])
        out_refs[1][0] = y[:, n:]
    else:
        out_refs[0][0] = y


def _inproj(x, mods, mod_row, g1, w, with_gate, name):
    bsz, t, d = x.shape
    tm = min(TOK_TILE, t)
    n_out = 2 if with_gate else 1
    n = w.shape[1] // n_out
    out_shape = jax.ShapeDtypeStruct((bsz, t, n), F32)
    out_spec = pl.BlockSpec((1, tm, n), lambda b, i: (b, i, 0))
    return pl.pallas_call(
        functools.partial(_inproj_kernel, with_gate=with_gate),
        grid=(bsz, t // tm),
        in_specs=[pl.BlockSpec((1, tm, d), lambda b, i: (b, i, 0)),
                  pl.BlockSpec((1, 6, d), lambda b, i: (mod_row(b), 0, 0)),
                  _resident((1, d)), _resident(w.shape)],
        out_specs=[out_spec] * n_out,
        out_shape=[out_shape] * n_out,
        compiler_params=_params("parallel", "parallel"),
        name=name,
    )(x, mods, g1, w)


def _scan8(a, b, row, reverse):
    for s in (1, 2, 4):
        if reverse:
            keep = row < SUBLANES - s
            shift = SUBLANES - s
        else:
            keep = row >= s
            shift = s
        a_sh = jnp.where(keep, pltpu.roll(a, shift, 0), 1.0)
        b_sh = jnp.where(keep, pltpu.roll(b, shift, 0), 0.0)
        b = a * b_sh + b
        a = a * a_sh
    return a, b


def _segment_scan(a_ref, b_ref, h_ref, row8, seg_len, reverse):
    nv = LRU_SEGS // SUBLANES

    def rows(k, j):
        jj = seg_len - 1 - j if reverse else j
        return pl.ds(k * SUBLANES * seg_len + jj, SUBLANES, stride=seg_len)

    zero = jnp.zeros((SUBLANES, LANES), F32)
    h = [zero] * nv
    p = [jnp.ones((SUBLANES, LANES), F32)] * nv
    for j in range(seg_len):
        for k in range(nv):
            a = a_ref[rows(k, j), :]
            h[k] = a * h[k] + b_ref[rows(k, j), :]
            p[k] = a * p[k]
    carry = zero
    start = [None] * nv
    for k in (range(nv - 1, -1, -1) if reverse else range(nv)):
        pk, hk = _scan8(p[k], h[k], row8, reverse)
        ends = hk + pk * carry
        if reverse:
            start[k] = jnp.where(row8 == SUBLANES - 1, carry, pltpu.roll(ends, SUBLANES - 1, 0))
            carry = jnp.broadcast_to(ends[0:1], ends.shape)
        else:
            start[k] = jnp.where(row8 == 0, carry, pltpu.roll(ends, 1, 0))
            carry = jnp.broadcast_to(ends[SUBLANES - 1:SUBLANES], ends.shape)
    h = start
    for j in range(seg_len):
        for k in range(nv):
            h[k] = a_ref[rows(k, j), :] * h[k] + b_ref[rows(k, j), :]
            h_ref[rows(k, j), :] = h[k]


def _lru_kernel(uz_ref, ux_ref, gate_ref, cw_ref, cb_ref, wc_ref, bc_ref, lam_ref, o_ref,
                af_ref, bf_ref, ab_ref, bb_ref, hf_ref, hb_ref, pad_ref, *, seg_len):
    lz = uz_ref.shape[1]
    lx = ux_ref.shape[1]
    cw = cw_ref[...]
    cb = cb_ref[...]
    wc = wc_ref[0]
    bc = bc_ref[0]
    neg_lam = -lam_ref[...]
    softplus = jnp.maximum(neg_lam, 0.0) + jnp.log1p(jnp.exp(-jnp.abs(neg_lam)))
    kh = (-0.5 * LRU_C * LOG2_E) * softplus
    zeros8 = jnp.zeros((SUBLANES, LANES), F32)

    def prep(src_ref, off_f, off_b):
        n = src_ref.shape[1]
        pad_ref[pl.ds(0, SUBLANES), :] = zeros8
        pad_ref[pl.ds(SUBLANES, n), :] = src_ref[0]
        pad_ref[pl.ds(SUBLANES + n, SUBLANES), :] = zeros8
        u = cb + cw[2:3] * src_ref[0]
        for k in (0, 1, 3):
            u = u + cw[k:k + 1] * pad_ref[pl.ds(SUBLANES + k - 2, n), :]
        t = jnp.tanh(_dot(u.astype(BF16), wc) + bc)
        hu = 0.5 * u
        for d, (a_ref, b_ref, off) in enumerate(((af_ref, bf_ref, off_f), (ab_ref, bb_ref, off_b))):
            t_r = t[:, 2 * d * LRU_BLOCK:(2 * d + 1) * LRU_BLOCK]
            t_i = t[:, (2 * d + 1) * LRU_BLOCK:(2 * d + 2) * LRU_BLOCK]
            a = jnp.exp2(kh[d:d + 1] * t_r + kh[d:d + 1])
            y = 1.0 - a * a
            root = jnp.where(y > 0.0, y * lax.rsqrt(y), 0.0)
            a_ref[pl.ds(off, n), :] = a
            b_ref[pl.ds(off, n), :] = root * (hu * t_i + hu)

    prep(uz_ref, 0, lx)
    prep(ux_ref, lz, 0)
    row8 = lax.broadcasted_iota(jnp.int32, (SUBLANES, LANES), 0)
    _segment_scan(af_ref, bf_ref, hf_ref, row8, seg_len, False)
    _segment_scan(ab_ref, bb_ref, hb_ref, row8, seg_len, True)
    hsum = hf_ref[pl.ds(lz, lx), :] + hb_ref[pl.ds(0, lx), :]
    o_ref[0] = (hsum * gate_ref[0]).astype(BF16)


def _lru(uz, ux, gate, conv_w, conv_b, wcat, bcat, lam):
    bsz, lz, d_rnn = uz.shape
    lx = ux.shape[1]
    heads = d_rnn // LRU_BLOCK
    seg_len = (lz + lx) // LRU_SEGS
    assert seg_len * LRU_SEGS == lz + lx and seg_len % SUBLANES == SUBLANES // 2
    seq = lambda n: pl.BlockSpec((1, n, LRU_BLOCK), lambda b, h: (b, 0, h))
    scratch = pltpu.VMEM((lz + lx, LRU_BLOCK), F32)
    return pl.pallas_call(
        functools.partial(_lru_kernel, seg_len=seg_len),
        grid=(bsz, heads),
        in_specs=[seq(lz), seq(lx), seq(lx),
                  pl.BlockSpec((conv_w.shape[0], LRU_BLOCK), lambda b, h: (0, h)),
                  pl.BlockSpec((1, LRU_BLOCK), lambda b, h: (0, h)),
                  pl.BlockSpec((1, LRU_BLOCK, 4 * LRU_BLOCK), lambda b, h: (h, 0, 0)),
                  pl.BlockSpec((1, 1, 4 * LRU_BLOCK), lambda b, h: (h, 0, 0)),
                  pl.BlockSpec((2, LRU_BLOCK), lambda b, h: (0, h))],
        out_specs=seq(lx),
        out_shape=jax.ShapeDtypeStruct((bsz, lx, d_rnn), BF16),
        scratch_shapes=[scratch] * 6 + [pltpu.VMEM((lx + 2 * SUBLANES, LRU_BLOCK), F32)],
        compiler_params=_params("parallel", "parallel"),
        name="rglru",
    )(uz, ux, gate, conv_w, conv_b, wcat, bcat, lam)


def _store_row_tiles(ref, v):
    tm, d = v.shape
    for c in range(d // LANES):
        ref[pl.ds(c, tm, stride=d // LANES), :] = v[:, c * LANES:(c + 1) * LANES]


def _load_row_tiles(ref, tm, d):
    return [ref[pl.ds(c, tm, stride=d // LANES), :] for c in range(d // LANES)]


def _post_kernel(y_ref, x_ref, mod_ref, g2_ref, wo_ref, wr2_ref, x2_ref, h2t_ref, ri_ref, rt_ref):
    m = mod_ref[0]
    tm = x_ref.shape[1]
    rt = x_ref.shape[2] // LANES
    for q in range(tm // POST_SUB):
        rows = pl.ds(q * POST_SUB, POST_SUB)
        x2 = x_ref[0, rows, :] + m[2:3] * _dot(y_ref[0, rows, :], wo_ref[...])
        x2_ref[0, rows, :] = x2
        h2 = _norm_mod(x2, g2_ref[...], m[3:4], m[4:5])
        for c in range(rt):
            h2t_ref[pl.ds(q * POST_SUB * rt + c, POST_SUB, stride=rt), :] = h2[:, c * LANES:(c + 1) * LANES]
        hi, lo = _split_bf16(h2)
        both = _dot(hi, wr2_ref[...]) + _dot(lo, wr2_ref[...])
        logits = both + pltpu.roll(both, LANES - N_EXPERTS, 1)
        lane = lax.broadcasted_iota(jnp.int32, logits.shape, 1)
        logits = jnp.where(lane < N_EXPERTS, logits, -1e30)
        e = jnp.exp(logits - jnp.max(logits, axis=-1, keepdims=True))
        p = e / jnp.sum(e, axis=-1, keepdims=True)
        p1 = jnp.max(p, axis=-1, keepdims=True)
        i1 = jnp.min(jnp.where(p == p1, lane, LANES), axis=-1, keepdims=True)
        rest = jnp.where(lane == i1, -1.0, p)
        p2 = jnp.max(rest, axis=-1, keepdims=True)
        i2 = jnp.min(jnp.where(rest == p2, lane, LANES), axis=-1, keepdims=True)
        den = p1 + p2
        ri = jnp.where(lane == 0, i1.astype(F32),
                       jnp.where(lane == 1, i2.astype(F32),
                                 jnp.where(lane == 2, p1 / den,
                                           jnp.where(lane == 3, p2 / den, 0.0))))
        ri_ref[rows, :] = ri
        rt_ref[:, rows] = jnp.transpose(ri)[0:SUBLANES, :]


def _post(y, x, mods, g2, wo, wr2):
    bsz, t, d = x.shape
    d_rnn = y.shape[-1]
    tm = min(POST_TILE, t)
    nt = t // tm
    rt = d // LANES
    return pl.pallas_call(
        _post_kernel,
        grid=(bsz, nt),
        in_specs=[pl.BlockSpec((1, tm, d_rnn), lambda b, i: (b, i, 0)),
                  pl.BlockSpec((1, tm, d), lambda b, i: (b, i, 0)),
                  pl.BlockSpec((1, 6, d), lambda b, i: (b, 0, 0)),
                  _resident((1, d)), _resident(wo.shape), _resident(wr2.shape)],
        out_specs=[pl.BlockSpec((1, tm, d), lambda b, i: (b, i, 0)),
                   pl.BlockSpec((tm * rt, LANES), lambda b, i: (b * nt + i, 0)),
                   pl.BlockSpec((tm, LANES), lambda b, i: (b * nt + i, 0)),
                   pl.BlockSpec((SUBLANES, tm), lambda b, i: (0, b * nt + i))],
        out_shape=[jax.ShapeDtypeStruct(x.shape, F32),
                   jax.ShapeDtypeStruct((bsz * t * rt, LANES), F32),
                   jax.ShapeDtypeStruct((bsz * t, LANES), F32),
                   jax.ShapeDtypeStruct((SUBLANES, bsz * t), F32)],
        compiler_params=_params("parallel", "parallel"),
        name="outproj_router",
    )(y, x, mods, g2, wo, wr2)


def _moe_kernel(te_ref, nu_ref, first_ref, nxt_ref, prev_ref, h2t_ref, wg_ref, wu_ref, wd_ref, y2_ref,
                xs_buf, xb_ref, act_ref, acc_ref, out_buf, gsem, ssem, *, n_steps):
    i = pl.program_id(0)
    j = pl.program_id(1)
    tm, d = xb_ref.shape
    tf = act_ref.shape[1]
    rt = d // LANES
    nu = nu_ref[0]
    used = i < nu
    slot = i % 2

    def gather_row(idx_ref, buf_slot, r):
        src_row = pl.multiple_of(idx_ref[r] * rt, rt)
        dst_row = pl.multiple_of(r * rt, rt)
        pltpu.make_async_copy(h2t_ref.at[pl.ds(src_row, rt), :],
                              xs_buf.at[buf_slot, pl.ds(dst_row, rt), :], gsem.at[buf_slot]).start()

    def scatter_row(r):
        src_row = pl.multiple_of(r * rt, rt)
        dst_row = pl.multiple_of(prev_ref[r] * rt, rt)
        pltpu.make_async_copy(out_buf.at[1 - slot, pl.ds(src_row, rt), :],
                              y2_ref.at[pl.ds(dst_row, rt), :], ssem).start()

    def wait_scatter():
        pltpu.make_async_copy(out_buf.at[0], out_buf.at[0], ssem).wait()

    @pl.when(jnp.logical_and(i == 0, j == 0))
    def _():
        out_buf[...] = jnp.zeros(out_buf.shape, out_buf.dtype)

        def issue(r, carry):
            gather_row(first_ref, 0, r)
            return carry
        lax.fori_loop(0, tm, issue, 0, unroll=DMA_UNROLL)

    @pl.when(jnp.logical_and(j == 0, i <= nu))
    def _():
        for s in range(2):
            @pl.when(slot == s)
            def _(s=s):
                pltpu.make_async_copy(xs_buf.at[s], xs_buf.at[s], gsem.at[s]).wait()

                @pl.when(used)
                def _():
                    for c in range(rt):
                        xb_ref[:, c * LANES:(c + 1) * LANES] = (
                            xs_buf[s, pl.ds(c, tm, stride=rt), :].astype(BF16))

        @pl.when(i == nu)
        def _():
            def issue(r, carry):
                scatter_row(r)
                return carry
            lax.fori_loop(0, tm, issue, 0, unroll=DMA_UNROLL)
            wait_scatter()

    def compute(step):
        xb = xb_ref[...]
        n_sub = tf // MOE_SUBCHUNK
        rows_step = tm // n_steps
        per = -(-rows_step // n_sub)
        for c in range(n_sub):
            cs = slice(c * MOE_SUBCHUNK, (c + 1) * MOE_SUBCHUNK)
            g = _dot(xb, wg_ref[0, :, cs])
            u = _dot(xb, wu_ref[0, :, cs])
            act_ref[:, cs] = ((g * jax.nn.sigmoid(g)) * u).astype(BF16)
            for r in range(step * rows_step + c * per, step * rows_step + min((c + 1) * per, rows_step)):
                gather_row(nxt_ref, 1 - slot, r)
                scatter_row(r)
        part = _dot(act_ref[...], wd_ref[0])
        if step < n_steps - 1:
            acc_ref[...] = part if step == 0 else acc_ref[...] + part
        else:
            total = part if step == 0 else acc_ref[...] + part
            for c in range(rt):
                out_buf[slot, pl.ds(c, tm, stride=rt), :] = total[:, c * LANES:(c + 1) * LANES]
            wait_scatter()

    for step in range(n_steps):
        @pl.when(jnp.logical_and(used, j == step))
        def _(step=step):
            compute(step)


def _moe(tile_e, n_used, slot_tok, prev_dst, h2t, wgu, wd, n_tok):
    d = wgu.shape[1]
    d_e = wd.shape[1]
    rt = d // LANES
    tm = MOE_TILE
    n_tiles = slot_tok.shape[0] // tm
    nj = d_e // MOE_FCHUNK

    def jf(i, j, nu):
        return jnp.where(i < nu[0], j, nj - 1)

    smem_tile = lambda f: pl.BlockSpec((tm,), f, memory_space=pltpu.SMEM)
    return pl.pallas_call(
        functools.partial(_moe_kernel, n_steps=nj),
        grid_spec=pltpu.PrefetchScalarGridSpec(
            num_scalar_prefetch=2,
            grid=(n_tiles, nj),
            in_specs=[smem_tile(lambda i, j, te, nu: (0,)),
                      smem_tile(lambda i, j, te, nu: (jnp.minimum(i + 1, n_tiles - 1),)),
                      smem_tile(lambda i, j, te, nu: (i,)),
                      pl.BlockSpec(memory_space=pl.ANY),
                      pl.BlockSpec((1, d, MOE_FCHUNK), lambda i, j, te, nu: (te[i], 0, jf(i, j, nu))),
                      pl.BlockSpec((1, d, MOE_FCHUNK), lambda i, j, te, nu: (te[i], 0, nj + jf(i, j, nu))),
                      pl.BlockSpec((1, MOE_FCHUNK, d), lambda i, j, te, nu: (te[i], jf(i, j, nu), 0))],
            out_specs=pl.BlockSpec(memory_space=pl.ANY),
            scratch_shapes=[pltpu.VMEM((2, tm * rt, LANES), F32),
                            pltpu.VMEM((tm, d), BF16),
                            pltpu.VMEM((tm, MOE_FCHUNK), BF16),
                            pltpu.VMEM((tm, d), F32),
                            pltpu.VMEM((2, tm * rt, LANES), F32),
                            pltpu.SemaphoreType.DMA((2,)),
                            pltpu.SemaphoreType.DMA]),
        out_shape=jax.ShapeDtypeStruct(((2 * n_tok + tm) * rt, LANES), F32),
        compiler_params=_params("arbitrary", "arbitrary"),
        name="moe_experts",
    )(tile_e, n_used, slot_tok, slot_tok, prev_dst, h2t, wgu, wgu, wd)


def _combine_kernel(x_ref, ri_ref, mod_ref, fg_ref, ya_ref, yb_ref, o_ref):
    tm, d = x_ref.shape[1], x_ref.shape[2]
    ri = ri_ref[...]
    w1 = ri[:, 2:3]
    w2 = ri[:, 3:4]
    ya = _load_row_tiles(ya_ref, tm, d)
    yb = _load_row_tiles(yb_ref, tm, d)
    moe = jnp.concatenate([w1 * a + w2 * b for a, b in zip(ya, yb)], axis=1)
    x3 = x_ref[0] + mod_ref[0][5:6] * moe
    ms = jnp.mean(x3 * x3, axis=-1, keepdims=True)
    o_ref[0] = (x3 * lax.rsqrt(ms + EPS)) * fg_ref[...]


def _combine(x2, ri, mods, fg, y2):
    bsz, t, d = x2.shape
    tm = COMBINE_TILE
    nt = t // tm
    rt = d // LANES
    n_blocks = bsz * nt
    return pl.pallas_call(
        _combine_kernel,
        grid=(bsz, nt),
        in_specs=[pl.BlockSpec((1, tm, d), lambda b, i: (b, i, 0)),
                  pl.BlockSpec((tm, LANES), lambda b, i: (b * nt + i, 0)),
                  pl.BlockSpec((1, 6, d), lambda b, i: (b, 0, 0)),
                  pl.BlockSpec((1, d), lambda b, i: (0, 0)),
                  pl.BlockSpec((tm * rt, LANES), lambda b, i: (b * nt + i, 0)),
                  pl.BlockSpec((tm * rt, LANES), lambda b, i: (n_blocks + b * nt + i, 0))],
        out_specs=pl.BlockSpec((1, tm, d), lambda b, i: (b, i, 0)),
        out_shape=jax.ShapeDtypeStruct(x2.shape, F32),
        compiler_params=_params("parallel", "parallel"),
        name="moe_combine_norm",
    )(x2, ri, mods, fg, y2, y2)


def _routing(flat_e, n_tok):
    n2 = flat_e.shape[0]
    experts = jnp.arange(N_EXPERTS, dtype=jnp.int32)
    counts = jnp.sum((flat_e[:, None] == experts[None, :]).astype(jnp.int32), axis=0)
    padded = (counts + MOE_TILE - 1) // MOE_TILE * MOE_TILE
    pends = jnp.cumsum(padded)
    n_tiles = n2 // MOE_TILE + N_EXPERTS + 1
    idx_bits = max(n2, MOE_TILE).bit_length()
    r = jnp.arange(MOE_TILE, dtype=jnp.int32)
    pad_e = jnp.where(r[None, :] < (padded - counts)[:, None], experts[:, None], N_EXPERTS)
    pad_e = jnp.concatenate([pad_e.reshape(-1), jnp.full((MOE_TILE,), N_EXPERTS, jnp.int32)])
    keys = jnp.concatenate([
        (flat_e << (idx_bits + 1)) | jnp.arange(n2, dtype=jnp.int32),
        (pad_e << (idx_bits + 1)) | (1 << idx_bits) | jnp.tile(r, N_EXPERTS + 1)])
    keys = jnp.sort(keys)
    valid = ((keys >> idx_bits) & 1) == 0
    f = keys & ((1 << idx_bits) - 1)
    slot = jnp.arange(n_tiles * MOE_TILE, dtype=jnp.int32)
    slot_tok = jnp.where(valid, f % n_tok, 0).astype(jnp.int32)
    slot_dst = jnp.where(valid, f, n2 + slot % MOE_TILE).astype(jnp.int32)
    prev_dst = jnp.concatenate([n2 + jnp.arange(MOE_TILE, dtype=jnp.int32), slot_dst[:-MOE_TILE]])
    tile0 = jnp.arange(n_tiles, dtype=jnp.int32) * MOE_TILE
    tile_e = jnp.minimum(jnp.sum((tile0[:, None] >= pends[None, :]).astype(jnp.int32), axis=1), N_EXPERTS - 1)
    n_used = (pends[-1] // MOE_TILE).astype(jnp.int32).reshape(1)
    return slot_tok, prev_dst, tile_e.astype(jnp.int32), n_used


def kernel(x, c, ctx, c_ctx, ada_w, ada_b, norm1_g, norm2_g, pool_w, pool_scale, ffn_w_gu, ffn_w_down,
           lru_w_in, lru_conv_w, lru_conv_b, lru_w_r, lru_b_r, lru_w_i, lru_b_i, lru_lambda, lru_w_out,
           moe_w_router, moe_w_gu, moe_w_down, final_g):
    bsz, seq, d = x.shape
    assert ada_w.shape[0] == 2 and seq % TOK_TILE == 0 and TOK_TILE % POOL_TILE == 0 and ctx.shape[1] == POOL_TILE
    assert d == SUBLANES * LANES
    d_rnn = lru_w_out.shape[1]
    heads = d_rnn // LRU_BLOCK
    gc = d // len(POOL_WINDOWS)
    z_row = bsz

    n_rows = (bsz + 1 + SUBLANES - 1) // SUBLANES * SUBLANES
    cc = jnp.concatenate([c, c_ctx[None, :], jnp.zeros((n_rows - bsz - 1, d), F32)], axis=0)
    mods = _adaln(cc, ada_w, ada_b).reshape(2, n_rows, 6, d)
    x_row = lambda b: b
    ctx_row = lambda b: z_row

    band_x, invc_x = _pool_tables(GRID_W, gc)
    band_z, invc_z = _pool_tables(POOL_TILE, gc)
    l0 = (norm1_g[0:1], norm2_g[0:1])
    l0w = (pool_w[0].astype(BF16), pool_scale[0:1], ffn_w_gu[0].astype(BF16), ffn_w_down[0].astype(BF16))
    x1 = _layer0(x, mods[0], x_row, *l0, band_x, invc_x, *l0w, name="layer0_latent")
    z1 = _layer0(ctx, mods[0], ctx_row, *l0, band_z, invc_z, *l0w, name="layer0_context")

    w_in = lru_w_in[0].astype(BF16)
    gate_x, ux = _inproj(x1, mods[1], x_row, norm1_g[1:2], w_in, True, name="lru_inproj_latent")
    (uz,) = _inproj(z1, mods[1], ctx_row, norm1_g[1:2], w_in[:, d_rnn:], False, name="lru_inproj_context")
    wcat = (0.5 * jnp.concatenate([lru_w_r[0, 0], lru_w_i[0, 0], lru_w_r[0, 1], lru_w_i[0, 1]], axis=-1)).astype(BF16)
    bcat = 0.5 * jnp.concatenate([v.reshape(heads, 1, LRU_BLOCK)
                                  for v in (lru_b_r[0, 0], lru_b_i[0, 0], lru_b_r[0, 1], lru_b_i[0, 1])], axis=-1)
    y = _lru(uz, ux, gate_x, lru_conv_w[0], lru_conv_b[0:1], wcat, bcat, lru_lambda[0])

    wr_hi = moe_w_router[0].astype(BF16)
    wr_lo = (moe_w_router[0] - wr_hi.astype(F32)).astype(BF16)
    wr2 = jnp.pad(jnp.concatenate([wr_hi, wr_lo], axis=1), ((0, 0), (0, LANES - 2 * N_EXPERTS)))
    x2, h2t, ri, rt = _post(y, x1, mods[1], norm2_g[1:2], lru_w_out[0].astype(BF16), wr2)

    n_tok = bsz * seq
    flat_e = rt[0:2].reshape(-1).astype(jnp.int32)
    slot_tok, prev_dst, tile_e, n_used = _routing(flat_e, n_tok)
    y2 = _moe(tile_e, n_used, slot_tok, prev_dst, h2t, moe_w_gu[0].astype(BF16), moe_w_down[0].astype(BF16), n_tok)
    return _combine(x2, ri, mods[1], final_g[None, :], y2)
```

```python
import functools

import numpy as np
import jax
import jax.numpy as jnp
from jax import lax
from jax.experimental import pallas as pl
from jax.experimental.pallas import tpu as pltpu

F32 = jnp.float32
BF16 = jnp.bfloat16

GRID_W = 64
POOL_WINDOWS = (2, 4, 8, 16)
LRU_BLOCK = 128
LRU_C = 8.0
N_EXPERTS = 8
EPS = 1e-6
LOG2_E = 1.4426950408889634

LANES = 128
SUBLANES = 8
VMEM_LIMIT = 56 * 1024 * 1024

POOL_TILE = 256
TOK_TILE = 512
FF_CHUNK = 256
MOE_TILE = 1024
MOE_FCHUNK = 1792
MOE_SUBCHUNK = 256
DMA_UNROLL = 8
POST_TILE = 1024
POST_SUB = 256
COMBINE_TILE = 256
LRU_SEGS = 64


def _dot(a, b):
    return jnp.dot(a, b, preferred_element_type=F32)


def _params(*sem):
    return pltpu.CompilerParams(dimension_semantics=sem, vmem_limit_bytes=VMEM_LIMIT)


def _resident(shape):
    return pl.BlockSpec(shape, lambda *_: (0,) * len(shape), pipeline_mode=pl.Buffered(1))


def _passenger_specs(cast, grid):
    steps = grid[0] * grid[1]
    n = max(k for k in range(1, steps + 1) if all(w.shape[0] % (2 * SUBLANES * k) == 0 for w in cast))
    chunk = lambda a, b: (jnp.minimum(a * grid[1] + b, n - 1), 0)
    specs = [pl.BlockSpec((w.shape[0] // n, w.shape[1]), chunk) for w in cast]
    return specs, [jax.ShapeDtypeStruct(w.shape, BF16) for w in cast]


def _run_passengers(srcs, dsts):
    for src, dst in zip(srcs, dsts):
        dst[...] = src[...].astype(dst.dtype)


def _norm_mod(x, g, shift, scale):
    ms = jnp.mean(x * x, axis=-1, keepdims=True)
    return (x * lax.rsqrt(ms + EPS)) * (g * (1.0 + scale)) + shift


def _split_bf16(v):
    hi = v.astype(BF16)
    lo = (v - hi.astype(F32)).astype(BF16)
    return hi, lo


def _ada_kernel(cc_ref, w_ref, b_ref, *refs):
    n_cast = len(refs) // 2
    o_ref = refs[n_cast]
    _run_passengers(refs[:n_cast], refs[n_cast + 1:])
    s = cc_ref[...]
    s = s * jax.nn.sigmoid(s)
    o_ref[0] = _dot(s.astype(BF16), w_ref[0].astype(BF16)) + b_ref[0]


def _adaln(cc, ada_w, ada_b, cast=()):
    n_layers, d, n = ada_w.shape
    r = cc.shape[0]
    tn = 1536
    grid = (n_layers, n // tn)
    cast_specs, cast_shapes = _passenger_specs(cast, grid)
    return pl.pallas_call(
        _ada_kernel,
        grid=grid,
        in_specs=[pl.BlockSpec((r, d), lambda l, j: (0, 0)),
                  pl.BlockSpec((1, d, tn), lambda l, j: (l, 0, j)),
                  pl.BlockSpec((1, 1, tn), lambda l, j: (l, 0, j))] + cast_specs,
        out_specs=[pl.BlockSpec((1, r, tn), lambda l, j: (l, 0, j))] + cast_specs,
        out_shape=[jax.ShapeDtypeStruct((n_layers, r, n), F32)] + cast_shapes,
        compiler_params=_params("arbitrary", "arbitrary"),
        name="adaln",
    )(cc, ada_w, ada_b.reshape(n_layers, 1, n), *cast)


def _pool_tables(seg_len, gc):
    t = np.arange(POOL_TILE)
    seg, pos = t // seg_len, t % seg_len
    band = np.zeros((len(POOL_WINDOWS), POOL_TILE, POOL_TILE), np.float32)
    invc = np.zeros((len(POOL_WINDOWS), POOL_TILE, gc), np.float32)
    for gi, win in enumerate(POOL_WINDOWS):
        lo = np.clip(pos - win // 2, 0, seg_len)
        hi = np.clip(pos - win // 2 + win, 0, seg_len)
        inside = (pos[None, :] >= lo[:, None]) & (pos[None, :] < hi[:, None])
        band[gi] = (seg[:, None] == seg[None, :]) & inside
        invc[gi] = (1.0 / (hi - lo))[:, None]
    return jnp.asarray(band, BF16), jnp.asarray(invc, F32)


def _layer0_kernel(x_ref, mod_ref, g1_ref, g2_ref, band_ref, invc_ref, pw_ref, ps_ref,
                   wgu_ref, wd_ref, *refs, d_ff):
    n_cast = (len(refs) - 2) // 2
    o_ref, act_ref = refs[n_cast], refs[-1]
    _run_passengers(refs[:n_cast], refs[n_cast + 1:-1])
    m = mod_ref[0]
    x = x_ref[0]
    tm, d = x.shape
    n_groups = band_ref.shape[0]
    gc = d // n_groups
    h = _norm_mod(x, g1_ref[...], m[0:1], m[1:2])
    rows = []
    for s in range(tm // POOL_TILE):
        hs = h[s * POOL_TILE:(s + 1) * POOL_TILE]
        cols = []
        for gi in range(n_groups):
            hg = hs[:, gi * gc:(gi + 1) * gc]
            hi, lo = _split_bf16(hg)
            band = band_ref[gi]
            pooled = (_dot(band, hi) + _dot(band, lo)) * invc_ref[gi] - hg
            cols.append(_dot(pooled.astype(BF16), pw_ref[gi]))
        rows.append(jnp.concatenate(cols, axis=1))
    y = rows[0] if len(rows) == 1 else jnp.concatenate(rows, axis=0)
    x1 = x + m[2:3] * (y * ps_ref[...])
    h2 = _norm_mod(x1, g2_ref[...], m[3:4], m[4:5]).astype(BF16)
    for c in range(d_ff // FF_CHUNK):
        g = _dot(h2, wgu_ref[:, c * FF_CHUNK:(c + 1) * FF_CHUNK])
        u = _dot(h2, wgu_ref[:, d_ff + c * FF_CHUNK:d_ff + (c + 1) * FF_CHUNK])
        act_ref[:, c * FF_CHUNK:(c + 1) * FF_CHUNK] = ((g * jax.nn.sigmoid(g)) * u).astype(BF16)
    o_ref[0] = x1 + m[5:6] * _dot(act_ref[...], wd_ref[...])


def _layer0(x, mods, mod_row, g1, g2, band, invc, pw, ps, wgu, wd, name, cast=()):
    bsz, t, d = x.shape
    d_ff = wd.shape[0]
    tm = min(TOK_TILE, t)
    grid = (bsz, t // tm)
    cast_specs, cast_shapes = _passenger_specs(cast, grid)
    return pl.pallas_call(
        functools.partial(_layer0_kernel, d_ff=d_ff),
        grid=grid,
        in_specs=[pl.BlockSpec((1, tm, d), lambda b, i: (b, i, 0)),
                  pl.BlockSpec((1, 6, d), lambda b, i: (mod_row(b), 0, 0)),
                  _resident((1, d)), _resident((1, d)),
                  _resident(band.shape), _resident(invc.shape), _resident(pw.shape), _resident((1, d)),
                  _resident(wgu.shape), _resident(wd.shape)] + cast_specs,
        out_specs=[pl.BlockSpec((1, tm, d), lambda b, i: (b, i, 0))] + cast_specs,
        out_shape=[jax.ShapeDtypeStruct(x.shape, F32)] + cast_shapes,
        scratch_shapes=[pltpu.VMEM((tm, d_ff), BF16)],
        compiler_params=_params("arbitrary", "arbitrary"),
        name=name,
    )(x, mods, g1, g2, band, invc, pw, ps, wgu, wd, *cast)


def _inproj_kernel(x_ref, mod_ref, g1_ref, w_ref, *out_refs, with_gate):
    m = mod_ref[0]
    h = _norm_mod(x_ref[0], g1_ref[...], m[0:1], m[1:2]).astype(BF16)
    y = _dot(h, w_ref[...])
    n = out_refs[0].shape[-1]
    if with_gate:
        out_refs[0][0] = jax.nn.gelu(y[:, :n])
        out_refs[1][0] = y[:, n:]
    else:
        out_refs[0][0] = y


def _inproj(x, mods, mod_row, g1, w, with_gate, name):
    bsz, t, d = x.shape
    tm = min(TOK_TILE, t)
    n_out = 2 if with_gate else 1
    n = w.shape[1] // n_out
    out_shape = jax.ShapeDtypeStruct((bsz, t, n), F32)
    out_spec = pl.BlockSpec((1, tm, n), lambda b, i: (b, i, 0))
    return pl.pallas_call(
        functools.partial(_inproj_kernel, with_gate=with_gate),
        grid=(bsz, t // tm),
        in_specs=[pl.BlockSpec((1, tm, d), lambda b, i: (b, i, 0)),
                  pl.BlockSpec((1, 6, d), lambda b, i: (mod_row(b), 0, 0)),
                  _resident((1, d)), _resident(w.shape)],
        out_specs=[out_spec] * n_out,
        out_shape=[out_shape] * n_out,
        compiler_params=_params("parallel", "parallel"),
        name=name,
    )(x, mods, g1, w)


def _scan8(a, b, row, reverse):
    for s in (1, 2, 4):
        if reverse:
            keep = row < SUBLANES - s
            shift = SUBLANES - s
        else:
            keep = row >= s
            shift = s
        a_sh = jnp.where(keep, pltpu.roll(a, shift, 0), 1.0)
        b_sh = jnp.where(keep, pltpu.roll(b, shift, 0), 0.0)
        b = a * b_sh + b
        a = a * a_sh
    return a, b


def _segment_scan(a_ref, b_ref, h_ref, row8, seg_len, reverse):
    nv = LRU_SEGS // SUBLANES

    def rows(k, j):
        jj = seg_len - 1 - j if reverse else j
        return pl.ds(k * SUBLANES * seg_len + jj, SUBLANES, stride=seg_len)

    zero = jnp.zeros((SUBLANES, LANES), F32)
    h = [zero] * nv
    p = [jnp.ones((SUBLANES, LANES), F32)] * nv
    for j in range(seg_len):
        for k in range(nv):
            a = a_ref[rows(k, j), :]
            h[k] = a * h[k] + b_ref[rows(k, j), :]
            p[k] = a * p[k]
    carry = zero
    start = [None] * nv
    for k in (range(nv - 1, -1, -1) if reverse else range(nv)):
        pk, hk = _scan8(p[k], h[k], row8, reverse)
        ends = hk + pk * carry
        if reverse:
            start[k] = jnp.where(row8 == SUBLANES - 1, carry, pltpu.roll(ends, SUBLANES - 1, 0))
            carry = jnp.broadcast_to(ends[0:1], ends.shape)
        else:
            start[k] = jnp.where(row8 == 0, carry, pltpu.roll(ends, 1, 0))
            carry = jnp.broadcast_to(ends[SUBLANES - 1:SUBLANES], ends.shape)
    h = start
    for j in range(seg_len):
        for k in range(nv):
            h[k] = a_ref[rows(k, j), :] * h[k] + b_ref[rows(k, j), :]
            h_ref[rows(k, j), :] = h[k]


def _lru_kernel(uz_ref, ux_ref, gate_ref, cw_ref, cb_ref, wc_ref, bc_ref, lam_ref, *refs, seg_len):
    n_cast = (len(refs) - 8) // 2
    cast_in, (o_ref, *cast_out) = refs[:n_cast], refs[n_cast:2 * n_cast + 1]
    af_ref, bf_ref, ab_ref, bb_ref, hf_ref, hb_ref, pad_ref = refs[2 * n_cast + 1:]
    _run_passengers(cast_in, cast_out)
    lz = uz_ref.shape[1]
    lx = ux_ref.shape[1]
    cw = cw_ref[...]
    cb = cb_ref[...]
    wc = wc_ref[0]
    bc = bc_ref[0]
    neg_lam = -lam_ref[...]
    softplus = jnp.maximum(neg_lam, 0.0) + jnp.log1p(jnp.exp(-jnp.abs(neg_lam)))
    kh = (-0.5 * LRU_C * LOG2_E) * softplus
    zeros8 = jnp.zeros((SUBLANES, LANES), F32)

    def prep(src_ref, off_f, off_b):
        n = src_ref.shape[1]
        pad_ref[pl.ds(0, SUBLANES), :] = zeros8
        pad_ref[pl.ds(SUBLANES, n), :] = src_ref[0]
        pad_ref[pl.ds(SUBLANES + n, SUBLANES), :] = zeros8
        u = cb + cw[2:3] * src_ref[0]
        for k in (0, 1, 3):
            u = u + cw[k:k + 1] * pad_ref[pl.ds(SUBLANES + k - 2, n), :]
        t = jnp.tanh(_dot(u.astype(BF16), wc) + bc)
        hu = 0.5 * u
        for d, (a_ref, b_ref, off) in enumerate(((af_ref, bf_ref, off_f), (ab_ref, bb_ref, off_b))):
            t_r = t[:, 2 * d * LRU_BLOCK:(2 * d + 1) * LRU_BLOCK]
            t_i = t[:, (2 * d + 1) * LRU_BLOCK:(2 * d + 2) * LRU_BLOCK]
            a = jnp.exp2(kh[d:d + 1] * t_r + kh[d:d + 1])
            y = 1.0 - a * a
            root = jnp.where(y > 0.0, y * lax.rsqrt(y), 0.0)
            a_ref[pl.ds(off, n), :] = a
            b_ref[pl.ds(off, n), :] = root * (hu * t_i + hu)

    prep(uz_ref, 0, lx)
    prep(ux_ref, lz, 0)
    row8 = lax.broadcasted_iota(jnp.int32, (SUBLANES, LANES), 0)
    _segment_scan(af_ref, bf_ref, hf_ref, row8, seg_len, False)
    _segment_scan(ab_ref, bb_ref, hb_ref, row8, seg_len, True)
    hsum = hf_ref[pl.ds(lz, lx), :] + hb_ref[pl.ds(0, lx), :]
    o_ref[0] = (hsum * gate_ref[0]).astype(BF16)


def _lru(uz, ux, gate, conv_w, conv_b, wcat, bcat, lam, cast=()):
    bsz, lz, d_rnn = uz.shape
    lx = ux.shape[1]
    heads = d_rnn // LRU_BLOCK
    cast_specs, cast_shapes = _passenger_specs(cast, (bsz, heads))
    seg_len = (lz + lx) // LRU_SEGS
    assert seg_len * LRU_SEGS == lz + lx and seg_len % SUBLANES == SUBLANES // 2
    seq = lambda n: pl.BlockSpec((1, n, LRU_BLOCK), lambda b, h: (b, 0, h))
    scratch = pltpu.VMEM((lz + lx, LRU_BLOCK), F32)
    return pl.pallas_call(
        functools.partial(_lru_kernel, seg_len=seg_len),
        grid=(bsz, heads),
        in_specs=[seq(lz), seq(lx), seq(lx),
                  pl.BlockSpec((conv_w.shape[0], LRU_BLOCK), lambda b, h: (0, h)),
                  pl.BlockSpec((1, LRU_BLOCK), lambda b, h: (0, h)),
                  pl.BlockSpec((1, LRU_BLOCK, 4 * LRU_BLOCK), lambda b, h: (h, 0, 0)),
                  pl.BlockSpec((1, 1, 4 * LRU_BLOCK), lambda b, h: (h, 0, 0)),
                  pl.BlockSpec((2, LRU_BLOCK), lambda b, h: (0, h))] + cast_specs,
        out_specs=[seq(lx)] + cast_specs,
        out_shape=[jax.ShapeDtypeStruct((bsz, lx, d_rnn), BF16)] + cast_shapes,
        scratch_shapes=[scratch] * 6 + [pltpu.VMEM((lx + 2 * SUBLANES, LRU_BLOCK), F32)],
        compiler_params=_params("arbitrary", "arbitrary"),
        name="rglru",
    )(uz, ux, gate, conv_w, conv_b, wcat, bcat, lam, *cast)


def _store_row_tiles(ref, v):
    tm, d = v.shape
    for c in range(d // LANES):
        ref[pl.ds(c, tm, stride=d // LANES), :] = v[:, c * LANES:(c + 1) * LANES]


def _load_row_tiles(ref, tm, d):
    return [ref[pl.ds(c, tm, stride=d // LANES), :] for c in range(d // LANES)]


def _post_kernel(y_ref, x_ref, mod_ref, g2_ref, wo_ref, wr2_ref, x2_ref, h2t_ref, ri_ref, rt_ref):
    m = mod_ref[0]
    tm = x_ref.shape[1]
    rt = x_ref.shape[2] // LANES
    for q in range(tm // POST_SUB):
        rows = pl.ds(q * POST_SUB, POST_SUB)
        x2 = x_ref[0, rows, :] + m[2:3] * _dot(y_ref[0, rows, :], wo_ref[...])
        x2_ref[0, rows, :] = x2
        h2 = _norm_mod(x2, g2_ref[...], m[3:4], m[4:5])
        for c in range(rt):
            h2t_ref[pl.ds(q * POST_SUB * rt + c, POST_SUB, stride=rt), :] = h2[:, c * LANES:(c + 1) * LANES]
        hi, lo = _split_bf16(h2)
        both = _dot(hi, wr2_ref[...]) + _dot(lo, wr2_ref[...])
        logits = both + pltpu.roll(both, LANES - N_EXPERTS, 1)
        lane = lax.broadcasted_iota(jnp.int32, logits.shape, 1)
        logits = jnp.where(lane < N_EXPERTS, logits, -1e30)
        e = jnp.exp(logits - jnp.max(logits, axis=-1, keepdims=True))
        p = e / jnp.sum(e, axis=-1, keepdims=True)
        p1 = jnp.max(p, axis=-1, keepdims=True)
        i1 = jnp.min(jnp.where(p == p1, lane, LANES), axis=-1, keepdims=True)
        rest = jnp.where(lane == i1, -1.0, p)
        p2 = jnp.max(rest, axis=-1, keepdims=True)
        i2 = jnp.min(jnp.where(rest == p2, lane, LANES), axis=-1, keepdims=True)
        den = p1 + p2
        ri = jnp.where(lane == 0, i1.astype(F32),
                       jnp.where(lane == 1, i2.astype(F32),
                                 jnp.where(lane == 2, p1 / den,
                                           jnp.where(lane == 3, p2 / den, 0.0))))
        ri_ref[rows, :] = ri
        rt_ref[:, rows] = jnp.transpose(ri)[0:SUBLANES, :]


def _post(y, x, mods, g2, wo, wr2):
    bsz, t, d = x.shape
    d_rnn = y.shape[-1]
    tm = min(POST_TILE, t)
    nt = t // tm
    rt = d // LANES
    return pl.pallas_call(
        _post_kernel,
        grid=(bsz, nt),
        in_specs=[pl.BlockSpec((1, tm, d_rnn), lambda b, i: (b, i, 0)),
                  pl.BlockSpec((1, tm, d), lambda b, i: (b, i, 0)),
                  pl.BlockSpec((1, 6, d), lambda b, i: (b, 0, 0)),
                  _resident((1, d)), _resident(wo.shape), _resident(wr2.shape)],
        out_specs=[pl.BlockSpec((1, tm, d), lambda b, i: (b, i, 0)),
                   pl.BlockSpec((tm * rt, LANES), lambda b, i: (b * nt + i, 0)),
                   pl.BlockSpec((tm, LANES), lambda b, i: (b * nt + i, 0)),
                   pl.BlockSpec((SUBLANES, tm), lambda b, i: (0, b * nt + i))],
        out_shape=[jax.ShapeDtypeStruct(x.shape, F32),
                   jax.ShapeDtypeStruct((bsz * t * rt, LANES), F32),
                   jax.ShapeDtypeStruct((bsz * t, LANES), F32),
                   jax.ShapeDtypeStruct((SUBLANES, bsz * t), F32)],
        compiler_params=_params("parallel", "parallel"),
        name="outproj_router",
    )(y, x, mods, g2, wo, wr2)


def _moe_kernel(te_ref, nu_ref, first_ref, nxt_ref, prev_ref, h2t_ref, wg_ref, wu_ref, wd_ref, y2_ref,
                xs_buf, xb_ref, act_ref, acc_ref, out_buf, gsem, ssem, *, n_steps):
    i = pl.program_id(0)
    j = pl.program_id(1)
    tm, d = xb_ref.shape
    tf = act_ref.shape[1]
    rt = d // LANES
    nu = nu_ref[0]
    used = i < nu
    slot = i % 2

    def gather_row(idx_ref, buf_slot, r):
        src_row = pl.multiple_of(idx_ref[r] * rt, rt)
        dst_row = pl.multiple_of(r * rt, rt)
        pltpu.make_async_copy(h2t_ref.at[pl.ds(src_row, rt), :],
                              xs_buf.at[buf_slot, pl.ds(dst_row, rt), :], gsem.at[buf_slot]).start()

    def scatter_row(r):
        src_row = pl.multiple_of(r * rt, rt)
        dst_row = pl.multiple_of(prev_ref[r] * rt, rt)
        pltpu.make_async_copy(out_buf.at[1 - slot, pl.ds(src_row, rt), :],
                              y2_ref.at[pl.ds(dst_row, rt), :], ssem).start()

    def wait_scatter():
        pltpu.make_async_copy(out_buf.at[0], out_buf.at[0], ssem).wait()

    @pl.when(jnp.logical_and(i == 0, j == 0))
    def _():
        out_buf[...] = jnp.zeros(out_buf.shape, out_buf.dtype)

        def issue(r, carry):
            gather_row(first_ref, 0, r)
            return carry
        lax.fori_loop(0, tm, issue, 0, unroll=DMA_UNROLL)

    @pl.when(jnp.logical_and(j == 0, i <= nu))
    def _():
        for s in range(2):
            @pl.when(slot == s)
            def _(s=s):
                pltpu.make_async_copy(xs_buf.at[s], xs_buf.at[s], gsem.at[s]).wait()

                @pl.when(used)
                def _():
                    for c in range(rt):
                        xb_ref[:, c * LANES:(c + 1) * LANES] = (
                            xs_buf[s, pl.ds(c, tm, stride=rt), :].astype(BF16))

        @pl.when(i == nu)
        def _():
            def issue(r, carry):
                scatter_row(r)
                return carry
            lax.fori_loop(0, tm, issue, 0, unroll=DMA_UNROLL)
            wait_scatter()

    def compute(step):
        xb = xb_ref[...]
        n_sub = tf // MOE_SUBCHUNK
        rows_step = tm // n_steps
        per = -(-rows_step // n_sub)
        for c in range(n_sub):
            cs = slice(c * MOE_SUBCHUNK, (c + 1) * MOE_SUBCHUNK)
            g = _dot(xb, wg_ref[0, :, cs])
            u = _dot(xb, wu_ref[0, :, cs])
            act_ref[:, cs] = ((g * jax.nn.sigmoid(g)) * u).astype(BF16)
            for r in range(step * rows_step + c * per, step * rows_step + min((c + 1) * per, rows_step)):
                gather_row(nxt_ref, 1 - slot, r)
                scatter_row(r)
        part = _dot(act_ref[...], wd_ref[0])
        if step < n_steps - 1:
            acc_ref[...] = part if step == 0 else acc_ref[...] + part
        else:
            total = part if step == 0 else acc_ref[...] + part
            for c in range(rt):
                out_buf[slot, pl.ds(c, tm, stride=rt), :] = total[:, c * LANES:(c + 1) * LANES]
            wait_scatter()

    for step in range(n_steps):
        @pl.when(jnp.logical_and(used, j == step))
        def _(step=step):
            compute(step)


def _moe(tile_e, n_used, slot_tok, prev_dst, h2t, wgu, wd, n_tok):
    d = wgu.shape[1]
    d_e = wd.shape[1]
    rt = d // LANES
    tm = MOE_TILE
    n_tiles = slot_tok.shape[0] // tm
    nj = d_e // MOE_FCHUNK

    def jf(i, j, nu):
        return jnp.where(i < nu[0], j, nj - 1)

    smem_tile = lambda f: pl.BlockSpec((tm,), f, memory_space=pltpu.SMEM)
    return pl.pallas_call(
        functools.partial(_moe_kernel, n_steps=nj),
        grid_spec=pltpu.PrefetchScalarGridSpec(
            num_scalar_prefetch=2,
            grid=(n_tiles, nj),
            in_specs=[smem_tile(lambda i, j, te, nu: (0,)),
                      smem_tile(lambda i, j, te, nu: (jnp.minimum(i + 1, n_tiles - 1),)),
                      smem_tile(lambda i, j, te, nu: (i,)),
                      pl.BlockSpec(memory_space=pl.ANY),
                      pl.BlockSpec((1, d, MOE_FCHUNK), lambda i, j, te, nu: (te[i], 0, jf(i, j, nu))),
                      pl.BlockSpec((1, d, MOE_FCHUNK), lambda i, j, te, nu: (te[i], 0, nj + jf(i, j, nu))),
                      pl.BlockSpec((1, MOE_FCHUNK, d), lambda i, j, te, nu: (te[i], jf(i, j, nu), 0))],
            out_specs=pl.BlockSpec(memory_space=pl.ANY),
            scratch_shapes=[pltpu.VMEM((2, tm * rt, LANES), F32),
                            pltpu.VMEM((tm, d), BF16),
                            pltpu.VMEM((tm, MOE_FCHUNK), BF16),
                            pltpu.VMEM((tm, d), F32),
                            pltpu.VMEM((2, tm * rt, LANES), F32),
                            pltpu.SemaphoreType.DMA((2,)),
                            pltpu.SemaphoreType.DMA]),
        out_shape=jax.ShapeDtypeStruct(((2 * n_tok + tm) * rt, LANES), F32),
        compiler_params=_params("arbitrary", "arbitrary"),
        name="moe_experts",
    )(tile_e, n_used, slot_tok, slot_tok, prev_dst, h2t, wgu, wgu, wd)


def _combine_kernel(x_ref, ri_ref, mod_ref, fg_ref, ya_ref, yb_ref, o_ref):
    tm, d = x_ref.shape[1], x_ref.shape[2]
    ri = ri_ref[...]
    w1 = ri[:, 2:3]
    w2 = ri[:, 3:4]
    ya = _load_row_tiles(ya_ref, tm, d)
    yb = _load_row_tiles(yb_ref, tm, d)
    moe = jnp.concatenate([w1 * a + w2 * b for a, b in zip(ya, yb)], axis=1)
    x3 = x_ref[0] + mod_ref[0][5:6] * moe
    ms = jnp.mean(x3 * x3, axis=-1, keepdims=True)
    o_ref[0] = (x3 * lax.rsqrt(ms + EPS)) * fg_ref[...]


def _combine(x2, ri, mods, fg, y2):
    bsz, t, d = x2.shape
    tm = COMBINE_TILE
    nt = t // tm
    rt = d // LANES
    n_blocks = bsz * nt
    return pl.pallas_call(
        _combine_kernel,
        grid=(bsz, nt),
        in_specs=[pl.BlockSpec((1, tm, d), lambda b, i: (b, i, 0)),
                  pl.BlockSpec((tm, LANES), lambda b, i: (b * nt + i, 0)),
                  pl.BlockSpec((1, 6, d), lambda b, i: (b, 0, 0)),
                  pl.BlockSpec((1, d), lambda b, i: (0, 0)),
                  pl.BlockSpec((tm * rt, LANES), lambda b, i: (b * nt + i, 0)),
                  pl.BlockSpec((tm * rt, LANES), lambda b, i: (n_blocks + b * nt + i, 0))],
        out_specs=pl.BlockSpec((1, tm, d), lambda b, i: (b, i, 0)),
        out_shape=jax.ShapeDtypeStruct(x2.shape, F32),
        compiler_params=_params("parallel", "parallel"),
        name="moe_combine_norm",
    )(x2, ri, mods, fg, y2, y2)


def _routing(flat_e, n_tok):
    n2 = flat_e.shape[0]
    experts = jnp.arange(N_EXPERTS, dtype=jnp.int32)
    counts = jnp.sum((flat_e[:, None] == experts[None, :]).astype(jnp.int32), axis=0)
    padded = (counts + MOE_TILE - 1) // MOE_TILE * MOE_TILE
    pends = jnp.cumsum(padded)
    n_tiles = n2 // MOE_TILE + N_EXPERTS + 1
    idx_bits = max(n2, MOE_TILE).bit_length()
    r = jnp.arange(MOE_TILE, dtype=jnp.int32)
    pad_e = jnp.where(r[None, :] < (padded - counts)[:, None], experts[:, None], N_EXPERTS)
    pad_e = jnp.concatenate([pad_e.reshape(-1), jnp.full((MOE_TILE,), N_EXPERTS, jnp.int32)])
    keys = jnp.concatenate([
        (flat_e << (idx_bits + 1)) | jnp.arange(n2, dtype=jnp.int32),
        (pad_e << (idx_bits + 1)) | (1 << idx_bits) | jnp.tile(r, N_EXPERTS + 1)])
    keys = jnp.sort(keys)
    valid = ((keys >> idx_bits) & 1) == 0
    f = keys & ((1 << idx_bits) - 1)
    slot = jnp.arange(n_tiles * MOE_TILE, dtype=jnp.int32)
    slot_tok = jnp.where(valid, f % n_tok, 0).astype(jnp.int32)
    slot_dst = jnp.where(valid, f, n2 + slot % MOE_TILE).astype(jnp.int32)
    prev_dst = jnp.concatenate([n2 + jnp.arange(MOE_TILE, dtype=jnp.int32), slot_dst[:-MOE_TILE]])
    tile0 = jnp.arange(n_tiles, dtype=jnp.int32) * MOE_TILE
    tile_e = jnp.minimum(jnp.sum((tile0[:, None] >= pends[None, :]).astype(jnp.int32), axis=1), N_EXPERTS - 1)
    n_used = (pends[-1] // MOE_TILE).astype(jnp.int32).reshape(1)
    return slot_tok, prev_dst, tile_e.astype(jnp.int32), n_used


def kernel(x, c, ctx, c_ctx, ada_w, ada_b, norm1_g, norm2_g, pool_w, pool_scale, ffn_w_gu, ffn_w_down,
           lru_w_in, lru_conv_w, lru_conv_b, lru_w_r, lru_b_r, lru_w_i, lru_b_i, lru_lambda, lru_w_out,
           moe_w_router, moe_w_gu, moe_w_down, final_g):
    bsz, seq, d = x.shape
    assert ada_w.shape[0] == 2 and seq % TOK_TILE == 0 and TOK_TILE % POOL_TILE == 0 and ctx.shape[1] == POOL_TILE
    assert d == SUBLANES * LANES
    d_rnn = lru_w_out.shape[1]
    heads = d_rnn // LRU_BLOCK
    gc = d // len(POOL_WINDOWS)
    z_row = bsz

    n_rows = (bsz + 1 + SUBLANES - 1) // SUBLANES * SUBLANES
    cc = jnp.concatenate([c, c_ctx[None, :], jnp.zeros((n_rows - bsz - 1, d), F32)], axis=0)
    n_grp = pool_w.shape[1]
    mods, wgu0, wd0, pw0 = _adaln(cc, ada_w, ada_b, cast=(ffn_w_gu[0], ffn_w_down[0], pool_w[0].reshape(n_grp * gc, gc)))
    mods = mods.reshape(2, n_rows, 6, d)
    x_row = lambda b: b
    ctx_row = lambda b: z_row

    band_x, invc_x = _pool_tables(GRID_W, gc)
    band_z, invc_z = _pool_tables(POOL_TILE, gc)
    l0 = (norm1_g[0:1], norm2_g[0:1])
    l0w = (pw0.reshape(n_grp, gc, gc), pool_scale[0:1], wgu0, wd0)
    (x1,) = _layer0(x, mods[0], x_row, *l0, band_x, invc_x, *l0w, name="layer0_latent")
    z1, w_in, w_out = _layer0(ctx, mods[0], ctx_row, *l0, band_z, invc_z, *l0w, name="layer0_context",
                              cast=(lru_w_in[0], lru_w_out[0]))

    gate_x, ux = _inproj(x1, mods[1], x_row, norm1_g[1:2], w_in, True, name="lru_inproj_latent")
    (uz,) = _inproj(z1, mods[1], ctx_row, norm1_g[1:2], w_in[:, d_rnn:], False, name="lru_inproj_context")
    wcat = (0.5 * jnp.concatenate([lru_w_r[0, 0], lru_w_i[0, 0], lru_w_r[0, 1], lru_w_i[0, 1]], axis=-1)).astype(BF16)
    bcat = 0.5 * jnp.concatenate([v.reshape(heads, 1, LRU_BLOCK)
                                  for v in (lru_b_r[0, 0], lru_b_i[0, 0], lru_b_r[0, 1], lru_b_i[0, 1])], axis=-1)
    n_exp, _, d_gu = moe_w_gu.shape[1:]
    d_exp = moe_w_down.shape[2]
    y, wgu_e, wd_e = _lru(uz, ux, gate_x, lru_conv_w[0], lru_conv_b[0:1], wcat, bcat, lru_lambda[0],
                          cast=(moe_w_gu[0].reshape(n_exp * d, d_gu), moe_w_down[0].reshape(n_exp * d_exp, d)))
    wgu_e = wgu_e.reshape(n_exp, d, d_gu)
    wd_e = wd_e.reshape(n_exp, d_exp, d)

    wr_hi = moe_w_router[0].astype(BF16)
    wr_lo = (moe_w_router[0] - wr_hi.astype(F32)).astype(BF16)
    wr2 = jnp.pad(jnp.concatenate([wr_hi, wr_lo], axis=1), ((0, 0), (0, LANES - 2 * N_EXPERTS)))
    x2, h2t, ri, rt = _post(y, x1, mods[1], norm2_g[1:2], w_out, wr2)

    n_tok = bsz * seq
    flat_e = rt[0:2].reshape(-1).astype(jnp.int32)
    slot_tok, prev_dst, tile_e, n_used = _routing(flat_e, n_tok)
    y2 = _moe(tile_e, n_used, slot_tok, prev_dst, h2t, wgu_e, wd_e, n_tok)
    return _combine(x2, ri, mods[1], final_g[None, :], y2)
```

```python
import functools

import numpy as np
import jax
import jax.numpy as jnp
from jax import lax
from jax.experimental import pallas as pl
from jax.experimental.pallas import tpu as pltpu

F32 = jnp.float32
BF16 = jnp.bfloat16

GRID_W = 64
POOL_WINDOWS = (2, 4, 8, 16)
LRU_BLOCK = 128
LRU_C = 8.0
N_EXPERTS = 8
EPS = 1e-6
LOG2_E = 1.4426950408889634

LANES = 128
SUBLANES = 8
VMEM_LIMIT = 56 * 1024 * 1024

POOL_TILE = 256
TOK_TILE = 512
FF_CHUNK = 256
MOE_TILE = 1024
MOE_FCHUNK = 1792
MOE_SUBCHUNK = 256
DMA_UNROLL = 8
POST_TILE = 1024
POST_SUB = 256
COMBINE_TILE = 512
LRU_SEGS = 64


def _dot(a, b):
    return jnp.dot(a, b, preferred_element_type=F32)


def _params(*sem):
    return pltpu.CompilerParams(dimension_semantics=sem, vmem_limit_bytes=VMEM_LIMIT)


def _resident(shape):
    return pl.BlockSpec(shape, lambda *_: (0,) * len(shape), pipeline_mode=pl.Buffered(1))


def _passenger_specs(cast, grid):
    steps = grid[0] * grid[1]
    n = max(k for k in range(1, steps + 1) if all(w.shape[0] % (2 * SUBLANES * k) == 0 for w in cast))
    chunk = lambda a, b: (jnp.minimum(a * grid[1] + b, n - 1), 0)
    specs = [pl.BlockSpec((w.shape[0] // n, w.shape[1]), chunk) for w in cast]
    return specs, [jax.ShapeDtypeStruct(w.shape, BF16) for w in cast]


def _run_passengers(srcs, dsts):
    for src, dst in zip(srcs, dsts):
        dst[...] = src[...].astype(dst.dtype)


def _norm_mod(x, g, shift, scale):
    ms = jnp.mean(x * x, axis=-1, keepdims=True)
    return (x * lax.rsqrt(ms + EPS)) * (g * (1.0 + scale)) + shift


def _split_bf16(v):
    hi = v.astype(BF16)
    lo = (v - hi.astype(F32)).astype(BF16)
    return hi, lo


def _ada_kernel(cc_ref, w_ref, b_ref, *refs):
    n_cast = len(refs) // 2
    o_ref = refs[n_cast]
    _run_passengers(refs[:n_cast], refs[n_cast + 1:])
    s = cc_ref[...]
    s = s * jax.nn.sigmoid(s)
    o_ref[0] = _dot(s.astype(BF16), w_ref[0].astype(BF16)) + b_ref[0]


def _adaln(cc, ada_w, ada_b, cast=()):
    n_layers, d, n = ada_w.shape
    r = cc.shape[0]
    tn = 1536
    grid = (n_layers, n // tn)
    cast_specs, cast_shapes = _passenger_specs(cast, grid)
    return pl.pallas_call(
        _ada_kernel,
        grid=grid,
        in_specs=[pl.BlockSpec((r, d), lambda l, j: (0, 0)),
                  pl.BlockSpec((1, d, tn), lambda l, j: (l, 0, j)),
                  pl.BlockSpec((1, 1, tn), lambda l, j: (l, 0, j))] + cast_specs,
        out_specs=[pl.BlockSpec((1, r, tn), lambda l, j: (l, 0, j))] + cast_specs,
        out_shape=[jax.ShapeDtypeStruct((n_layers, r, n), F32)] + cast_shapes,
        compiler_params=_params("arbitrary", "arbitrary"),
        name="adaln",
    )(cc, ada_w, ada_b.reshape(n_layers, 1, n), *cast)


def _pool_tables(seg_len, gc):
    t = np.arange(POOL_TILE)
    seg, pos = t // seg_len, t % seg_len
    band = np.zeros((len(POOL_WINDOWS), POOL_TILE, POOL_TILE), np.float32)
    invc = np.zeros((len(POOL_WINDOWS), POOL_TILE, gc), np.float32)
    for gi, win in enumerate(POOL_WINDOWS):
        lo = np.clip(pos - win // 2, 0, seg_len)
        hi = np.clip(pos - win // 2 + win, 0, seg_len)
        inside = (pos[None, :] >= lo[:, None]) & (pos[None, :] < hi[:, None])
        band[gi] = (seg[:, None] == seg[None, :]) & inside
        invc[gi] = (1.0 / (hi - lo))[:, None]
    return jnp.asarray(band, BF16), jnp.asarray(invc, F32)


def _layer0_kernel(x_ref, mod_ref, g1_ref, g2_ref, band_ref, invc_ref, pw_ref, ps_ref,
                   wgu_ref, wd_ref, *refs, d_ff):
    n_cast = (len(refs) - 2) // 2
    o_ref, act_ref = refs[n_cast], refs[-1]
    _run_passengers(refs[:n_cast], refs[n_cast + 1:-1])
    m = mod_ref[0]
    x = x_ref[0]
    tm, d = x.shape
    n_groups = band_ref.shape[0]
    gc = d // n_groups
    h = _norm_mod(x, g1_ref[...], m[0:1], m[1:2])
    rows = []
    for s in range(tm // POOL_TILE):
        hs = h[s * POOL_TILE:(s + 1) * POOL_TILE]
        cols = []
        for gi in range(n_groups):
            hg = hs[:, gi * gc:(gi + 1) * gc]
            hi, lo = _split_bf16(hg)
            band = band_ref[gi]
            pooled = (_dot(band, hi) + _dot(band, lo)) * invc_ref[gi] - hg
            cols.append(_dot(pooled.astype(BF16), pw_ref[gi]))
        rows.append(jnp.concatenate(cols, axis=1))
    y = rows[0] if len(rows) == 1 else jnp.concatenate(rows, axis=0)
    x1 = x + m[2:3] * (y * ps_ref[...])
    h2 = _norm_mod(x1, g2_ref[...], m[3:4], m[4:5]).astype(BF16)
    for c in range(d_ff // FF_CHUNK):
        g = _dot(h2, wgu_ref[:, c * FF_CHUNK:(c + 1) * FF_CHUNK])
        u = _dot(h2, wgu_ref[:, d_ff + c * FF_CHUNK:d_ff + (c + 1) * FF_CHUNK])
        act_ref[:, c * FF_CHUNK:(c + 1) * FF_CHUNK] = ((g * jax.nn.sigmoid(g)) * u).astype(BF16)
    o_ref[0] = x1 + m[5:6] * _dot(act_ref[...], wd_ref[...])


def _layer0(x, mods, mod_row, g1, g2, band, invc, pw, ps, wgu, wd, name, cast=()):
    bsz, t, d = x.shape
    d_ff = wd.shape[0]
    tm = min(TOK_TILE, t)
    grid = (bsz, t // tm)
    cast_specs, cast_shapes = _passenger_specs(cast, grid)
    return pl.pallas_call(
        functools.partial(_layer0_kernel, d_ff=d_ff),
        grid=grid,
        in_specs=[pl.BlockSpec((1, tm, d), lambda b, i: (b, i, 0)),
                  pl.BlockSpec((1, 6, d), lambda b, i: (mod_row(b), 0, 0)),
                  _resident((1, d)), _resident((1, d)),
                  _resident(band.shape), _resident(invc.shape), _resident(pw.shape), _resident((1, d)),
                  _resident(wgu.shape), _resident(wd.shape)] + cast_specs,
        out_specs=[pl.BlockSpec((1, tm, d), lambda b, i: (b, i, 0))] + cast_specs,
        out_shape=[jax.ShapeDtypeStruct(x.shape, F32)] + cast_shapes,
        scratch_shapes=[pltpu.VMEM((tm, d_ff), BF16)],
        compiler_params=_params("arbitrary", "arbitrary"),
        name=name,
    )(x, mods, g1, g2, band, invc, pw, ps, wgu, wd, *cast)


def _inproj_kernel(x_ref, mod_ref, g1_ref, w_ref, *out_refs, with_gate):
    m = mod_ref[0]
    h = _norm_mod(x_ref[0], g1_ref[...], m[0:1], m[1:2]).astype(BF16)
    y = _dot(h, w_ref[...])
    n = out_refs[0].shape[-1]
    if with_gate:
        out_refs[0][0] = jax.nn.gelu(y[:, :n])
        out_refs[1][0] = y[:, n:]
    else:
        out_refs[0][0] = y


def _inproj(x, mods, mod_row, g1, w, with_gate, name):
    bsz, t, d = x.shape
    tm = min(TOK_TILE, t)
    n_out = 2 if with_gate else 1
    n = w.shape[1] // n_out
    out_shape = jax.ShapeDtypeStruct((bsz, t, n), F32)
    out_spec = pl.BlockSpec((1, tm, n), lambda b, i: (b, i, 0))
    return pl.pallas_call(
        functools.partial(_inproj_kernel, with_gate=with_gate),
        grid=(bsz, t // tm),
        in_specs=[pl.BlockSpec((1, tm, d), lambda b, i: (b, i, 0)),
                  pl.BlockSpec((1, 6, d), lambda b, i: (mod_row(b), 0, 0)),
                  _resident((1, d)), _resident(w.shape)],
        out_specs=[out_spec] * n_out,
        out_shape=[out_shape] * n_out,
        compiler_params=_params("parallel", "parallel"),
        name=name,
    )(x, mods, g1, w)


def _scan8(a, b, row, reverse):
    for s in (1, 2, 4):
        if reverse:
            keep = row < SUBLANES - s
            shift = SUBLANES - s
        else:
            keep = row >= s
            shift = s
        a_sh = jnp.where(keep, pltpu.roll(a, shift, 0), 1.0)
        b_sh = jnp.where(keep, pltpu.roll(b, shift, 0), 0.0)
        b = a * b_sh + b
        a = a * a_sh
    return a, b


def _segment_scan(a_ref, b_ref, h_ref, row8, seg_len, reverse):
    nv = LRU_SEGS // SUBLANES

    def rows(k, j):
        jj = seg_len - 1 - j if reverse else j
        return pl.ds(k * SUBLANES * seg_len + jj, SUBLANES, stride=seg_len)

    zero = jnp.zeros((SUBLANES, LANES), F32)
    h = [zero] * nv
    p = [jnp.ones((SUBLANES, LANES), F32)] * nv
    for j in range(seg_len):
        for k in range(nv):
            a = a_ref[rows(k, j), :]
            h[k] = a * h[k] + b_ref[rows(k, j), :]
            p[k] = a * p[k]
    carry = zero
    start = [None] * nv
    for k in (range(nv - 1, -1, -1) if reverse else range(nv)):
        pk, hk = _scan8(p[k], h[k], row8, reverse)
        ends = hk + pk * carry
        if reverse:
            start[k] = jnp.where(row8 == SUBLANES - 1, carry, pltpu.roll(ends, SUBLANES - 1, 0))
            carry = jnp.broadcast_to(ends[0:1], ends.shape)
        else:
            start[k] = jnp.where(row8 == 0, carry, pltpu.roll(ends, 1, 0))
            carry = jnp.broadcast_to(ends[SUBLANES - 1:SUBLANES], ends.shape)
    h = start
    for j in range(seg_len):
        for k in range(nv):
            h[k] = a_ref[rows(k, j), :] * h[k] + b_ref[rows(k, j), :]
            h_ref[rows(k, j), :] = h[k]


def _lru_kernel(uz_ref, ux_ref, gate_ref, cw_ref, cb_ref, wc_ref, bc_ref, lam_ref, *refs, seg_len):
    n_cast = (len(refs) - 8) // 2
    cast_in, (o_ref, *cast_out) = refs[:n_cast], refs[n_cast:2 * n_cast + 1]
    af_ref, bf_ref, ab_ref, bb_ref, hf_ref, hb_ref, pad_ref = refs[2 * n_cast + 1:]
    _run_passengers(cast_in, cast_out)
    lz = uz_ref.shape[1]
    lx = ux_ref.shape[1]
    cw = cw_ref[...]
    cb = cb_ref[...]
    wc = wc_ref[0]
    bc = bc_ref[0]
    neg_lam = -lam_ref[...]
    softplus = jnp.maximum(neg_lam, 0.0) + jnp.log1p(jnp.exp(-jnp.abs(neg_lam)))
    kh = (-0.5 * LRU_C * LOG2_E) * softplus
    zeros8 = jnp.zeros((SUBLANES, LANES), F32)

    def prep(src_ref, off_f, off_b):
        n = src_ref.shape[1]
        pad_ref[pl.ds(0, SUBLANES), :] = zeros8
        pad_ref[pl.ds(SUBLANES, n), :] = src_ref[0]
        pad_ref[pl.ds(SUBLANES + n, SUBLANES), :] = zeros8
        u = cb + cw[2:3] * src_ref[0]
        for k in (0, 1, 3):
            u = u + cw[k:k + 1] * pad_ref[pl.ds(SUBLANES + k - 2, n), :]
        t = jnp.tanh(_dot(u.astype(BF16), wc) + bc)
        hu = 0.5 * u
        for d, (a_ref, b_ref, off) in enumerate(((af_ref, bf_ref, off_f), (ab_ref, bb_ref, off_b))):
            t_r = t[:, 2 * d * LRU_BLOCK:(2 * d + 1) * LRU_BLOCK]
            t_i = t[:, (2 * d + 1) * LRU_BLOCK:(2 * d + 2) * LRU_BLOCK]
            a = jnp.exp2(kh[d:d + 1] * t_r + kh[d:d + 1])
            y = 1.0 - a * a
            root = jnp.where(y > 0.0, y * lax.rsqrt(y), 0.0)
            a_ref[pl.ds(off, n), :] = a
            b_ref[pl.ds(off, n), :] = root * (hu * t_i + hu)

    prep(uz_ref, 0, lx)
    prep(ux_ref, lz, 0)
    row8 = lax.broadcasted_iota(jnp.int32, (SUBLANES, LANES), 0)
    _segment_scan(af_ref, bf_ref, hf_ref, row8, seg_len, False)
    _segment_scan(ab_ref, bb_ref, hb_ref, row8, seg_len, True)
    hsum = hf_ref[pl.ds(lz, lx), :] + hb_ref[pl.ds(0, lx), :]
    o_ref[0] = (hsum * gate_ref[0]).astype(BF16)


def _lru(uz, ux, gate, conv_w, conv_b, wcat, bcat, lam, cast=()):
    bsz, lz, d_rnn = uz.shape
    lx = ux.shape[1]
    heads = d_rnn // LRU_BLOCK
    cast_specs, cast_shapes = _passenger_specs(cast, (bsz, heads))
    seg_len = (lz + lx) // LRU_SEGS
    assert seg_len * LRU_SEGS == lz + lx and seg_len % SUBLANES == SUBLANES // 2
    seq = lambda n: pl.BlockSpec((1, n, LRU_BLOCK), lambda b, h: (b, 0, h))
    scratch = pltpu.VMEM((lz + lx, LRU_BLOCK), F32)
    return pl.pallas_call(
        functools.partial(_lru_kernel, seg_len=seg_len),
        grid=(bsz, heads),
        in_specs=[seq(lz), seq(lx), seq(lx),
                  pl.BlockSpec((conv_w.shape[0], LRU_BLOCK), lambda b, h: (0, h)),
                  pl.BlockSpec((1, LRU_BLOCK), lambda b, h: (0, h)),
                  pl.BlockSpec((1, LRU_BLOCK, 4 * LRU_BLOCK), lambda b, h: (h, 0, 0)),
                  pl.BlockSpec((1, 1, 4 * LRU_BLOCK), lambda b, h: (h, 0, 0)),
                  pl.BlockSpec((2, LRU_BLOCK), lambda b, h: (0, h))] + cast_specs,
        out_specs=[seq(lx)] + cast_specs,
        out_shape=[jax.ShapeDtypeStruct((bsz, lx, d_rnn), BF16)] + cast_shapes,
        scratch_shapes=[scratch] * 6 + [pltpu.VMEM((lx + 2 * SUBLANES, LRU_BLOCK), F32)],
        compiler_params=_params("arbitrary", "arbitrary"),
        name="rglru",
    )(uz, ux, gate, conv_w, conv_b, wcat, bcat, lam, *cast)


def _store_row_tiles(ref, v):
    tm, d = v.shape
    for c in range(d // LANES):
        ref[pl.ds(c, tm, stride=d // LANES), :] = v[:, c * LANES:(c + 1) * LANES]


def _load_row_tiles(ref, tm, d):
    return [ref[pl.ds(c, tm, stride=d // LANES), :] for c in range(d // LANES)]


def _post_kernel(y_ref, x_ref, mod_ref, g2_ref, wo_ref, wr2_ref, x2_ref, h2t_ref, ri_ref, rt_ref):
    m = mod_ref[0]
    tm = x_ref.shape[1]
    rt = x_ref.shape[2] // LANES
    for q in range(tm // POST_SUB):
        rows = pl.ds(q * POST_SUB, POST_SUB)
        x2 = x_ref[0, rows, :] + m[2:3] * _dot(y_ref[0, rows, :], wo_ref[...])
        x2_ref[0, rows, :] = x2
        h2 = _norm_mod(x2, g2_ref[...], m[3:4], m[4:5])
        for c in range(rt):
            h2t_ref[pl.ds(q * POST_SUB * rt + c, POST_SUB, stride=rt), :] = h2[:, c * LANES:(c + 1) * LANES]
        hi, lo = _split_bf16(h2)
        both = _dot(hi, wr2_ref[...]) + _dot(lo, wr2_ref[...])
        logits = both + pltpu.roll(both, LANES - N_EXPERTS, 1)
        lane = lax.broadcasted_iota(jnp.int32, logits.shape, 1)
        logits = jnp.where(lane < N_EXPERTS, logits, -1e30)
        e = jnp.exp(logits - jnp.max(logits, axis=-1, keepdims=True))
        p = e / jnp.sum(e, axis=-1, keepdims=True)
        p1 = jnp.max(p, axis=-1, keepdims=True)
        i1 = jnp.min(jnp.where(p == p1, lane, LANES), axis=-1, keepdims=True)
        rest = jnp.where(lane == i1, -1.0, p)
        p2 = jnp.max(rest, axis=-1, keepdims=True)
        i2 = jnp.min(jnp.where(rest == p2, lane, LANES), axis=-1, keepdims=True)
        den = p1 + p2
        ri = jnp.where(lane == 0, i1.astype(F32),
                       jnp.where(lane == 1, i2.astype(F32),
                                 jnp.where(lane == 2, p1 / den,
                                           jnp.where(lane == 3, p2 / den, 0.0))))
        ri_ref[rows, :] = ri
        rt_ref[:, rows] = jnp.transpose(ri)[0:SUBLANES, :]


def _post(y, x, mods, g2, wo, wr2):
    bsz, t, d = x.shape
    d_rnn = y.shape[-1]
    tm = min(POST_TILE, t)
    nt = t // tm
    rt = d // LANES
    return pl.pallas_call(
        _post_kernel,
        grid=(bsz, nt),
        in_specs=[pl.BlockSpec((1, tm, d_rnn), lambda b, i: (b, i, 0)),
                  pl.BlockSpec((1, tm, d), lambda b, i: (b, i, 0)),
                  pl.BlockSpec((1, 6, d), lambda b, i: (b, 0, 0)),
                  _resident((1, d)), _resident(wo.shape), _resident(wr2.shape)],
        out_specs=[pl.BlockSpec((1, tm, d), lambda b, i: (b, i, 0)),
                   pl.BlockSpec((tm * rt, LANES), lambda b, i: (b * nt + i, 0)),
                   pl.BlockSpec((tm, LANES), lambda b, i: (b * nt + i, 0)),
                   pl.BlockSpec((SUBLANES, tm), lambda b, i: (0, b * nt + i))],
        out_shape=[jax.ShapeDtypeStruct(x.shape, F32),
                   jax.ShapeDtypeStruct((bsz * t * rt, LANES), F32),
                   jax.ShapeDtypeStruct((bsz * t, LANES), F32),
                   jax.ShapeDtypeStruct((SUBLANES, bsz * t), F32)],
        compiler_params=_params("parallel", "parallel"),
        name="outproj_router",
    )(y, x, mods, g2, wo, wr2)


def _moe_kernel(te_ref, nu_ref, first_ref, nxt_ref, prev_ref, h2t_ref, wg_ref, wu_ref, wd_ref, y2_ref,
                xs_buf, xb_ref, act_ref, acc_ref, out_buf, gsem, ssem, *, n_steps):
    i = pl.program_id(0)
    j = pl.program_id(1)
    tm, d = xb_ref.shape
    tf = act_ref.shape[1]
    rt = d // LANES
    nu = nu_ref[0]
    used = i < nu
    slot = i % 2

    def gather_row(idx_ref, buf_slot, r):
        src_row = pl.multiple_of(idx_ref[r] * rt, rt)
        dst_row = pl.multiple_of(r * rt, rt)
        pltpu.make_async_copy(h2t_ref.at[pl.ds(src_row, rt), :],
                              xs_buf.at[buf_slot, pl.ds(dst_row, rt), :], gsem.at[buf_slot]).start()

    def scatter_row(r, priority=0):
        src_row = pl.multiple_of(r * rt, rt)
        dst_row = pl.multiple_of(prev_ref[r] * rt, rt)
        pltpu.make_async_copy(out_buf.at[1 - slot, pl.ds(src_row, rt), :],
                              y2_ref.at[pl.ds(dst_row, rt), :], ssem).start(priority=priority)

    def wait_scatter():
        pltpu.make_async_copy(out_buf.at[0], out_buf.at[0], ssem).wait()

    @pl.when(jnp.logical_and(i == 0, j == 0))
    def _():
        out_buf[...] = jnp.zeros(out_buf.shape, out_buf.dtype)

        def issue(r, carry):
            gather_row(first_ref, 0, r)
            return carry
        lax.fori_loop(0, tm, issue, 0, unroll=DMA_UNROLL)

    @pl.when(jnp.logical_and(j == 0, i <= nu))
    def _():
        for s in range(2):
            @pl.when(slot == s)
            def _(s=s):
                pltpu.make_async_copy(xs_buf.at[s], xs_buf.at[s], gsem.at[s]).wait()

                @pl.when(used)
                def _():
                    for c in range(rt):
                        xb_ref[:, c * LANES:(c + 1) * LANES] = (
                            xs_buf[s, pl.ds(c, tm, stride=rt), :].astype(BF16))

        @pl.when(i == nu)
        def _():
            def issue(r, carry):
                scatter_row(r)
                return carry
            lax.fori_loop(0, tm, issue, 0, unroll=DMA_UNROLL)
            wait_scatter()

    def compute(step):
        xb = xb_ref[...]
        n_sub = tf // MOE_SUBCHUNK
        rows_step = tm // n_steps
        per = -(-rows_step // n_sub)
        for c in range(n_sub):
            cs = slice(c * MOE_SUBCHUNK, (c + 1) * MOE_SUBCHUNK)
            g = _dot(xb, wg_ref[0, :, cs])
            u = _dot(xb, wu_ref[0, :, cs])
            act_ref[:, cs] = ((g * jax.nn.sigmoid(g)) * u).astype(BF16)
            for r in range(step * rows_step + c * per, step * rows_step + min((c + 1) * per, rows_step)):
                gather_row(nxt_ref, 1 - slot, r)
                scatter_row(r, priority=r % 2)
        part = _dot(act_ref[...], wd_ref[0])
        if step < n_steps - 1:
            acc_ref[...] = part if step == 0 else acc_ref[...] + part
        else:
            total = part if step == 0 else acc_ref[...] + part
            for c in range(rt):
                out_buf[slot, pl.ds(c, tm, stride=rt), :] = total[:, c * LANES:(c + 1) * LANES]
            wait_scatter()

    for step in range(n_steps):
        @pl.when(jnp.logical_and(used, j == step))
        def _(step=step):
            compute(step)


def _moe(tile_e, n_used, slot_tok, prev_dst, h2t, wgu, wd, n_tok):
    d = wgu.shape[1]
    d_e = wd.shape[1]
    rt = d // LANES
    tm = MOE_TILE
    n_tiles = slot_tok.shape[0] // tm
    nj = d_e // MOE_FCHUNK

    def jf(i, j, nu):
        return jnp.where(i < nu[0], j, nj - 1)

    smem_tile = lambda f: pl.BlockSpec((tm,), f, memory_space=pltpu.SMEM)
    return pl.pallas_call(
        functools.partial(_moe_kernel, n_steps=nj),
        grid_spec=pltpu.PrefetchScalarGridSpec(
            num_scalar_prefetch=2,
            grid=(n_tiles, nj),
            in_specs=[smem_tile(lambda i, j, te, nu: (0,)),
                      smem_tile(lambda i, j, te, nu: (jnp.minimum(i + 1, n_tiles - 1),)),
                      smem_tile(lambda i, j, te, nu: (i,)),
                      pl.BlockSpec(memory_space=pl.ANY),
                      pl.BlockSpec((1, d, MOE_FCHUNK), lambda i, j, te, nu: (te[i], 0, jf(i, j, nu))),
                      pl.BlockSpec((1, d, MOE_FCHUNK), lambda i, j, te, nu: (te[i], 0, nj + jf(i, j, nu))),
                      pl.BlockSpec((1, MOE_FCHUNK, d), lambda i, j, te, nu: (te[i], jf(i, j, nu), 0))],
            out_specs=pl.BlockSpec(memory_space=pl.ANY),
            scratch_shapes=[pltpu.VMEM((2, tm * rt, LANES), F32),
                            pltpu.VMEM((tm, d), BF16),
                            pltpu.VMEM((tm, MOE_FCHUNK), BF16),
                            pltpu.VMEM((tm, d), F32),
                            pltpu.VMEM((2, tm * rt, LANES), F32),
                            pltpu.SemaphoreType.DMA((2,)),
                            pltpu.SemaphoreType.DMA]),
        out_shape=jax.ShapeDtypeStruct(((2 * n_tok + tm) * rt, LANES), F32),
        compiler_params=_params("arbitrary", "arbitrary"),
        name="moe_experts",
    )(tile_e, n_used, slot_tok, slot_tok, prev_dst, h2t, wgu, wgu, wd)


def _combine_kernel(x_ref, ri_ref, mod_ref, fg_ref, ya_ref, yb_ref, o_ref):
    tm, d = x_ref.shape[1], x_ref.shape[2]
    ri = ri_ref[...]
    w1 = ri[:, 2:3]
    w2 = ri[:, 3:4]
    ya = _load_row_tiles(ya_ref, tm, d)
    yb = _load_row_tiles(yb_ref, tm, d)
    moe = jnp.concatenate([w1 * a + w2 * b for a, b in zip(ya, yb)], axis=1)
    x3 = x_ref[0] + mod_ref[0][5:6] * moe
    ms = jnp.mean(x3 * x3, axis=-1, keepdims=True)
    o_ref[0] = (x3 * lax.rsqrt(ms + EPS)) * fg_ref[...]


def _combine(x2, ri, mods, fg, y2):
    bsz, t, d = x2.shape
    tm = COMBINE_TILE
    nt = t // tm
    rt = d // LANES
    n_blocks = bsz * nt
    return pl.pallas_call(
        _combine_kernel,
        grid=(bsz, nt),
        in_specs=[pl.BlockSpec((1, tm, d), lambda b, i: (b, i, 0)),
                  pl.BlockSpec((tm, LANES), lambda b, i: (b * nt + i, 0)),
                  pl.BlockSpec((1, 6, d), lambda b, i: (b, 0, 0)),
                  pl.BlockSpec((1, d), lambda b, i: (0, 0)),
                  pl.BlockSpec((tm * rt, LANES), lambda b, i: (b * nt + i, 0)),
                  pl.BlockSpec((tm * rt, LANES), lambda b, i: (n_blocks + b * nt + i, 0))],
        out_specs=pl.BlockSpec((1, tm, d), lambda b, i: (b, i, 0)),
        out_shape=jax.ShapeDtypeStruct(x2.shape, F32),
        compiler_params=_params("parallel", "parallel"),
        name="moe_combine_norm",
    )(x2, ri, mods, fg, y2, y2)


def _routing(flat_e, n_tok):
    n2 = flat_e.shape[0]
    experts = jnp.arange(N_EXPERTS, dtype=jnp.int32)
    counts = jnp.sum((flat_e[:, None] == experts[None, :]).astype(jnp.int32), axis=0)
    padded = (counts + MOE_TILE - 1) // MOE_TILE * MOE_TILE
    pends = jnp.cumsum(padded)
    n_tiles = n2 // MOE_TILE + N_EXPERTS + 1
    idx_bits = max(n2, MOE_TILE).bit_length()
    r = jnp.arange(MOE_TILE, dtype=jnp.int32)
    pad_e = jnp.where(r[None, :] < (padded - counts)[:, None], experts[:, None], N_EXPERTS)
    pad_e = jnp.concatenate([pad_e.reshape(-1), jnp.full((MOE_TILE,), N_EXPERTS, jnp.int32)])
    keys = jnp.concatenate([
        (flat_e << (idx_bits + 1)) | jnp.arange(n2, dtype=jnp.int32),
        (pad_e << (idx_bits + 1)) | (1 << idx_bits) | jnp.tile(r, N_EXPERTS + 1)])
    keys = jnp.sort(keys)
    valid = ((keys >> idx_bits) & 1) == 0
    f = keys & ((1 << idx_bits) - 1)
    slot = jnp.arange(n_tiles * MOE_TILE, dtype=jnp.int32)
    slot_tok = jnp.where(valid, f % n_tok, 0).astype(jnp.int32)
    slot_dst = jnp.where(valid, f, n2 + slot % MOE_TILE).astype(jnp.int32)
    prev_dst = jnp.concatenate([n2 + jnp.arange(MOE_TILE, dtype=jnp.int32), slot_dst[:-MOE_TILE]])
    tile0 = jnp.arange(n_tiles, dtype=jnp.int32) * MOE_TILE
    tile_e = jnp.minimum(jnp.sum((tile0[:, None] >= pends[None, :]).astype(jnp.int32), axis=1), N_EXPERTS - 1)
    n_used = (pends[-1] // MOE_TILE).astype(jnp.int32).reshape(1)
    return slot_tok, prev_dst, tile_e.astype(jnp.int32), n_used


def kernel(x, c, ctx, c_ctx, ada_w, ada_b, norm1_g, norm2_g, pool_w, pool_scale, ffn_w_gu, ffn_w_down,
           lru_w_in, lru_conv_w, lru_conv_b, lru_w_r, lru_b_r, lru_w_i, lru_b_i, lru_lambda, lru_w_out,
           moe_w_router, moe_w_gu, moe_w_down, final_g):
    bsz, seq, d = x.shape
    assert ada_w.shape[0] == 2 and seq % TOK_TILE == 0 and TOK_TILE % POOL_TILE == 0 and ctx.shape[1] == POOL_TILE
    assert d == SUBLANES * LANES
    d_rnn = lru_w_out.shape[1]
    heads = d_rnn // LRU_BLOCK
    gc = d // len(POOL_WINDOWS)
    z_row = bsz

    n_rows = (bsz + 1 + SUBLANES - 1) // SUBLANES * SUBLANES
    cc = jnp.concatenate([c, c_ctx[None, :], jnp.zeros((n_rows - bsz - 1, d), F32)], axis=0)
    n_grp = pool_w.shape[1]
    mods, wgu0, wd0, pw0 = _adaln(cc, ada_w, ada_b, cast=(ffn_w_gu[0], ffn_w_down[0], pool_w[0].reshape(n_grp * gc, gc)))
    mods = mods.reshape(2, n_rows, 6, d)
    x_row = lambda b: b
    ctx_row = lambda b: z_row

    band_x, invc_x = _pool_tables(GRID_W, gc)
    band_z, invc_z = _pool_tables(POOL_TILE, gc)
    l0 = (norm1_g[0:1], norm2_g[0:1])
    l0w = (pw0.reshape(n_grp, gc, gc), pool_scale[0:1], wgu0, wd0)
    n_exp, _, d_gu = moe_w_gu.shape[1:]
    d_exp = moe_w_down.shape[2]
    x1, wgu_e = _layer0(x, mods[0], x_row, *l0, band_x, invc_x, *l0w, name="layer0_latent",
                        cast=(moe_w_gu[0].reshape(n_exp * d, d_gu),))
    z1, w_in, w_out = _layer0(ctx, mods[0], ctx_row, *l0, band_z, invc_z, *l0w, name="layer0_context",
                              cast=(lru_w_in[0], lru_w_out[0]))

    gate_x, ux = _inproj(x1, mods[1], x_row, norm1_g[1:2], w_in, True, name="lru_inproj_latent")
    (uz,) = _inproj(z1, mods[1], ctx_row, norm1_g[1:2], w_in[:, d_rnn:], False, name="lru_inproj_context")
    wcat = (0.5 * jnp.concatenate([lru_w_r[0, 0], lru_w_i[0, 0], lru_w_r[0, 1], lru_w_i[0, 1]], axis=-1)).astype(BF16)
    bcat = 0.5 * jnp.concatenate([v.reshape(heads, 1, LRU_BLOCK)
                                  for v in (lru_b_r[0, 0], lru_b_i[0, 0], lru_b_r[0, 1], lru_b_i[0, 1])], axis=-1)
    y, wd_e = _lru(uz, ux, gate_x, lru_conv_w[0], lru_conv_b[0:1], wcat, bcat, lru_lambda[0],
                   cast=(moe_w_down[0].reshape(n_exp * d_exp, d),))
    wgu_e = wgu_e.reshape(n_exp, d, d_gu)
    wd_e = wd_e.reshape(n_exp, d_exp, d)

    wr_hi = moe_w_router[0].astype(BF16)
    wr_lo = (moe_w_router[0] - wr_hi.astype(F32)).astype(BF16)
    wr2 = jnp.pad(jnp.concatenate([wr_hi, wr_lo], axis=1), ((0, 0), (0, LANES - 2 * N_EXPERTS)))
    x2, h2t, ri, rt = _post(y, x1, mods[1], norm2_g[1:2], w_out, wr2)

    n_tok = bsz * seq
    flat_e = rt[0:2].reshape(-1).astype(jnp.int32)
    slot_tok, prev_dst, tile_e, n_used = _routing(flat_e, n_tok)
    y2 = _moe(tile_e, n_used, slot_tok, prev_dst, h2t, wgu_e, wd_e, n_tok)
    return _combine(x2, ri, mods[1], final_g[None, :], y2)
```

```python
import functools

import numpy as np
import jax
import jax.numpy as jnp
from jax import lax
from jax.experimental import pallas as pl
from jax.experimental.pallas import tpu as pltpu

F32 = jnp.float32
BF16 = jnp.bfloat16

GRID_W = 64
POOL_WINDOWS = (2, 4, 8, 16)
LRU_BLOCK = 128
LRU_C = 8.0
N_EXPERTS = 8
EPS = 1e-6
LOG2_E = 1.4426950408889634

LANES = 128
SUBLANES = 8
VMEM_LIMIT = 56 * 1024 * 1024

POOL_TILE = 256
TOK_TILE = 512
FF_CHUNK = 256
MOE_TILE = 1024
MOE_FCHUNK = 1792
MOE_SUBCHUNK = 256
DMA_UNROLL = 8
POST_TILE = 1024
POST_SUB = 256
COMBINE_TILE = 512
LRU_SEGS = 64


def _dot(a, b):
    return jnp.dot(a, b, preferred_element_type=F32)


def _params(*sem):
    return pltpu.CompilerParams(dimension_semantics=sem, vmem_limit_bytes=VMEM_LIMIT)


def _resident(shape):
    return pl.BlockSpec(shape, lambda *_: (0,) * len(shape), pipeline_mode=pl.Buffered(1))


def _passenger_specs(cast, grid):
    steps = grid[0] * grid[1]
    n = max(k for k in range(1, steps + 1) if all(w.shape[0] % (2 * SUBLANES * k) == 0 for w in cast))
    chunk = lambda a, b: (jnp.minimum(a * grid[1] + b, n - 1), 0)
    specs = [pl.BlockSpec((w.shape[0] // n, w.shape[1]), chunk) for w in cast]
    return specs, [jax.ShapeDtypeStruct(w.shape, BF16) for w in cast]


def _run_passengers(srcs, dsts):
    for src, dst in zip(srcs, dsts):
        dst[...] = src[...].astype(dst.dtype)


def _norm_mod(x, g, shift, scale):
    ms = jnp.mean(x * x, axis=-1, keepdims=True)
    return (x * lax.rsqrt(ms + EPS)) * (g * (1.0 + scale)) + shift


def _split_bf16(v):
    hi = v.astype(BF16)
    lo = (v - hi.astype(F32)).astype(BF16)
    return hi, lo


def _ada_kernel(cc_ref, w_ref, b_ref, *refs):
    n_cast = len(refs) // 2
    o_ref = refs[n_cast]
    _run_passengers(refs[:n_cast], refs[n_cast + 1:])
    s = cc_ref[...]
    s = s * jax.nn.sigmoid(s)
    o_ref[0] = _dot(s.astype(BF16), w_ref[0].astype(BF16)) + b_ref[0]


def _adaln(cc, ada_w, ada_b, cast=()):
    n_layers, d, n = ada_w.shape
    r = cc.shape[0]
    tn = 1536
    grid = (n_layers, n // tn)
    cast_specs, cast_shapes = _passenger_specs(cast, grid)
    return pl.pallas_call(
        _ada_kernel,
        grid=grid,
        in_specs=[pl.BlockSpec((r, d), lambda l, j: (0, 0)),
                  pl.BlockSpec((1, d, tn), lambda l, j: (l, 0, j)),
                  pl.BlockSpec((1, 1, tn), lambda l, j: (l, 0, j))] + cast_specs,
        out_specs=[pl.BlockSpec((1, r, tn), lambda l, j: (l, 0, j))] + cast_specs,
        out_shape=[jax.ShapeDtypeStruct((n_layers, r, n), F32)] + cast_shapes,
        compiler_params=_params("arbitrary", "arbitrary"),
        name="adaln",
    )(cc, ada_w, ada_b.reshape(n_layers, 1, n), *cast)


def _pool_tables(seg_len, gc):
    t = np.arange(POOL_TILE)
    seg, pos = t // seg_len, t % seg_len
    band = np.zeros((len(POOL_WINDOWS), POOL_TILE, POOL_TILE), np.float32)
    invc = np.zeros((len(POOL_WINDOWS), POOL_TILE, gc), np.float32)
    for gi, win in enumerate(POOL_WINDOWS):
        lo = np.clip(pos - win // 2, 0, seg_len)
        hi = np.clip(pos - win // 2 + win, 0, seg_len)
        inside = (pos[None, :] >= lo[:, None]) & (pos[None, :] < hi[:, None])
        band[gi] = (seg[:, None] == seg[None, :]) & inside
        invc[gi] = (1.0 / (hi - lo))[:, None]
    return jnp.asarray(band, BF16), jnp.asarray(invc, F32)


def _layer0_kernel(x_ref, mod_ref, g1_ref, g2_ref, band_ref, invc_ref, pw_ref, ps_ref,
                   wgu_ref, wd_ref, *refs, d_ff):
    n_cast = (len(refs) - 2) // 2
    o_ref, act_ref = refs[n_cast], refs[-1]
    _run_passengers(refs[:n_cast], refs[n_cast + 1:-1])
    m = mod_ref[0]
    x = x_ref[0]
    tm, d = x.shape
    n_groups = band_ref.shape[0]
    gc = d // n_groups
    h = _norm_mod(x, g1_ref[...], m[0:1], m[1:2])
    rows = []
    for s in range(tm // POOL_TILE):
        hs = h[s * POOL_TILE:(s + 1) * POOL_TILE]
        cols = []
        for gi in range(n_groups):
            hg = hs[:, gi * gc:(gi + 1) * gc]
            hi, lo = _split_bf16(hg)
            band = band_ref[gi]
            pooled = (_dot(band, hi) + _dot(band, lo)) * invc_ref[gi] - hg
            cols.append(_dot(pooled.astype(BF16), pw_ref[gi]))
        rows.append(jnp.concatenate(cols, axis=1))
    y = rows[0] if len(rows) == 1 else jnp.concatenate(rows, axis=0)
    x1 = x + m[2:3] * (y * ps_ref[...])
    h2 = _norm_mod(x1, g2_ref[...], m[3:4], m[4:5]).astype(BF16)
    for c in range(d_ff // FF_CHUNK):
        g = _dot(h2, wgu_ref[:, c * FF_CHUNK:(c + 1) * FF_CHUNK])
        u = _dot(h2, wgu_ref[:, d_ff + c * FF_CHUNK:d_ff + (c + 1) * FF_CHUNK])
        act_ref[:, c * FF_CHUNK:(c + 1) * FF_CHUNK] = ((g * jax.nn.sigmoid(g)) * u).astype(BF16)
    o_ref[0] = x1 + m[5:6] * _dot(act_ref[...], wd_ref[...])


def _layer0(x, mods, mod_row, g1, g2, band, invc, pw, ps, wgu, wd, name, cast=()):
    bsz, t, d = x.shape
    d_ff = wd.shape[0]
    tm = min(TOK_TILE, t)
    grid = (bsz, t // tm)
    cast_specs, cast_shapes = _passenger_specs(cast, grid)
    return pl.pallas_call(
        functools.partial(_layer0_kernel, d_ff=d_ff),
        grid=grid,
        in_specs=[pl.BlockSpec((1, tm, d), lambda b, i: (b, i, 0)),
                  pl.BlockSpec((1, 6, d), lambda b, i: (mod_row(b), 0, 0)),
                  _resident((1, d)), _resident((1, d)),
                  _resident(band.shape), _resident(invc.shape), _resident(pw.shape), _resident((1, d)),
                  _resident(wgu.shape), _resident(wd.shape)] + cast_specs,
        out_specs=[pl.BlockSpec((1, tm, d), lambda b, i: (b, i, 0))] + cast_specs,
        out_shape=[jax.ShapeDtypeStruct(x.shape, F32)] + cast_shapes,
        scratch_shapes=[pltpu.VMEM((tm, d_ff), BF16)],
        compiler_params=_params("arbitrary", "arbitrary"),
        name=name,
    )(x, mods, g1, g2, band, invc, pw, ps, wgu, wd, *cast)


def _inproj_kernel(x_ref, mod_ref, g1_ref, w_ref, *out_refs, with_gate):
    m = mod_ref[0]
    h = _norm_mod(x_ref[0], g1_ref[...], m[0:1], m[1:2]).astype(BF16)
    y = _dot(h, w_ref[...])
    n = out_refs[0].shape[-1]
    if with_gate:
        out_refs[0][0] = jax.nn.gelu(y[:, :n])
        out_refs[1][0] = y[:, n:]
    else:
        out_refs[0][0] = y


def _inproj(x, mods, mod_row, g1, w, with_gate, name):
    bsz, t, d = x.shape
    tm = min(TOK_TILE, t)
    n_out = 2 if with_gate else 1
    n = w.shape[1] // n_out
    out_shape = jax.ShapeDtypeStruct((bsz, t, n), F32)
    out_spec = pl.BlockSpec((1, tm, n), lambda b, i: (b, i, 0))
    return pl.pallas_call(
        functools.partial(_inproj_kernel, with_gate=with_gate),
        grid=(bsz, t // tm),
        in_specs=[pl.BlockSpec((1, tm, d), lambda b, i: (b, i, 0)),
                  pl.BlockSpec((1, 6, d), lambda b, i: (mod_row(b), 0, 0)),
                  _resident((1, d)), _resident(w.shape)],
        out_specs=[out_spec] * n_out,
        out_shape=[out_shape] * n_out,
        compiler_params=_params("parallel", "parallel"),
        name=name,
    )(x, mods, g1, w)


def _scan8(a, b, row, reverse):
    for s in (1, 2, 4):
        if reverse:
            keep = row < SUBLANES - s
            shift = SUBLANES - s
        else:
            keep = row >= s
            shift = s
        a_sh = jnp.where(keep, pltpu.roll(a, shift, 0), 1.0)
        b_sh = jnp.where(keep, pltpu.roll(b, shift, 0), 0.0)
        b = a * b_sh + b
        a = a * a_sh
    return a, b


def _segment_scan(a_ref, b_ref, h_ref, row8, seg_len, reverse):
    nv = LRU_SEGS // SUBLANES

    def rows(k, j):
        jj = seg_len - 1 - j if reverse else j
        return pl.ds(k * SUBLANES * seg_len + jj, SUBLANES, stride=seg_len)

    zero = jnp.zeros((SUBLANES, LANES), F32)
    h = [zero] * nv
    p = [jnp.ones((SUBLANES, LANES), F32)] * nv
    for j in range(seg_len):
        for k in range(nv):
            a = a_ref[rows(k, j), :]
            h[k] = a * h[k] + b_ref[rows(k, j), :]
            p[k] = a * p[k]
    carry = zero
    start = [None] * nv
    for k in (range(nv - 1, -1, -1) if reverse else range(nv)):
        pk, hk = _scan8(p[k], h[k], row8, reverse)
        ends = hk + pk * carry
        if reverse:
            start[k] = jnp.where(row8 == SUBLANES - 1, carry, pltpu.roll(ends, SUBLANES - 1, 0))
            carry = jnp.broadcast_to(ends[0:1], ends.shape)
        else:
            start[k] = jnp.where(row8 == 0, carry, pltpu.roll(ends, 1, 0))
            carry = jnp.broadcast_to(ends[SUBLANES - 1:SUBLANES], ends.shape)
    h = start
    for j in range(seg_len):
        for k in range(nv):
            h[k] = a_ref[rows(k, j), :] * h[k] + b_ref[rows(k, j), :]
            h_ref[rows(k, j), :] = h[k]


def _lru_kernel(uz_ref, ux_ref, gate_ref, cw_ref, cb_ref, wc_ref, bc_ref, lam_ref, *refs, seg_len):
    n_cast = (len(refs) - 8) // 2
    cast_in, (o_ref, *cast_out) = refs[:n_cast], refs[n_cast:2 * n_cast + 1]
    af_ref, bf_ref, ab_ref, bb_ref, hf_ref, hb_ref, pad_ref = refs[2 * n_cast + 1:]
    _run_passengers(cast_in, cast_out)
    lz = uz_ref.shape[1]
    lx = ux_ref.shape[1]
    cw = cw_ref[...]
    cb = cb_ref[...]
    wc = wc_ref[0]
    bc = bc_ref[0]
    neg_lam = -lam_ref[...]
    softplus = jnp.maximum(neg_lam, 0.0) + jnp.log1p(jnp.exp(-jnp.abs(neg_lam)))
    kh = (-0.5 * LRU_C * LOG2_E) * softplus
    zeros8 = jnp.zeros((SUBLANES, LANES), F32)

    def prep(src_ref, off_f, off_b):
        n = src_ref.shape[1]
        pad_ref[pl.ds(0, SUBLANES), :] = zeros8
        pad_ref[pl.ds(SUBLANES, n), :] = src_ref[0]
        pad_ref[pl.ds(SUBLANES + n, SUBLANES), :] = zeros8
        u = cb + cw[2:3] * src_ref[0]
        for k in (0, 1, 3):
            u = u + cw[k:k + 1] * pad_ref[pl.ds(SUBLANES + k - 2, n), :]
        t = jnp.tanh(_dot(u.astype(BF16), wc) + bc)
        hu = 0.5 * u
        for d, (a_ref, b_ref, off) in enumerate(((af_ref, bf_ref, off_f), (ab_ref, bb_ref, off_b))):
            t_r = t[:, 2 * d * LRU_BLOCK:(2 * d + 1) * LRU_BLOCK]
            t_i = t[:, (2 * d + 1) * LRU_BLOCK:(2 * d + 2) * LRU_BLOCK]
            a = jnp.exp2(kh[d:d + 1] * t_r + kh[d:d + 1])
            y = 1.0 - a * a
            root = jnp.where(y > 0.0, y * lax.rsqrt(y), 0.0)
            a_ref[pl.ds(off, n), :] = a
            b_ref[pl.ds(off, n), :] = root * (hu * t_i + hu)

    prep(uz_ref, 0, lx)
    prep(ux_ref, lz, 0)
    row8 = lax.broadcasted_iota(jnp.int32, (SUBLANES, LANES), 0)
    _segment_scan(af_ref, bf_ref, hf_ref, row8, seg_len, False)
    _segment_scan(ab_ref, bb_ref, hb_ref, row8, seg_len, True)
    hsum = hf_ref[pl.ds(lz, lx), :] + hb_ref[pl.ds(0, lx), :]
    o_ref[0] = (hsum * gate_ref[0]).astype(BF16)


def _lru(uz, ux, gate, conv_w, conv_b, wcat, bcat, lam, cast=()):
    bsz, lz, d_rnn = uz.shape
    lx = ux.shape[1]
    heads = d_rnn // LRU_BLOCK
    cast_specs, cast_shapes = _passenger_specs(cast, (bsz, heads))
    seg_len = (lz + lx) // LRU_SEGS
    assert seg_len * LRU_SEGS == lz + lx and seg_len % SUBLANES == SUBLANES // 2
    seq = lambda n: pl.BlockSpec((1, n, LRU_BLOCK), lambda b, h: (b, 0, h))
    scratch = pltpu.VMEM((lz + lx, LRU_BLOCK), F32)
    return pl.pallas_call(
        functools.partial(_lru_kernel, seg_len=seg_len),
        grid=(bsz, heads),
        in_specs=[seq(lz), seq(lx), seq(lx),
                  pl.BlockSpec((conv_w.shape[0], LRU_BLOCK), lambda b, h: (0, h)),
                  pl.BlockSpec((1, LRU_BLOCK), lambda b, h: (0, h)),
                  pl.BlockSpec((1, LRU_BLOCK, 4 * LRU_BLOCK), lambda b, h: (h, 0, 0)),
                  pl.BlockSpec((1, 1, 4 * LRU_BLOCK), lambda b, h: (h, 0, 0)),
                  pl.BlockSpec((2, LRU_BLOCK), lambda b, h: (0, h))] + cast_specs,
        out_specs=[seq(lx)] + cast_specs,
        out_shape=[jax.ShapeDtypeStruct((bsz, lx, d_rnn), BF16)] + cast_shapes,
        scratch_shapes=[scratch] * 6 + [pltpu.VMEM((lx + 2 * SUBLANES, LRU_BLOCK), F32)],
        compiler_params=_params("arbitrary", "arbitrary"),
        name="rglru",
    )(uz, ux, gate, conv_w, conv_b, wcat, bcat, lam, *cast)


def _store_row_tiles(ref, v):
    tm, d = v.shape
    for c in range(d // LANES):
        ref[pl.ds(c, tm, stride=d // LANES), :] = v[:, c * LANES:(c + 1) * LANES]


def _load_row_tiles(ref, tm, d):
    return [ref[pl.ds(c, tm, stride=d // LANES), :] for c in range(d // LANES)]


def _post_kernel(y_ref, x_ref, mod_ref, g2_ref, wo_ref, wr2_ref, x2_ref, h2t_ref, ri_ref, rt_ref):
    m = mod_ref[0]
    tm = x_ref.shape[1]
    rt = x_ref.shape[2] // LANES
    for q in range(tm // POST_SUB):
        rows = pl.ds(q * POST_SUB, POST_SUB)
        x2 = x_ref[0, rows, :] + m[2:3] * _dot(y_ref[0, rows, :], wo_ref[...])
        x2_ref[0, rows, :] = x2
        h2 = _norm_mod(x2, g2_ref[...], m[3:4], m[4:5])
        for c in range(rt):
            h2t_ref[pl.ds(q * POST_SUB * rt + c, POST_SUB, stride=rt), :] = h2[:, c * LANES:(c + 1) * LANES]
        hi, lo = _split_bf16(h2)
        both = _dot(hi, wr2_ref[...]) + _dot(lo, wr2_ref[...])
        logits = both + pltpu.roll(both, LANES - N_EXPERTS, 1)
        lane = lax.broadcasted_iota(jnp.int32, logits.shape, 1)
        logits = jnp.where(lane < N_EXPERTS, logits, -1e30)
        e = jnp.exp(logits - jnp.max(logits, axis=-1, keepdims=True))
        p = e / jnp.sum(e, axis=-1, keepdims=True)
        p1 = jnp.max(p, axis=-1, keepdims=True)
        i1 = jnp.min(jnp.where(p == p1, lane, LANES), axis=-1, keepdims=True)
        rest = jnp.where(lane == i1, -1.0, p)
        p2 = jnp.max(rest, axis=-1, keepdims=True)
        i2 = jnp.min(jnp.where(rest == p2, lane, LANES), axis=-1, keepdims=True)
        den = p1 + p2
        ri = jnp.where(lane == 0, i1.astype(F32),
                       jnp.where(lane == 1, i2.astype(F32),
                                 jnp.where(lane == 2, p1 / den,
                                           jnp.where(lane == 3, p2 / den, 0.0))))
        ri_ref[rows, :] = ri
        rt_ref[:, rows] = jnp.transpose(ri)[0:SUBLANES, :]


def _post(y, x, mods, g2, wo, wr2):
    bsz, t, d = x.shape
    d_rnn = y.shape[-1]
    tm = min(POST_TILE, t)
    nt = t // tm
    rt = d // LANES
    return pl.pallas_call(
        _post_kernel,
        grid=(bsz, nt),
        in_specs=[pl.BlockSpec((1, tm, d_rnn), lambda b, i: (b, i, 0)),
                  pl.BlockSpec((1, tm, d), lambda b, i: (b, i, 0)),
                  pl.BlockSpec((1, 6, d), lambda b, i: (b, 0, 0)),
                  _resident((1, d)), _resident(wo.shape), _resident(wr2.shape)],
        out_specs=[pl.BlockSpec((1, tm, d), lambda b, i: (b, i, 0)),
                   pl.BlockSpec((tm * rt, LANES), lambda b, i: (b * nt + i, 0)),
                   pl.BlockSpec((tm, LANES), lambda b, i: (b * nt + i, 0)),
                   pl.BlockSpec((SUBLANES, tm), lambda b, i: (0, b * nt + i))],
        out_shape=[jax.ShapeDtypeStruct(x.shape, F32),
                   jax.ShapeDtypeStruct((bsz * t * rt, LANES), F32),
                   jax.ShapeDtypeStruct((bsz * t, LANES), F32),
                   jax.ShapeDtypeStruct((SUBLANES, bsz * t), F32)],
        compiler_params=_params("parallel", "parallel"),
        name="outproj_router",
    )(y, x, mods, g2, wo, wr2)


def _moe_kernel(te_ref, nu_ref, first_ref, nxt_ref, prev_ref, h2t_ref, wg_ref, wu_ref, wd_ref, y2_ref,
                xs_buf, xb_ref, act_ref, acc_ref, out_buf, gsem, ssem, *, n_steps):
    i = pl.program_id(0)
    j = pl.program_id(1)
    tm, d = xb_ref.shape
    tf = act_ref.shape[1]
    rt = d // LANES
    nu = nu_ref[0]
    used = i < nu
    slot = i % 2

    def gather_row(idx_ref, buf_slot, r, priority=0):
        src_row = pl.multiple_of(idx_ref[r] * rt, rt)
        dst_row = pl.multiple_of(r * rt, rt)
        pltpu.make_async_copy(h2t_ref.at[pl.ds(src_row, rt), :],
                              xs_buf.at[buf_slot, pl.ds(dst_row, rt), :],
                              gsem.at[buf_slot]).start(priority=priority)

    def scatter_row(r, priority=0):
        src_row = pl.multiple_of(r * rt, rt)
        dst_row = pl.multiple_of(prev_ref[r] * rt, rt)
        pltpu.make_async_copy(out_buf.at[1 - slot, pl.ds(src_row, rt), :],
                              y2_ref.at[pl.ds(dst_row, rt), :], ssem).start(priority=priority)

    def wait_scatter():
        pltpu.make_async_copy(out_buf.at[0], out_buf.at[0], ssem).wait()

    @pl.when(jnp.logical_and(i == 0, j == 0))
    def _():
        out_buf[...] = jnp.zeros(out_buf.shape, out_buf.dtype)

        def issue(r, carry):
            gather_row(first_ref, 0, r)
            return carry
        lax.fori_loop(0, tm, issue, 0, unroll=DMA_UNROLL)

    @pl.when(jnp.logical_and(j == 0, i <= nu))
    def _():
        for s in range(2):
            @pl.when(slot == s)
            def _(s=s):
                pltpu.make_async_copy(xs_buf.at[s], xs_buf.at[s], gsem.at[s]).wait()

                @pl.when(used)
                def _():
                    for c in range(rt):
                        xb_ref[:, c * LANES:(c + 1) * LANES] = (
                            xs_buf[s, pl.ds(c, tm, stride=rt), :].astype(BF16))

        @pl.when(i == nu)
        def _():
            def issue(r, carry):
                scatter_row(r)
                return carry
            lax.fori_loop(0, tm, issue, 0, unroll=DMA_UNROLL)
            wait_scatter()

    def compute(step):
        xb = xb_ref[...]
        n_sub = tf // MOE_SUBCHUNK
        rows_step = tm // n_steps
        n_bursts = 2 * n_sub
        per = -(-rows_step // n_bursts)

        def burst(k):
            for r in range(step * rows_step + k * per, step * rows_step + min((k + 1) * per, rows_step)):
                gather_row(nxt_ref, 1 - slot, r, priority=r % 2)
                scatter_row(r, priority=(r + 1) % 2)

        for c in range(n_sub):
            cs = slice(c * MOE_SUBCHUNK, (c + 1) * MOE_SUBCHUNK)
            g = _dot(xb, wg_ref[0, :, cs])
            burst(2 * c)
            u = _dot(xb, wu_ref[0, :, cs])
            burst(2 * c + 1)
            act_ref[:, cs] = ((g * jax.nn.sigmoid(g)) * u).astype(BF16)
        part = _dot(act_ref[...], wd_ref[0])
        if step < n_steps - 1:
            acc_ref[...] = part if step == 0 else acc_ref[...] + part
        else:
            total = part if step == 0 else acc_ref[...] + part
            for c in range(rt):
                out_buf[slot, pl.ds(c, tm, stride=rt), :] = total[:, c * LANES:(c + 1) * LANES]
            wait_scatter()

    for step in range(n_steps):
        @pl.when(jnp.logical_and(used, j == step))
        def _(step=step):
            compute(step)


def _moe(tile_e, n_used, slot_tok, prev_dst, h2t, wgu, wd, n_tok):
    d = wgu.shape[1]
    d_e = wd.shape[1]
    rt = d // LANES
    tm = MOE_TILE
    n_tiles = slot_tok.shape[0] // tm
    nj = d_e // MOE_FCHUNK

    def jf(i, j, nu):
        return jnp.where(i < nu[0], j, nj - 1)

    smem_tile = lambda f: pl.BlockSpec((tm,), f, memory_space=pltpu.SMEM)
    return pl.pallas_call(
        functools.partial(_moe_kernel, n_steps=nj),
        grid_spec=pltpu.PrefetchScalarGridSpec(
            num_scalar_prefetch=2,
            grid=(n_tiles, nj),
            in_specs=[smem_tile(lambda i, j, te, nu: (0,)),
                      smem_tile(lambda i, j, te, nu: (jnp.minimum(i + 1, n_tiles - 1),)),
                      smem_tile(lambda i, j, te, nu: (i,)),
                      pl.BlockSpec(memory_space=pl.ANY),
                      pl.BlockSpec((1, d, MOE_FCHUNK), lambda i, j, te, nu: (te[i], 0, jf(i, j, nu))),
                      pl.BlockSpec((1, d, MOE_FCHUNK), lambda i, j, te, nu: (te[i], 0, nj + jf(i, j, nu))),
                      pl.BlockSpec((1, MOE_FCHUNK, d), lambda i, j, te, nu: (te[i], jf(i, j, nu), 0))],
            out_specs=pl.BlockSpec(memory_space=pl.ANY),
            scratch_shapes=[pltpu.VMEM((2, tm * rt, LANES), F32),
                            pltpu.VMEM((tm, d), BF16),
                            pltpu.VMEM((tm, MOE_FCHUNK), BF16),
                            pltpu.VMEM((tm, d), F32),
                            pltpu.VMEM((2, tm * rt, LANES), F32),
                            pltpu.SemaphoreType.DMA((2,)),
                            pltpu.SemaphoreType.DMA]),
        out_shape=jax.ShapeDtypeStruct(((2 * n_tok + tm) * rt, LANES), F32),
        compiler_params=_params("arbitrary", "arbitrary"),
        name="moe_experts",
    )(tile_e, n_used, slot_tok, slot_tok, prev_dst, h2t, wgu, wgu, wd)


def _combine_kernel(x_ref, ri_ref, mod_ref, fg_ref, ya_ref, yb_ref, o_ref):
    tm, d = x_ref.shape[1], x_ref.shape[2]
    ri = ri_ref[...]
    w1 = ri[:, 2:3]
    w2 = ri[:, 3:4]
    ya = _load_row_tiles(ya_ref, tm, d)
    yb = _load_row_tiles(yb_ref, tm, d)
    moe = jnp.concatenate([w1 * a + w2 * b for a, b in zip(ya, yb)], axis=1)
    x3 = x_ref[0] + mod_ref[0][5:6] * moe
    ms = jnp.mean(x3 * x3, axis=-1, keepdims=True)
    o_ref[0] = (x3 * lax.rsqrt(ms + EPS)) * fg_ref[...]


def _combine(x2, ri, mods, fg, y2):
    bsz, t, d = x2.shape
    tm = COMBINE_TILE
    nt = t // tm
    rt = d // LANES
    n_blocks = bsz * nt
    return pl.pallas_call(
        _combine_kernel,
        grid=(bsz, nt),
        in_specs=[pl.BlockSpec((1, tm, d), lambda b, i: (b, i, 0)),
                  pl.BlockSpec((tm, LANES), lambda b, i: (b * nt + i, 0)),
                  pl.BlockSpec((1, 6, d), lambda b, i: (b, 0, 0)),
                  pl.BlockSpec((1, d), lambda b, i: (0, 0)),
                  pl.BlockSpec((tm * rt, LANES), lambda b, i: (b * nt + i, 0)),
                  pl.BlockSpec((tm * rt, LANES), lambda b, i: (n_blocks + b * nt + i, 0))],
        out_specs=pl.BlockSpec((1, tm, d), lambda b, i: (b, i, 0)),
        out_shape=jax.ShapeDtypeStruct(x2.shape, F32),
        compiler_params=_params("parallel", "parallel"),
        name="moe_combine_norm",
    )(x2, ri, mods, fg, y2, y2)


def _routing(flat_e, n_tok):
    n2 = flat_e.shape[0]
    experts = jnp.arange(N_EXPERTS, dtype=jnp.int32)
    counts = jnp.sum((flat_e[:, None] == experts[None, :]).astype(jnp.int32), axis=0)
    padded = (counts + MOE_TILE - 1) // MOE_TILE * MOE_TILE
    pends = jnp.cumsum(padded)
    n_tiles = n2 // MOE_TILE + N_EXPERTS + 1
    idx_bits = max(n2, MOE_TILE).bit_length()
    r = jnp.arange(MOE_TILE, dtype=jnp.int32)
    pad_e = jnp.where(r[None, :] < (padded - counts)[:, None], experts[:, None], N_EXPERTS)
    pad_e = jnp.concatenate([pad_e.reshape(-1), jnp.full((MOE_TILE,), N_EXPERTS, jnp.int32)])
    keys = jnp.concatenate([
        (flat_e << (idx_bits + 1)) | jnp.arange(n2, dtype=jnp.int32),
        (pad_e << (idx_bits + 1)) | (1 << idx_bits) | jnp.tile(r, N_EXPERTS + 1)])
    keys = jnp.sort(keys)
    valid = ((keys >> idx_bits) & 1) == 0
    f = keys & ((1 << idx_bits) - 1)
    slot = jnp.arange(n_tiles * MOE_TILE, dtype=jnp.int32)
    slot_tok = jnp.where(valid, f % n_tok, 0).astype(jnp.int32)
    slot_dst = jnp.where(valid, f, n2 + slot % MOE_TILE).astype(jnp.int32)
    prev_dst = jnp.concatenate([n2 + jnp.arange(MOE_TILE, dtype=jnp.int32), slot_dst[:-MOE_TILE]])
    tile0 = jnp.arange(n_tiles, dtype=jnp.int32) * MOE_TILE
    tile_e = jnp.minimum(jnp.sum((tile0[:, None] >= pends[None, :]).astype(jnp.int32), axis=1), N_EXPERTS - 1)
    n_used = (pends[-1] // MOE_TILE).astype(jnp.int32).reshape(1)
    return slot_tok, prev_dst, tile_e.astype(jnp.int32), n_used


def kernel(x, c, ctx, c_ctx, ada_w, ada_b, norm1_g, norm2_g, pool_w, pool_scale, ffn_w_gu, ffn_w_down,
           lru_w_in, lru_conv_w, lru_conv_b, lru_w_r, lru_b_r, lru_w_i, lru_b_i, lru_lambda, lru_w_out,
           moe_w_router, moe_w_gu, moe_w_down, final_g):
    bsz, seq, d = x.shape
    assert ada_w.shape[0] == 2 and seq % TOK_TILE == 0 and TOK_TILE % POOL_TILE == 0 and ctx.shape[1] == POOL_TILE
    assert d == SUBLANES * LANES
    d_rnn = lru_w_out.shape[1]
    heads = d_rnn // LRU_BLOCK
    gc = d // len(POOL_WINDOWS)
    z_row = bsz

    n_rows = (bsz + 1 + SUBLANES - 1) // SUBLANES * SUBLANES
    cc = jnp.concatenate([c, c_ctx[None, :], jnp.zeros((n_rows - bsz - 1, d), F32)], axis=0)
    n_grp = pool_w.shape[1]
    mods, wgu0, wd0, pw0 = _adaln(cc, ada_w, ada_b, cast=(ffn_w_gu[0], ffn_w_down[0], pool_w[0].reshape(n_grp * gc, gc)))
    mods = mods.reshape(2, n_rows, 6, d)
    x_row = lambda b: b
    ctx_row = lambda b: z_row

    band_x, invc_x = _pool_tables(GRID_W, gc)
    band_z, invc_z = _pool_tables(POOL_TILE, gc)
    l0 = (norm1_g[0:1], norm2_g[0:1])
    l0w = (pw0.reshape(n_grp, gc, gc), pool_scale[0:1], wgu0, wd0)
    n_exp, _, d_gu = moe_w_gu.shape[1:]
    d_exp = moe_w_down.shape[2]
    x1, wgu_e = _layer0(x, mods[0], x_row, *l0, band_x, invc_x, *l0w, name="layer0_latent",
                        cast=(moe_w_gu[0].reshape(n_exp * d, d_gu),))
    z1, w_in, w_out = _layer0(ctx, mods[0], ctx_row, *l0, band_z, invc_z, *l0w, name="layer0_context",
                              cast=(lru_w_in[0], lru_w_out[0]))

    gate_x, ux = _inproj(x1, mods[1], x_row, norm1_g[1:2], w_in, True, name="lru_inproj_latent")
    (uz,) = _inproj(z1, mods[1], ctx_row, norm1_g[1:2], w_in[:, d_rnn:], False, name="lru_inproj_context")
    wcat = (0.5 * jnp.concatenate([lru_w_r[0, 0], lru_w_i[0, 0], lru_w_r[0, 1], lru_w_i[0, 1]], axis=-1)).astype(BF16)
    bcat = 0.5 * jnp.concatenate([v.reshape(heads, 1, LRU_BLOCK)
                                  for v in (lru_b_r[0, 0], lru_b_i[0, 0], lru_b_r[0, 1], lru_b_i[0, 1])], axis=-1)
    y, wd_e = _lru(uz, ux, gate_x, lru_conv_w[0], lru_conv_b[0:1], wcat, bcat, lru_lambda[0],
                   cast=(moe_w_down[0].reshape(n_exp * d_exp, d),))
    wgu_e = wgu_e.reshape(n_exp, d, d_gu)
    wd_e = wd_e.reshape(n_exp, d_exp, d)

    wr_hi = moe_w_router[0].astype(BF16)
    wr_lo = (moe_w_router[0] - wr_hi.astype(F32)).astype(BF16)
    wr2 = jnp.pad(jnp.concatenate([wr_hi, wr_lo], axis=1), ((0, 0), (0, LANES - 2 * N_EXPERTS)))
    x2, h2t, ri, rt = _post(y, x1, mods[1], norm2_g[1:2], w_out, wr2)

    n_tok = bsz * seq
    flat_e = rt[0:2].reshape(-1).astype(jnp.int32)
    slot_tok, prev_dst, tile_e, n_used = _routing(flat_e, n_tok)
    y2 = _moe(tile_e, n_used, slot_tok, prev_dst, h2t, wgu_e, wd_e, n_tok)
    return _combine(x2, ri, mods[1], final_g[None, :], y2)
```

```python
import functools

import numpy as np
import jax
import jax.numpy as jnp
from jax import lax
from jax.experimental import pallas as pl
from jax.experimental.pallas import tpu as pltpu

F32 = jnp.float32
BF16 = jnp.bfloat16

GRID_W = 64
POOL_WINDOWS = (2, 4, 8, 16)
LRU_BLOCK = 128
LRU_C = 8.0
N_EXPERTS = 8
EPS = 1e-6
LOG2_E = 1.4426950408889634

LANES = 128
SUBLANES = 8
VMEM_LIMIT = 56 * 1024 * 1024

POOL_TILE = 256
TOK_TILE = 512
INPROJ_TILE = 1024
FF_CHUNK = 256
MOE_TILE = 1024
MOE_FCHUNK = 1792
MOE_SUBCHUNK = 256
DMA_UNROLL = 8
POST_TILE = 1024
POST_SUB = 256
COMBINE_TILE = 1024
LRU_SEGS = 64


def _dot(a, b):
    return jnp.dot(a, b, preferred_element_type=F32)


def _params(*sem):
    return pltpu.CompilerParams(dimension_semantics=sem, vmem_limit_bytes=VMEM_LIMIT)


def _resident(shape):
    return pl.BlockSpec(shape, lambda *_: (0,) * len(shape), pipeline_mode=pl.Buffered(1))


def _passenger_specs(cast, grid):
    steps = grid[0] * grid[1]
    n = max(k for k in range(1, steps + 1) if all(w.shape[0] % (2 * SUBLANES * k) == 0 for w in cast))
    chunk = lambda a, b: (jnp.minimum(a * grid[1] + b, n - 1), 0)
    specs = [pl.BlockSpec((w.shape[0] // n, w.shape[1]), chunk) for w in cast]
    return specs, [jax.ShapeDtypeStruct(w.shape, BF16) for w in cast]


def _run_passengers(srcs, dsts):
    for src, dst in zip(srcs, dsts):
        dst[...] = src[...].astype(dst.dtype)


def _norm_mod(x, g, shift, scale):
    ms = jnp.mean(x * x, axis=-1, keepdims=True)
    return (x * lax.rsqrt(ms + EPS)) * (g * (1.0 + scale)) + shift


def _split_bf16(v):
    hi = v.astype(BF16)
    lo = (v - hi.astype(F32)).astype(BF16)
    return hi, lo


def _ada_kernel(cc_ref, w_ref, b_ref, *refs):
    n_cast = len(refs) // 2
    o_ref = refs[n_cast]
    _run_passengers(refs[:n_cast], refs[n_cast + 1:])
    s = cc_ref[...]
    s = s * jax.nn.sigmoid(s)
    o_ref[0] = _dot(s.astype(BF16), w_ref[0].astype(BF16)) + b_ref[0]


def _adaln(cc, ada_w, ada_b, cast=()):
    n_layers, d, n = ada_w.shape
    r = cc.shape[0]
    tn = 1536
    grid = (n_layers, n // tn)
    cast_specs, cast_shapes = _passenger_specs(cast, grid)
    return pl.pallas_call(
        _ada_kernel,
        grid=grid,
        in_specs=[pl.BlockSpec((r, d), lambda l, j: (0, 0)),
                  pl.BlockSpec((1, d, tn), lambda l, j: (l, 0, j)),
                  pl.BlockSpec((1, 1, tn), lambda l, j: (l, 0, j))] + cast_specs,
        out_specs=[pl.BlockSpec((1, r, tn), lambda l, j: (l, 0, j))] + cast_specs,
        out_shape=[jax.ShapeDtypeStruct((n_layers, r, n), F32)] + cast_shapes,
        compiler_params=_params("arbitrary", "arbitrary"),
        name="adaln",
    )(cc, ada_w, ada_b.reshape(n_layers, 1, n), *cast)


def _pool_tables(seg_len, gc):
    t = np.arange(POOL_TILE)
    seg, pos = t // seg_len, t % seg_len
    band = np.zeros((len(POOL_WINDOWS), POOL_TILE, POOL_TILE), np.float32)
    invc = np.zeros((len(POOL_WINDOWS), POOL_TILE, gc), np.float32)
    for gi, win in enumerate(POOL_WINDOWS):
        lo = np.clip(pos - win // 2, 0, seg_len)
        hi = np.clip(pos - win // 2 + win, 0, seg_len)
        inside = (pos[None, :] >= lo[:, None]) & (pos[None, :] < hi[:, None])
        band[gi] = (seg[:, None] == seg[None, :]) & inside
        invc[gi] = (1.0 / (hi - lo))[:, None]
    return jnp.asarray(band, BF16), jnp.asarray(invc, F32)


def _layer0_kernel(x_ref, mod_ref, g1_ref, g2_ref, band_ref, invc_ref, pw_ref, ps_ref,
                   wgu_ref, wd_ref, *refs, d_ff):
    n_cast = (len(refs) - 2) // 2
    o_ref, act_ref = refs[n_cast], refs[-1]
    _run_passengers(refs[:n_cast], refs[n_cast + 1:-1])
    m = mod_ref[0]
    x = x_ref[0]
    tm, d = x.shape
    n_groups = band_ref.shape[0]
    gc = d // n_groups
    h = _norm_mod(x, g1_ref[...], m[0:1], m[1:2])
    rows = []
    for s in range(tm // POOL_TILE):
        hs = h[s * POOL_TILE:(s + 1) * POOL_TILE]
        cols = []
        for gi in range(n_groups):
            hg = hs[:, gi * gc:(gi + 1) * gc]
            hi, lo = _split_bf16(hg)
            band = band_ref[gi]
            pooled = (_dot(band, hi) + _dot(band, lo)) * invc_ref[gi] - hg
            cols.append(_dot(pooled.astype(BF16), pw_ref[gi]))
        rows.append(jnp.concatenate(cols, axis=1))
    y = rows[0] if len(rows) == 1 else jnp.concatenate(rows, axis=0)
    x1 = x + m[2:3] * (y * ps_ref[...])
    h2 = _norm_mod(x1, g2_ref[...], m[3:4], m[4:5]).astype(BF16)
    for c in range(d_ff // FF_CHUNK):
        g = _dot(h2, wgu_ref[:, c * FF_CHUNK:(c + 1) * FF_CHUNK])
        u = _dot(h2, wgu_ref[:, d_ff + c * FF_CHUNK:d_ff + (c + 1) * FF_CHUNK])
        act_ref[:, c * FF_CHUNK:(c + 1) * FF_CHUNK] = ((g * jax.nn.sigmoid(g)) * u).astype(BF16)
    o_ref[0] = x1 + m[5:6] * _dot(act_ref[...], wd_ref[...])


def _layer0(x, mods, mod_row, g1, g2, band, invc, pw, ps, wgu, wd, name, cast=()):
    bsz, t, d = x.shape
    d_ff = wd.shape[0]
    tm = min(TOK_TILE, t)
    grid = (bsz, t // tm)
    cast_specs, cast_shapes = _passenger_specs(cast, grid)
    return pl.pallas_call(
        functools.partial(_layer0_kernel, d_ff=d_ff),
        grid=grid,
        in_specs=[pl.BlockSpec((1, tm, d), lambda b, i: (b, i, 0)),
                  pl.BlockSpec((1, 6, d), lambda b, i: (mod_row(b), 0, 0)),
                  _resident((1, d)), _resident((1, d)),
                  _resident(band.shape), _resident(invc.shape), _resident(pw.shape), _resident((1, d)),
                  _resident(wgu.shape), _resident(wd.shape)] + cast_specs,
        out_specs=[pl.BlockSpec((1, tm, d), lambda b, i: (b, i, 0))] + cast_specs,
        out_shape=[jax.ShapeDtypeStruct(x.shape, F32)] + cast_shapes,
        scratch_shapes=[pltpu.VMEM((tm, d_ff), BF16)],
        compiler_params=_params("arbitrary", "arbitrary"),
        name=name,
    )(x, mods, g1, g2, band, invc, pw, ps, wgu, wd, *cast)


def _inproj_kernel(x_ref, mod_ref, g1_ref, w_ref, *out_refs, with_gate):
    m = mod_ref[0]
    h = _norm_mod(x_ref[0], g1_ref[...], m[0:1], m[1:2]).astype(BF16)
    y = _dot(h, w_ref[...])
    n = out_refs[0].shape[-1]
    if with_gate:
        out_refs[0][0] = jax.nn.gelu(y[:, :n])
        out_refs[1][0] = y[:, n:]
    else:
        out_refs[0][0] = y


def _inproj(x, mods, mod_row, g1, w, with_gate, name):
    bsz, t, d = x.shape
    tm = min(INPROJ_TILE, t)
    n_out = 2 if with_gate else 1
    n = w.shape[1] // n_out
    out_shape = jax.ShapeDtypeStruct((bsz, t, n), F32)
    out_spec = pl.BlockSpec((1, tm, n), lambda b, i: (b, i, 0))
    return pl.pallas_call(
        functools.partial(_inproj_kernel, with_gate=with_gate),
        grid=(bsz, t // tm),
        in_specs=[pl.BlockSpec((1, tm, d), lambda b, i: (b, i, 0)),
                  pl.BlockSpec((1, 6, d), lambda b, i: (mod_row(b), 0, 0)),
                  _resident((1, d)), _resident(w.shape)],
        out_specs=[out_spec] * n_out,
        out_shape=[out_shape] * n_out,
        compiler_params=_params("parallel", "parallel"),
        name=name,
    )(x, mods, g1, w)


def _scan8(a, b, row, reverse):
    for s in (1, 2, 4):
        if reverse:
            keep = row < SUBLANES - s
            shift = SUBLANES - s
        else:
            keep = row >= s
            shift = s
        a_sh = jnp.where(keep, pltpu.roll(a, shift, 0), 1.0)
        b_sh = jnp.where(keep, pltpu.roll(b, shift, 0), 0.0)
        b = a * b_sh + b
        a = a * a_sh
    return a, b


def _segment_scan(a_ref, b_ref, h_ref, row8, seg_len, reverse):
    nv = LRU_SEGS // SUBLANES

    def rows(k, j):
        jj = seg_len - 1 - j if reverse else j
        return pl.ds(k * SUBLANES * seg_len + jj, SUBLANES, stride=seg_len)

    zero = jnp.zeros((SUBLANES, LANES), F32)
    h = [zero] * nv
    p = [jnp.ones((SUBLANES, LANES), F32)] * nv
    for j in range(seg_len):
        for k in range(nv):
            a = a_ref[rows(k, j), :]
            h[k] = a * h[k] + b_ref[rows(k, j), :]
            p[k] = a * p[k]
    carry = zero
    start = [None] * nv
    for k in (range(nv - 1, -1, -1) if reverse else range(nv)):
        pk, hk = _scan8(p[k], h[k], row8, reverse)
        ends = hk + pk * carry
        if reverse:
            start[k] = jnp.where(row8 == SUBLANES - 1, carry, pltpu.roll(ends, SUBLANES - 1, 0))
            carry = jnp.broadcast_to(ends[0:1], ends.shape)
        else:
            start[k] = jnp.where(row8 == 0, carry, pltpu.roll(ends, 1, 0))
            carry = jnp.broadcast_to(ends[SUBLANES - 1:SUBLANES], ends.shape)
    h = start
    for j in range(seg_len):
        for k in range(nv):
            h[k] = a_ref[rows(k, j), :] * h[k] + b_ref[rows(k, j), :]
            h_ref[rows(k, j), :] = h[k]


def _lru_kernel(uz_ref, ux_ref, gate_ref, cw_ref, cb_ref, wc_ref, bc_ref, lam_ref, *refs, seg_len):
    n_cast = (len(refs) - 8) // 2
    cast_in, (o_ref, *cast_out) = refs[:n_cast], refs[n_cast:2 * n_cast + 1]
    af_ref, bf_ref, ab_ref, bb_ref, hf_ref, hb_ref, pad_ref = refs[2 * n_cast + 1:]
    _run_passengers(cast_in, cast_out)
    lz = uz_ref.shape[1]
    lx = ux_ref.shape[1]
    cw = cw_ref[...]
    cb = cb_ref[...]
    wc = wc_ref[0]
    bc = bc_ref[0]
    neg_lam = -lam_ref[...]
    softplus = jnp.maximum(neg_lam, 0.0) + jnp.log1p(jnp.exp(-jnp.abs(neg_lam)))
    kh = (-0.5 * LRU_C * LOG2_E) * softplus
    zeros8 = jnp.zeros((SUBLANES, LANES), F32)

    def prep(src_ref, off_f, off_b):
        n = src_ref.shape[1]
        pad_ref[pl.ds(0, SUBLANES), :] = zeros8
        pad_ref[pl.ds(SUBLANES, n), :] = src_ref[0]
        pad_ref[pl.ds(SUBLANES + n, SUBLANES), :] = zeros8
        u = cb + cw[2:3] * src_ref[0]
        for k in (0, 1, 3):
            u = u + cw[k:k + 1] * pad_ref[pl.ds(SUBLANES + k - 2, n), :]
        t = jnp.tanh(_dot(u.astype(BF16), wc) + bc)
        hu = 0.5 * u
        for d, (a_ref, b_ref, off) in enumerate(((af_ref, bf_ref, off_f), (ab_ref, bb_ref, off_b))):
            t_r = t[:, 2 * d * LRU_BLOCK:(2 * d + 1) * LRU_BLOCK]
            t_i = t[:, (2 * d + 1) * LRU_BLOCK:(2 * d + 2) * LRU_BLOCK]
            a = jnp.exp2(kh[d:d + 1] * t_r + kh[d:d + 1])
            y = 1.0 - a * a
            root = jnp.where(y > 0.0, y * lax.rsqrt(y), 0.0)
            a_ref[pl.ds(off, n), :] = a
            b_ref[pl.ds(off, n), :] = root * (hu * t_i + hu)

    prep(uz_ref, 0, lx)
    prep(ux_ref, lz, 0)
    row8 = lax.broadcasted_iota(jnp.int32, (SUBLANES, LANES), 0)
    _segment_scan(af_ref, bf_ref, hf_ref, row8, seg_len, False)
    _segment_scan(ab_ref, bb_ref, hb_ref, row8, seg_len, True)
    hsum = hf_ref[pl.ds(lz, lx), :] + hb_ref[pl.ds(0, lx), :]
    o_ref[0] = (hsum * gate_ref[0]).astype(BF16)


def _lru(uz, ux, gate, conv_w, conv_b, wcat, bcat, lam, cast=()):
    bsz, lz, d_rnn = uz.shape
    lx = ux.shape[1]
    heads = d_rnn // LRU_BLOCK
    cast_specs, cast_shapes = _passenger_specs(cast, (bsz, heads))
    seg_len = (lz + lx) // LRU_SEGS
    assert seg_len * LRU_SEGS == lz + lx and seg_len % SUBLANES == SUBLANES // 2
    seq = lambda n: pl.BlockSpec((1, n, LRU_BLOCK), lambda b, h: (b, 0, h))
    scratch = pltpu.VMEM((lz + lx, LRU_BLOCK), F32)
    return pl.pallas_call(
        functools.partial(_lru_kernel, seg_len=seg_len),
        grid=(bsz, heads),
        in_specs=[seq(lz), seq(lx), seq(lx),
                  pl.BlockSpec((conv_w.shape[0], LRU_BLOCK), lambda b, h: (0, h)),
                  pl.BlockSpec((1, LRU_BLOCK), lambda b, h: (0, h)),
                  pl.BlockSpec((1, LRU_BLOCK, 4 * LRU_BLOCK), lambda b, h: (h, 0, 0)),
                  pl.BlockSpec((1, 1, 4 * LRU_BLOCK), lambda b, h: (h, 0, 0)),
                  pl.BlockSpec((2, LRU_BLOCK), lambda b, h: (0, h))] + cast_specs,
        out_specs=[seq(lx)] + cast_specs,
        out_shape=[jax.ShapeDtypeStruct((bsz, lx, d_rnn), BF16)] + cast_shapes,
        scratch_shapes=[scratch] * 6 + [pltpu.VMEM((lx + 2 * SUBLANES, LRU_BLOCK), F32)],
        compiler_params=_params("arbitrary", "arbitrary"),
        name="rglru",
    )(uz, ux, gate, conv_w, conv_b, wcat, bcat, lam, *cast)


def _store_row_tiles(ref, v):
    tm, d = v.shape
    for c in range(d // LANES):
        ref[pl.ds(c, tm, stride=d // LANES), :] = v[:, c * LANES:(c + 1) * LANES]


def _load_row_tiles(ref, tm, d):
    return [ref[pl.ds(c, tm, stride=d // LANES), :] for c in range(d // LANES)]


def _post_kernel(y_ref, x_ref, mod_ref, g2_ref, wo_ref, wr2_ref, x2_ref, h2t_ref, ri_ref, rt_ref):
    m = mod_ref[0]
    tm = x_ref.shape[1]
    rt = x_ref.shape[2] // LANES
    for q in range(tm // POST_SUB):
        rows = pl.ds(q * POST_SUB, POST_SUB)
        x2 = x_ref[0, rows, :] + m[2:3] * _dot(y_ref[0, rows, :], wo_ref[...])
        x2_ref[0, rows, :] = x2
        h2 = _norm_mod(x2, g2_ref[...], m[3:4], m[4:5])
        for c in range(rt):
            h2t_ref[pl.ds(q * POST_SUB * rt + c, POST_SUB, stride=rt), :] = h2[:, c * LANES:(c + 1) * LANES]
        hi, lo = _split_bf16(h2)
        both = _dot(hi, wr2_ref[...]) + _dot(lo, wr2_ref[...])
        logits = both + pltpu.roll(both, LANES - N_EXPERTS, 1)
        lane = lax.broadcasted_iota(jnp.int32, logits.shape, 1)
        logits = jnp.where(lane < N_EXPERTS, logits, -1e30)
        e = jnp.exp(logits - jnp.max(logits, axis=-1, keepdims=True))
        p = e / jnp.sum(e, axis=-1, keepdims=True)
        p1 = jnp.max(p, axis=-1, keepdims=True)
        i1 = jnp.min(jnp.where(p == p1, lane, LANES), axis=-1, keepdims=True)
        rest = jnp.where(lane == i1, -1.0, p)
        p2 = jnp.max(rest, axis=-1, keepdims=True)
        i2 = jnp.min(jnp.where(rest == p2, lane, LANES), axis=-1, keepdims=True)
        den = p1 + p2
        ri = jnp.where(lane == 0, i1.astype(F32),
                       jnp.where(lane == 1, i2.astype(F32),
                                 jnp.where(lane == 2, p1 / den,
                                           jnp.where(lane == 3, p2 / den, 0.0))))
        ri_ref[rows, :] = ri
        rt_ref[:, rows] = jnp.transpose(ri)[0:SUBLANES, :]


def _post(y, x, mods, g2, wo, wr2):
    bsz, t, d = x.shape
    d_rnn = y.shape[-1]
    tm = min(POST_TILE, t)
    nt = t // tm
    rt = d // LANES
    return pl.pallas_call(
        _post_kernel,
        grid=(bsz, nt),
        in_specs=[pl.BlockSpec((1, tm, d_rnn), lambda b, i: (b, i, 0)),
                  pl.BlockSpec((1, tm, d), lambda b, i: (b, i, 0)),
                  pl.BlockSpec((1, 6, d), lambda b, i: (b, 0, 0)),
                  _resident((1, d)), _resident(wo.shape), _resident(wr2.shape)],
        out_specs=[pl.BlockSpec((1, tm, d), lambda b, i: (b, i, 0)),
                   pl.BlockSpec((tm * rt, LANES), lambda b, i: (b * nt + i, 0)),
                   pl.BlockSpec((tm, LANES), lambda b, i: (b * nt + i, 0)),
                   pl.BlockSpec((SUBLANES, tm), lambda b, i: (0, b * nt + i))],
        out_shape=[jax.ShapeDtypeStruct(x.shape, F32),
                   jax.ShapeDtypeStruct((bsz * t * rt, LANES), F32),
                   jax.ShapeDtypeStruct((bsz * t, LANES), F32),
                   jax.ShapeDtypeStruct((SUBLANES, bsz * t), F32)],
        compiler_params=_params("parallel", "parallel"),
        name="outproj_router",
    )(y, x, mods, g2, wo, wr2)


def _moe_kernel(te_ref, nu_ref, first_ref, nxt_ref, prev_ref, h2t_ref, wg_ref, wu_ref, wd_ref, y2_ref,
                xs_buf, xb_ref, act_ref, acc_ref, out_buf, gsem, ssem, *, n_steps):
    i = pl.program_id(0)
    j = pl.program_id(1)
    tm, d = xb_ref.shape
    tf = act_ref.shape[1]
    rt = d // LANES
    nu = nu_ref[0]
    used = i < nu
    slot = i % 2

    def gather_row(idx_ref, buf_slot, r, priority=0):
        src_row = pl.multiple_of(idx_ref[r] * rt, rt)
        dst_row = pl.multiple_of(r * rt, rt)
        pltpu.make_async_copy(h2t_ref.at[pl.ds(src_row, rt), :],
                              xs_buf.at[buf_slot, pl.ds(dst_row, rt), :],
                              gsem.at[buf_slot]).start(priority=priority)

    def scatter_row(r, priority=0):
        src_row = pl.multiple_of(r * rt, rt)
        dst_row = pl.multiple_of(prev_ref[r] * rt, rt)
        pltpu.make_async_copy(out_buf.at[1 - slot, pl.ds(src_row, rt), :],
                              y2_ref.at[pl.ds(dst_row, rt), :], ssem).start(priority=priority)

    def wait_scatter():
        pltpu.make_async_copy(out_buf.at[0], out_buf.at[0], ssem).wait()

    @pl.when(jnp.logical_and(i == 0, j == 0))
    def _():
        out_buf[...] = jnp.zeros(out_buf.shape, out_buf.dtype)

        def issue(r, carry):
            gather_row(first_ref, 0, r)
            return carry
        lax.fori_loop(0, tm, issue, 0, unroll=DMA_UNROLL)

    @pl.when(jnp.logical_and(j == 0, i <= nu))
    def _():
        for s in range(2):
            @pl.when(slot == s)
            def _(s=s):
                pltpu.make_async_copy(xs_buf.at[s], xs_buf.at[s], gsem.at[s]).wait()

                @pl.when(used)
                def _():
                    for c in range(rt):
                        xb_ref[:, c * LANES:(c + 1) * LANES] = (
                            xs_buf[s, pl.ds(c, tm, stride=rt), :].astype(BF16))

        @pl.when(i == nu)
        def _():
            def issue(r, carry):
                scatter_row(r)
                return carry
            lax.fori_loop(0, tm, issue, 0, unroll=DMA_UNROLL)
            wait_scatter()

    def compute(step):
        xb = xb_ref[...]
        n_sub = tf // MOE_SUBCHUNK
        per = -(-tm // (2 * n_sub))

        def issue_rows(k):
            if step == 0:
                for r in range(k * per, min((k + 1) * per, tm)):
                    gather_row(nxt_ref, 1 - slot, r, priority=r % 2)
                    scatter_row(r, priority=(r + 1) % 2)

        for c in range(n_sub):
            cs = slice(c * MOE_SUBCHUNK, (c + 1) * MOE_SUBCHUNK)
            g = _dot(xb, wg_ref[0, :, cs])
            issue_rows(2 * c)
            u = _dot(xb, wu_ref[0, :, cs])
            issue_rows(2 * c + 1)
            act_ref[:, cs] = ((g * jax.nn.sigmoid(g)) * u).astype(BF16)
        part = _dot(act_ref[...], wd_ref[0])
        if step < n_steps - 1:
            acc_ref[...] = part if step == 0 else acc_ref[...] + part
        else:
            total = part if step == 0 else acc_ref[...] + part
            for c in range(rt):
                out_buf[slot, pl.ds(c, tm, stride=rt), :] = total[:, c * LANES:(c + 1) * LANES]
            wait_scatter()

    for step in range(n_steps):
        @pl.when(jnp.logical_and(used, j == step))
        def _(step=step):
            compute(step)


def _moe(tile_e, n_used, slot_tok, prev_dst, h2t, wgu, wd, n_tok):
    d = wgu.shape[1]
    d_e = wd.shape[1]
    rt = d // LANES
    tm = MOE_TILE
    n_tiles = slot_tok.shape[0] // tm
    nj = d_e // MOE_FCHUNK

    def jf(i, j, nu):
        return jnp.where(i < nu[0], j, nj - 1)

    smem_tile = lambda f: pl.BlockSpec((tm,), f, memory_space=pltpu.SMEM)
    return pl.pallas_call(
        functools.partial(_moe_kernel, n_steps=nj),
        grid_spec=pltpu.PrefetchScalarGridSpec(
            num_scalar_prefetch=2,
            grid=(n_tiles, nj),
            in_specs=[smem_tile(lambda i, j, te, nu: (0,)),
                      smem_tile(lambda i, j, te, nu: (jnp.minimum(i + 1, n_tiles - 1),)),
                      smem_tile(lambda i, j, te, nu: (i,)),
                      pl.BlockSpec(memory_space=pl.ANY),
                      pl.BlockSpec((1, d, MOE_FCHUNK), lambda i, j, te, nu: (te[i], 0, jf(i, j, nu))),
                      pl.BlockSpec((1, d, MOE_FCHUNK), lambda i, j, te, nu: (te[i], 0, nj + jf(i, j, nu))),
                      pl.BlockSpec((1, MOE_FCHUNK, d), lambda i, j, te, nu: (te[i], jf(i, j, nu), 0))],
            out_specs=pl.BlockSpec(memory_space=pl.ANY),
            scratch_shapes=[pltpu.VMEM((2, tm * rt, LANES), F32),
                            pltpu.VMEM((tm, d), BF16),
                            pltpu.VMEM((tm, MOE_FCHUNK), BF16),
                            pltpu.VMEM((tm, d), F32),
                            pltpu.VMEM((2, tm * rt, LANES), F32),
                            pltpu.SemaphoreType.DMA((2,)),
                            pltpu.SemaphoreType.DMA]),
        out_shape=jax.ShapeDtypeStruct(((2 * n_tok + tm) * rt, LANES), F32),
        compiler_params=_params("arbitrary", "arbitrary"),
        name="moe_experts",
    )(tile_e, n_used, slot_tok, slot_tok, prev_dst, h2t, wgu, wgu, wd)


def _combine_kernel(x_ref, ri_ref, mod_ref, fg_ref, ya_ref, yb_ref, o_ref):
    tm, d = x_ref.shape[1], x_ref.shape[2]
    ri = ri_ref[...]
    w1 = ri[:, 2:3]
    w2 = ri[:, 3:4]
    ya = _load_row_tiles(ya_ref, tm, d)
    yb = _load_row_tiles(yb_ref, tm, d)
    moe = jnp.concatenate([w1 * a + w2 * b for a, b in zip(ya, yb)], axis=1)
    x3 = x_ref[0] + mod_ref[0][5:6] * moe
    ms = jnp.mean(x3 * x3, axis=-1, keepdims=True)
    o_ref[0] = (x3 * lax.rsqrt(ms + EPS)) * fg_ref[...]


def _combine(x2, ri, mods, fg, y2):
    bsz, t, d = x2.shape
    tm = COMBINE_TILE
    nt = t // tm
    rt = d // LANES
    n_blocks = bsz * nt
    return pl.pallas_call(
        _combine_kernel,
        grid=(bsz, nt),
        in_specs=[pl.BlockSpec((1, tm, d), lambda b, i: (b, i, 0)),
                  pl.BlockSpec((tm, LANES), lambda b, i: (b * nt + i, 0)),
                  pl.BlockSpec((1, 6, d), lambda b, i: (b, 0, 0)),
                  pl.BlockSpec((1, d), lambda b, i: (0, 0)),
                  pl.BlockSpec((tm * rt, LANES), lambda b, i: (b * nt + i, 0)),
                  pl.BlockSpec((tm * rt, LANES), lambda b, i: (n_blocks + b * nt + i, 0))],
        out_specs=pl.BlockSpec((1, tm, d), lambda b, i: (b, i, 0)),
        out_shape=jax.ShapeDtypeStruct(x2.shape, F32),
        compiler_params=_params("parallel", "parallel"),
        name="moe_combine_norm",
    )(x2, ri, mods, fg, y2, y2)


def _routing(flat_e, n_tok):
    n2 = flat_e.shape[0]
    experts = jnp.arange(N_EXPERTS, dtype=jnp.int32)
    counts = jnp.sum((flat_e[:, None] == experts[None, :]).astype(jnp.int32), axis=0)
    padded = (counts + MOE_TILE - 1) // MOE_TILE * MOE_TILE
    pends = jnp.cumsum(padded)
    n_tiles = n2 // MOE_TILE + N_EXPERTS + 1
    idx_bits = max(n2, MOE_TILE).bit_length()
    r = jnp.arange(MOE_TILE, dtype=jnp.int32)
    pad_e = jnp.where(r[None, :] < (padded - counts)[:, None], experts[:, None], N_EXPERTS)
    pad_e = jnp.concatenate([pad_e.reshape(-1), jnp.full((MOE_TILE,), N_EXPERTS, jnp.int32)])
    keys = jnp.concatenate([
        (flat_e << (idx_bits + 1)) | jnp.arange(n2, dtype=jnp.int32),
        (pad_e << (idx_bits + 1)) | (1 << idx_bits) | jnp.tile(r, N_EXPERTS + 1)])
    keys = jnp.sort(keys)
    valid = ((keys >> idx_bits) & 1) == 0
    f = keys & ((1 << idx_bits) - 1)
    slot = jnp.arange(n_tiles * MOE_TILE, dtype=jnp.int32)
    slot_tok = jnp.where(valid, f % n_tok, 0).astype(jnp.int32)
    slot_dst = jnp.where(valid, f, n2 + slot % MOE_TILE).astype(jnp.int32)
    prev_dst = jnp.concatenate([n2 + jnp.arange(MOE_TILE, dtype=jnp.int32), slot_dst[:-MOE_TILE]])
    tile0 = jnp.arange(n_tiles, dtype=jnp.int32) * MOE_TILE
    tile_e = jnp.minimum(jnp.sum((tile0[:, None] >= pends[None, :]).astype(jnp.int32), axis=1), N_EXPERTS - 1)
    n_used = (pends[-1] // MOE_TILE).astype(jnp.int32).reshape(1)
    return slot_tok, prev_dst, tile_e.astype(jnp.int32), n_used


def kernel(x, c, ctx, c_ctx, ada_w, ada_b, norm1_g, norm2_g, pool_w, pool_scale, ffn_w_gu, ffn_w_down,
           lru_w_in, lru_conv_w, lru_conv_b, lru_w_r, lru_b_r, lru_w_i, lru_b_i, lru_lambda, lru_w_out,
           moe_w_router, moe_w_gu, moe_w_down, final_g):
    bsz, seq, d = x.shape
    assert ada_w.shape[0] == 2 and seq % TOK_TILE == 0 and TOK_TILE % POOL_TILE == 0 and ctx.shape[1] == POOL_TILE
    assert d == SUBLANES * LANES
    d_rnn = lru_w_out.shape[1]
    heads = d_rnn // LRU_BLOCK
    gc = d // len(POOL_WINDOWS)
    z_row = bsz

    n_rows = (bsz + 1 + SUBLANES - 1) // SUBLANES * SUBLANES
    cc = jnp.concatenate([c, c_ctx[None, :], jnp.zeros((n_rows - bsz - 1, d), F32)], axis=0)
    n_grp = pool_w.shape[1]
    mods, wgu0, wd0, pw0 = _adaln(cc, ada_w, ada_b, cast=(ffn_w_gu[0], ffn_w_down[0], pool_w[0].reshape(n_grp * gc, gc)))
    mods = mods.reshape(2, n_rows, 6, d)
    x_row = lambda b: b
    ctx_row = lambda b: z_row

    band_x, invc_x = _pool_tables(GRID_W, gc)
    band_z, invc_z = _pool_tables(POOL_TILE, gc)
    l0 = (norm1_g[0:1], norm2_g[0:1])
    l0w = (pw0.reshape(n_grp, gc, gc), pool_scale[0:1], wgu0, wd0)
    n_exp, _, d_gu = moe_w_gu.shape[1:]
    d_exp = moe_w_down.shape[2]
    x1, wgu_e = _layer0(x, mods[0], x_row, *l0, band_x, invc_x, *l0w, name="layer0_latent",
                        cast=(moe_w_gu[0].reshape(n_exp * d, d_gu),))
    z1, w_in, w_out = _layer0(ctx, mods[0], ctx_row, *l0, band_z, invc_z, *l0w, name="layer0_context",
                              cast=(lru_w_in[0], lru_w_out[0]))

    gate_x, ux = _inproj(x1, mods[1], x_row, norm1_g[1:2], w_in, True, name="lru_inproj_latent")
    (uz,) = _inproj(z1, mods[1], ctx_row, norm1_g[1:2], w_in[:, d_rnn:], False, name="lru_inproj_context")
    wcat = (0.5 * jnp.concatenate([lru_w_r[0, 0], lru_w_i[0, 0], lru_w_r[0, 1], lru_w_i[0, 1]], axis=-1)).astype(BF16)
    bcat = 0.5 * jnp.concatenate([v.reshape(heads, 1, LRU_BLOCK)
                                  for v in (lru_b_r[0, 0], lru_b_i[0, 0], lru_b_r[0, 1], lru_b_i[0, 1])], axis=-1)
    y, wd_e = _lru(uz, ux, gate_x, lru_conv_w[0], lru_conv_b[0:1], wcat, bcat, lru_lambda[0],
                   cast=(moe_w_down[0].reshape(n_exp * d_exp, d),))
    wgu_e = wgu_e.reshape(n_exp, d, d_gu)
    wd_e = wd_e.reshape(n_exp, d_exp, d)

    wr_hi = moe_w_router[0].astype(BF16)
    wr_lo = (moe_w_router[0] - wr_hi.astype(F32)).astype(BF16)
    wr2 = jnp.pad(jnp.concatenate([wr_hi, wr_lo], axis=1), ((0, 0), (0, LANES - 2 * N_EXPERTS)))
    x2, h2t, ri, rt = _post(y, x1, mods[1], norm2_g[1:2], w_out, wr2)

    n_tok = bsz * seq
    flat_e = rt[0:2].reshape(-1).astype(jnp.int32)
    slot_tok, prev_dst, tile_e, n_used = _routing(flat_e, n_tok)
    y2 = _moe(tile_e, n_used, slot_tok, prev_dst, h2t, wgu_e, wd_e, n_tok)
    return _combine(x2, ri, mods[1], final_g[None, :], y2)
```

```python
import functools

import numpy as np
import jax
import jax.numpy as jnp
from jax import lax
from jax.experimental import pallas as pl
from jax.experimental.pallas import tpu as pltpu

F32 = jnp.float32
BF16 = jnp.bfloat16

GRID_W = 64
POOL_WINDOWS = (2, 4, 8, 16)
LRU_BLOCK = 128
LRU_C = 8.0
N_EXPERTS = 8
EPS = 1e-6
LOG2_E = 1.4426950408889634

LANES = 128
SUBLANES = 8
VMEM_LIMIT = 56 * 1024 * 1024

POOL_TILE = 256
TOK_TILE = 512
INPROJ_TILE = 1024
FF_CHUNK = 256
MOE_TILE = 1024
MOE_FCHUNK = 1792
MOE_SUBCHUNK = 256
DMA_UNROLL = 8
POST_TILE = 1024
POST_SUB = 256
COMBINE_TILE = 1024
LRU_SEGS = 64


def _dot(a, b):
    return jnp.dot(a, b, preferred_element_type=F32)


def _params(*sem):
    return pltpu.CompilerParams(dimension_semantics=sem, vmem_limit_bytes=VMEM_LIMIT)


def _resident(shape):
    return pl.BlockSpec(shape, lambda *_: (0,) * len(shape), pipeline_mode=pl.Buffered(1))


def _passenger_specs(cast, grid):
    steps = grid[0] * grid[1]
    n = max(k for k in range(1, steps + 1) if all(w.shape[0] % (2 * SUBLANES * k) == 0 for w in cast))
    chunk = lambda a, b: (jnp.minimum(a * grid[1] + b, n - 1), 0)
    specs = [pl.BlockSpec((w.shape[0] // n, w.shape[1]), chunk) for w in cast]
    return specs, [jax.ShapeDtypeStruct(w.shape, BF16) for w in cast]


def _run_passengers(srcs, dsts):
    for src, dst in zip(srcs, dsts):
        dst[...] = src[...].astype(dst.dtype)


def _norm_mod(x, g, shift, scale):
    ms = jnp.mean(x * x, axis=-1, keepdims=True)
    return (x * lax.rsqrt(ms + EPS)) * (g * (1.0 + scale)) + shift


def _split_bf16(v):
    hi = v.astype(BF16)
    lo = (v - hi.astype(F32)).astype(BF16)
    return hi, lo


def _ada_kernel(cc_ref, w_ref, b_ref, *refs):
    n_cast = len(refs) // 2
    o_ref = refs[n_cast]
    _run_passengers(refs[:n_cast], refs[n_cast + 1:])
    s = cc_ref[...]
    s = s * jax.nn.sigmoid(s)
    o_ref[0] = _dot(s.astype(BF16), w_ref[0].astype(BF16)) + b_ref[0]


def _adaln(cc, ada_w, ada_b, cast=()):
    n_layers, d, n = ada_w.shape
    r = cc.shape[0]
    tn = 1536
    grid = (n_layers, n // tn)
    cast_specs, cast_shapes = _passenger_specs(cast, grid)
    return pl.pallas_call(
        _ada_kernel,
        grid=grid,
        in_specs=[pl.BlockSpec((r, d), lambda l, j: (0, 0)),
                  pl.BlockSpec((1, d, tn), lambda l, j: (l, 0, j)),
                  pl.BlockSpec((1, 1, tn), lambda l, j: (l, 0, j))] + cast_specs,
        out_specs=[pl.BlockSpec((1, r, tn), lambda l, j: (l, 0, j))] + cast_specs,
        out_shape=[jax.ShapeDtypeStruct((n_layers, r, n), F32)] + cast_shapes,
        compiler_params=_params("arbitrary", "arbitrary"),
        name="adaln",
    )(cc, ada_w, ada_b.reshape(n_layers, 1, n), *cast)


def _pool_tables(seg_len, gc):
    t = np.arange(POOL_TILE)
    seg, pos = t // seg_len, t % seg_len
    band = np.zeros((len(POOL_WINDOWS), POOL_TILE, POOL_TILE), np.float32)
    invc = np.zeros((len(POOL_WINDOWS), POOL_TILE, gc), np.float32)
    for gi, win in enumerate(POOL_WINDOWS):
        lo = np.clip(pos - win // 2, 0, seg_len)
        hi = np.clip(pos - win // 2 + win, 0, seg_len)
        inside = (pos[None, :] >= lo[:, None]) & (pos[None, :] < hi[:, None])
        band[gi] = (seg[:, None] == seg[None, :]) & inside
        invc[gi] = (1.0 / (hi - lo))[:, None]
    return jnp.asarray(band, BF16), jnp.asarray(invc, F32)


def _layer0_kernel(x_ref, mod_ref, g1_ref, g2_ref, band_ref, invc_ref, pw_ref, ps_ref,
                   wgu_ref, wd_ref, *refs, d_ff):
    n_cast = (len(refs) - 2) // 2
    o_ref, act_ref = refs[n_cast], refs[-1]
    _run_passengers(refs[:n_cast], refs[n_cast + 1:-1])
    m = mod_ref[0]
    x = x_ref[0]
    tm, d = x.shape
    n_groups = band_ref.shape[0]
    gc = d // n_groups
    h = _norm_mod(x, g1_ref[...], m[0:1], m[1:2])
    rows = []
    for s in range(tm // POOL_TILE):
        hs = h[s * POOL_TILE:(s + 1) * POOL_TILE]
        cols = []
        for gi in range(n_groups):
            hg = hs[:, gi * gc:(gi + 1) * gc]
            hi, lo = _split_bf16(hg)
            band = band_ref[gi]
            pooled = (_dot(band, hi) + _dot(band, lo)) * invc_ref[gi] - hg
            cols.append(_dot(pooled.astype(BF16), pw_ref[gi]))
        rows.append(jnp.concatenate(cols, axis=1))
    y = rows[0] if len(rows) == 1 else jnp.concatenate(rows, axis=0)
    x1 = x + m[2:3] * (y * ps_ref[...])
    h2 = _norm_mod(x1, g2_ref[...], m[3:4], m[4:5]).astype(BF16)
    for c in range(d_ff // FF_CHUNK):
        g = _dot(h2, wgu_ref[:, c * FF_CHUNK:(c + 1) * FF_CHUNK])
        u = _dot(h2, wgu_ref[:, d_ff + c * FF_CHUNK:d_ff + (c + 1) * FF_CHUNK])
        act_ref[:, c * FF_CHUNK:(c + 1) * FF_CHUNK] = ((g * jax.nn.sigmoid(g)) * u).astype(BF16)
    o_ref[0] = x1 + m[5:6] * _dot(act_ref[...], wd_ref[...])


def _layer0(x, mods, mod_row, g1, g2, band, invc, pw, ps, wgu, wd, name, cast=()):
    bsz, t, d = x.shape
    d_ff = wd.shape[0]
    tm = min(TOK_TILE, t)
    grid = (bsz, t // tm)
    cast_specs, cast_shapes = _passenger_specs(cast, grid)
    return pl.pallas_call(
        functools.partial(_layer0_kernel, d_ff=d_ff),
        grid=grid,
        in_specs=[pl.BlockSpec((1, tm, d), lambda b, i: (b, i, 0)),
                  pl.BlockSpec((1, 6, d), lambda b, i: (mod_row(b), 0, 0)),
                  _resident((1, d)), _resident((1, d)),
                  _resident(band.shape), _resident(invc.shape), _resident(pw.shape), _resident((1, d)),
                  _resident(wgu.shape), _resident(wd.shape)] + cast_specs,
        out_specs=[pl.BlockSpec((1, tm, d), lambda b, i: (b, i, 0))] + cast_specs,
        out_shape=[jax.ShapeDtypeStruct(x.shape, F32)] + cast_shapes,
        scratch_shapes=[pltpu.VMEM((tm, d_ff), BF16)],
        compiler_params=_params("arbitrary", "arbitrary"),
        name=name,
    )(x, mods, g1, g2, band, invc, pw, ps, wgu, wd, *cast)


def _inproj_kernel(x_ref, mod_ref, g1_ref, w_ref, *out_refs, with_gate):
    m = mod_ref[0]
    h = _norm_mod(x_ref[0], g1_ref[...], m[0:1], m[1:2]).astype(BF16)
    y = _dot(h, w_ref[...])
    n = out_refs[0].shape[-1]
    if with_gate:
        out_refs[0][0] = jax.nn.gelu(y[:, :n])
        out_refs[1][0] = y[:, n:]
    else:
        out_refs[0][0] = y


def _inproj(x, mods, mod_row, g1, w, with_gate, name):
    bsz, t, d = x.shape
    tm = min(INPROJ_TILE, t)
    n_out = 2 if with_gate else 1
    n = w.shape[1] // n_out
    out_shape = jax.ShapeDtypeStruct((bsz, t, n), F32)
    out_spec = pl.BlockSpec((1, tm, n), lambda b, i: (b, i, 0))
    return pl.pallas_call(
        functools.partial(_inproj_kernel, with_gate=with_gate),
        grid=(bsz, t // tm),
        in_specs=[pl.BlockSpec((1, tm, d), lambda b, i: (b, i, 0)),
                  pl.BlockSpec((1, 6, d), lambda b, i: (mod_row(b), 0, 0)),
                  _resident((1, d)), _resident(w.shape)],
        out_specs=[out_spec] * n_out,
        out_shape=[out_shape] * n_out,
        compiler_params=_params("parallel", "parallel"),
        name=name,
    )(x, mods, g1, w)


def _scan8(a, b, row, reverse):
    for s in (1, 2, 4):
        if reverse:
            keep = row < SUBLANES - s
            shift = SUBLANES - s
        else:
            keep = row >= s
            shift = s
        a_sh = jnp.where(keep, pltpu.roll(a, shift, 0), 1.0)
        b_sh = jnp.where(keep, pltpu.roll(b, shift, 0), 0.0)
        b = a * b_sh + b
        a = a * a_sh
    return a, b


def _segment_scan(a_ref, b_ref, h_ref, row8, seg_len, reverse):
    nv = LRU_SEGS // SUBLANES

    def rows(k, j):
        jj = seg_len - 1 - j if reverse else j
        return pl.ds(k * SUBLANES * seg_len + jj, SUBLANES, stride=seg_len)

    zero = jnp.zeros((SUBLANES, LANES), F32)
    h = [zero] * nv
    p = [jnp.ones((SUBLANES, LANES), F32)] * nv
    for j in range(seg_len):
        for k in range(nv):
            a = a_ref[rows(k, j), :]
            h[k] = a * h[k] + b_ref[rows(k, j), :]
            p[k] = a * p[k]
    carry = zero
    start = [None] * nv
    for k in (range(nv - 1, -1, -1) if reverse else range(nv)):
        pk, hk = _scan8(p[k], h[k], row8, reverse)
        ends = hk + pk * carry
        if reverse:
            start[k] = jnp.where(row8 == SUBLANES - 1, carry, pltpu.roll(ends, SUBLANES - 1, 0))
            carry = jnp.broadcast_to(ends[0:1], ends.shape)
        else:
            start[k] = jnp.where(row8 == 0, carry, pltpu.roll(ends, 1, 0))
            carry = jnp.broadcast_to(ends[SUBLANES - 1:SUBLANES], ends.shape)
    h = start
    for j in range(seg_len):
        for k in range(nv):
            h[k] = a_ref[rows(k, j), :] * h[k] + b_ref[rows(k, j), :]
            h_ref[rows(k, j), :] = h[k]


def _lru_kernel(uz_ref, ux_ref, gate_ref, cw_ref, cb_ref, wc_ref, bc_ref, lam_ref, *refs, seg_len):
    n_cast = (len(refs) - 8) // 2
    cast_in, (o_ref, *cast_out) = refs[:n_cast], refs[n_cast:2 * n_cast + 1]
    af_ref, bf_ref, ab_ref, bb_ref, hf_ref, hb_ref, pad_ref = refs[2 * n_cast + 1:]
    _run_passengers(cast_in, cast_out)
    lz = uz_ref.shape[1]
    lx = ux_ref.shape[1]
    cw = cw_ref[...]
    cb = cb_ref[...]
    wc = wc_ref[0]
    bc = bc_ref[0]
    neg_lam = -lam_ref[...]
    softplus = jnp.maximum(neg_lam, 0.0) + jnp.log1p(jnp.exp(-jnp.abs(neg_lam)))
    kh = (-0.5 * LRU_C * LOG2_E) * softplus
    zeros8 = jnp.zeros((SUBLANES, LANES), F32)

    def prep(src_ref, off_f, off_b):
        n = src_ref.shape[1]
        pad_ref[pl.ds(0, SUBLANES), :] = zeros8
        pad_ref[pl.ds(SUBLANES, n), :] = src_ref[0]
        pad_ref[pl.ds(SUBLANES + n, SUBLANES), :] = zeros8
        u = cb + cw[2:3] * src_ref[0]
        for k in (0, 1, 3):
            u = u + cw[k:k + 1] * pad_ref[pl.ds(SUBLANES + k - 2, n), :]
        t = jnp.tanh(_dot(u.astype(BF16), wc) + bc)
        hu = 0.5 * u
        for d, (a_ref, b_ref, off) in enumerate(((af_ref, bf_ref, off_f), (ab_ref, bb_ref, off_b))):
            t_r = t[:, 2 * d * LRU_BLOCK:(2 * d + 1) * LRU_BLOCK]
            t_i = t[:, (2 * d + 1) * LRU_BLOCK:(2 * d + 2) * LRU_BLOCK]
            a = jnp.exp2(kh[d:d + 1] * t_r + kh[d:d + 1])
            y = 1.0 - a * a
            root = jnp.where(y > 0.0, y * lax.rsqrt(y), 0.0)
            a_ref[pl.ds(off, n), :] = a
            b_ref[pl.ds(off, n), :] = root * (hu * t_i + hu)

    prep(uz_ref, 0, lx)
    prep(ux_ref, lz, 0)
    row8 = lax.broadcasted_iota(jnp.int32, (SUBLANES, LANES), 0)
    _segment_scan(af_ref, bf_ref, hf_ref, row8, seg_len, False)
    _segment_scan(ab_ref, bb_ref, hb_ref, row8, seg_len, True)
    hsum = hf_ref[pl.ds(lz, lx), :] + hb_ref[pl.ds(0, lx), :]
    o_ref[0] = (hsum * gate_ref[0]).astype(BF16)


def _lru(uz, ux, gate, conv_w, conv_b, wcat, bcat, lam, cast=()):
    bsz, lz, d_rnn = uz.shape
    lx = ux.shape[1]
    heads = d_rnn // LRU_BLOCK
    cast_specs, cast_shapes = _passenger_specs(cast, (bsz, heads))
    seg_len = (lz + lx) // LRU_SEGS
    assert seg_len * LRU_SEGS == lz + lx and seg_len % SUBLANES == SUBLANES // 2
    seq = lambda n: pl.BlockSpec((1, n, LRU_BLOCK), lambda b, h: (b, 0, h))
    scratch = pltpu.VMEM((lz + lx, LRU_BLOCK), F32)
    return pl.pallas_call(
        functools.partial(_lru_kernel, seg_len=seg_len),
        grid=(bsz, heads),
        in_specs=[seq(lz), seq(lx), seq(lx),
                  pl.BlockSpec((conv_w.shape[0], LRU_BLOCK), lambda b, h: (0, h)),
                  pl.BlockSpec((1, LRU_BLOCK), lambda b, h: (0, h)),
                  pl.BlockSpec((1, LRU_BLOCK, 4 * LRU_BLOCK), lambda b, h: (h, 0, 0)),
                  pl.BlockSpec((1, 1, 4 * LRU_BLOCK), lambda b, h: (h, 0, 0)),
                  pl.BlockSpec((2, LRU_BLOCK), lambda b, h: (0, h))] + cast_specs,
        out_specs=[seq(lx)] + cast_specs,
        out_shape=[jax.ShapeDtypeStruct((bsz, lx, d_rnn), BF16)] + cast_shapes,
        scratch_shapes=[scratch] * 6 + [pltpu.VMEM((lx + 2 * SUBLANES, LRU_BLOCK), F32)],
        compiler_params=_params("arbitrary", "arbitrary"),
        name="rglru",
    )(uz, ux, gate, conv_w, conv_b, wcat, bcat, lam, *cast)


def _store_row_tiles(ref, v):
    tm, d = v.shape
    for c in range(d // LANES):
        ref[pl.ds(c, tm, stride=d // LANES), :] = v[:, c * LANES:(c + 1) * LANES]


def _load_row_tiles(ref, tm, d):
    return [ref[pl.ds(c, tm, stride=d // LANES), :] for c in range(d // LANES)]


def _post_kernel(y_ref, x_ref, mod_ref, g2_ref, wo_ref, wr2_ref, x2_ref, h2t_ref, ri_ref, rt_ref):
    m = mod_ref[0]
    tm = x_ref.shape[1]
    rt = x_ref.shape[2] // LANES
    for q in range(tm // POST_SUB):
        rows = pl.ds(q * POST_SUB, POST_SUB)
        x2 = x_ref[0, rows, :] + m[2:3] * _dot(y_ref[0, rows, :], wo_ref[...])
        x2_ref[0, rows, :] = x2
        h2 = _norm_mod(x2, g2_ref[...], m[3:4], m[4:5])
        for c in range(rt):
            h2t_ref[pl.ds(q * POST_SUB * rt + c, POST_SUB, stride=rt), :] = h2[:, c * LANES:(c + 1) * LANES]
        hi, lo = _split_bf16(h2)
        both = _dot(hi, wr2_ref[...]) + _dot(lo, wr2_ref[...])
        logits = both + pltpu.roll(both, LANES - N_EXPERTS, 1)
        lane = lax.broadcasted_iota(jnp.int32, logits.shape, 1)
        logits = jnp.where(lane < N_EXPERTS, logits, -1e30)
        e = jnp.exp(logits - jnp.max(logits, axis=-1, keepdims=True))
        p = e / jnp.sum(e, axis=-1, keepdims=True)
        p1 = jnp.max(p, axis=-1, keepdims=True)
        i1 = jnp.min(jnp.where(p == p1, lane, LANES), axis=-1, keepdims=True)
        rest = jnp.where(lane == i1, -1.0, p)
        p2 = jnp.max(rest, axis=-1, keepdims=True)
        i2 = jnp.min(jnp.where(rest == p2, lane, LANES), axis=-1, keepdims=True)
        den = p1 + p2
        ri = jnp.where(lane == 0, i1.astype(F32),
                       jnp.where(lane == 1, i2.astype(F32),
                                 jnp.where(lane == 2, p1 / den,
                                           jnp.where(lane == 3, p2 / den, 0.0))))
        ri_ref[rows, :] = ri
        rt_ref[:, rows] = jnp.transpose(ri)[0:SUBLANES, :]


def _post(y, x, mods, g2, wo, wr2):
    bsz, t, d = x.shape
    d_rnn = y.shape[-1]
    tm = min(POST_TILE, t)
    nt = t // tm
    rt = d // LANES
    return pl.pallas_call(
        _post_kernel,
        grid=(bsz, nt),
        in_specs=[pl.BlockSpec((1, tm, d_rnn), lambda b, i: (b, i, 0)),
                  pl.BlockSpec((1, tm, d), lambda b, i: (b, i, 0)),
                  pl.BlockSpec((1, 6, d), lambda b, i: (b, 0, 0)),
                  _resident((1, d)), _resident(wo.shape), _resident(wr2.shape)],
        out_specs=[pl.BlockSpec((1, tm, d), lambda b, i: (b, i, 0)),
                   pl.BlockSpec((tm * rt, LANES), lambda b, i: (b * nt + i, 0)),
                   pl.BlockSpec((tm, LANES), lambda b, i: (b * nt + i, 0)),
                   pl.BlockSpec((SUBLANES, tm), lambda b, i: (0, b * nt + i))],
        out_shape=[jax.ShapeDtypeStruct(x.shape, F32),
                   jax.ShapeDtypeStruct((bsz * t * rt, LANES), F32),
                   jax.ShapeDtypeStruct((bsz * t, LANES), F32),
                   jax.ShapeDtypeStruct((SUBLANES, bsz * t), F32)],
        compiler_params=_params("parallel", "parallel"),
        name="outproj_router",
    )(y, x, mods, g2, wo, wr2)


def _moe_kernel(te_ref, nu_ref, first_ref, nxt_ref, prev_ref, h2t_ref, wg_ref, wu_ref, wd_ref, y2_ref,
                xs_buf, xb_ref, act_ref, acc_ref, out_buf, gsem, ssem, *, n_steps):
    i = pl.program_id(0)
    j = pl.program_id(1)
    tm, d = xb_ref.shape
    tf = act_ref.shape[1]
    rt = d // LANES
    nu = nu_ref[0]
    used = i < nu
    slot = i % 2

    def gather_row(idx_ref, buf_slot, r, priority=0):
        src_row = pl.multiple_of(idx_ref[r] * rt, rt)
        dst_row = pl.multiple_of(r * rt, rt)
        pltpu.make_async_copy(h2t_ref.at[pl.ds(src_row, rt), :],
                              xs_buf.at[buf_slot, pl.ds(dst_row, rt), :],
                              gsem.at[buf_slot]).start(priority=priority)

    def scatter_row(r, priority=0):
        src_row = pl.multiple_of(r * rt, rt)
        dst_row = pl.multiple_of(prev_ref[r] * rt, rt)
        pltpu.make_async_copy(out_buf.at[1 - slot, pl.ds(src_row, rt), :],
                              y2_ref.at[pl.ds(dst_row, rt), :], ssem).start(priority=priority)

    def wait_scatter():
        pltpu.make_async_copy(out_buf.at[0], out_buf.at[0], ssem).wait()

    @pl.when(jnp.logical_and(i == 0, j == 0))
    def _():
        out_buf[...] = jnp.zeros(out_buf.shape, out_buf.dtype)

        def issue(r, carry):
            gather_row(first_ref, 0, r)
            return carry
        lax.fori_loop(0, tm, issue, 0, unroll=DMA_UNROLL)

    @pl.when(jnp.logical_and(j == 0, i <= nu))
    def _():
        for s in range(2):
            @pl.when(slot == s)
            def _(s=s):
                pltpu.make_async_copy(xs_buf.at[s], xs_buf.at[s], gsem.at[s]).wait()

                @pl.when(used)
                def _():
                    for c in range(rt):
                        xb_ref[:, c * LANES:(c + 1) * LANES] = (
                            xs_buf[s, pl.ds(c, tm, stride=rt), :].astype(BF16))

        @pl.when(i == nu)
        def _():
            def issue(r, carry):
                scatter_row(r)
                return carry
            lax.fori_loop(0, tm, issue, 0, unroll=DMA_UNROLL)
            wait_scatter()

    def compute(step):
        xb = xb_ref[...]
        n_sub = tf // MOE_SUBCHUNK
        per = -(-tm // (2 * n_sub))

        def issue_rows(k):
            if step == 0:
                for r in range(k * per, min((k + 1) * per, tm)):
                    gather_row(nxt_ref, 1 - slot, r, priority=r % 2)
                    scatter_row(r, priority=(r + 1) % 2)

        for c in range(n_sub):
            cs = slice(c * MOE_SUBCHUNK, (c + 1) * MOE_SUBCHUNK)
            g = _dot(xb, wg_ref[0, :, cs])
            issue_rows(2 * c)
            u = _dot(xb, wu_ref[0, :, cs])
            issue_rows(2 * c + 1)
            act_ref[:, cs] = ((g * jax.nn.sigmoid(g)) * u).astype(BF16)
        part = _dot(act_ref[...], wd_ref[0])
        if step < n_steps - 1:
            acc_ref[...] = part if step == 0 else acc_ref[...] + part
        else:
            total = part if step == 0 else acc_ref[...] + part
            for c in range(rt):
                out_buf[slot, pl.ds(c, tm, stride=rt), :] = total[:, c * LANES:(c + 1) * LANES]
            wait_scatter()

    for step in range(n_steps):
        @pl.when(jnp.logical_and(used, j == step))
        def _(step=step):
            compute(step)


def _moe(tile_e, n_used, slot_tok, prev_dst, h2t, wgu, wd, n_tok):
    d = wgu.shape[1]
    d_e = wd.shape[1]
    rt = d // LANES
    tm = MOE_TILE
    n_tiles = slot_tok.shape[0] // tm
    nj = d_e // MOE_FCHUNK

    def jf(i, j, nu):
        return jnp.where(i < nu[0], j, nj - 1)

    smem_tile = lambda f: pl.BlockSpec((tm,), f, memory_space=pltpu.SMEM)
    return pl.pallas_call(
        functools.partial(_moe_kernel, n_steps=nj),
        grid_spec=pltpu.PrefetchScalarGridSpec(
            num_scalar_prefetch=2,
            grid=(n_tiles, nj),
            in_specs=[smem_tile(lambda i, j, te, nu: (0,)),
                      smem_tile(lambda i, j, te, nu: (jnp.minimum(i + 1, n_tiles - 1),)),
                      smem_tile(lambda i, j, te, nu: (i,)),
                      pl.BlockSpec(memory_space=pl.ANY),
                      pl.BlockSpec((1, d, MOE_FCHUNK), lambda i, j, te, nu: (te[i], 0, jf(i, j, nu))),
                      pl.BlockSpec((1, d, MOE_FCHUNK), lambda i, j, te, nu: (te[i], 0, nj + jf(i, j, nu))),
                      pl.BlockSpec((1, MOE_FCHUNK, d), lambda i, j, te, nu: (te[i], jf(i, j, nu), 0))],
            out_specs=pl.BlockSpec(memory_space=pl.ANY),
            scratch_shapes=[pltpu.VMEM((2, tm * rt, LANES), F32),
                            pltpu.VMEM((tm, d), BF16),
                            pltpu.VMEM((tm, MOE_FCHUNK), BF16),
                            pltpu.VMEM((tm, d), F32),
                            pltpu.VMEM((2, tm * rt, LANES), F32),
                            pltpu.SemaphoreType.DMA((2,)),
                            pltpu.SemaphoreType.DMA]),
        out_shape=jax.ShapeDtypeStruct(((2 * n_tok + tm) * rt, LANES), F32),
        compiler_params=_params("arbitrary", "arbitrary"),
        name="moe_experts",
    )(tile_e, n_used, slot_tok, slot_tok, prev_dst, h2t, wgu, wgu, wd)


def _combine_kernel(x_ref, ri_ref, mod_ref, fg_ref, ya_ref, yb_ref, o_ref):
    tm, d = x_ref.shape[1], x_ref.shape[2]
    ri = ri_ref[...]
    w1 = ri[:, 2:3]
    w2 = ri[:, 3:4]
    ya = _load_row_tiles(ya_ref, tm, d)
    yb = _load_row_tiles(yb_ref, tm, d)
    moe = jnp.concatenate([w1 * a + w2 * b for a, b in zip(ya, yb)], axis=1)
    x3 = x_ref[0] + mod_ref[0][5:6] * moe
    ms = jnp.mean(x3 * x3, axis=-1, keepdims=True)
    o_ref[0] = (x3 * lax.rsqrt(ms + EPS)) * fg_ref[...]


def _combine(x2, ri, mods, fg, y2):
    bsz, t, d = x2.shape
    tm = min(COMBINE_TILE, t)
    nt = t // tm
    rt = d // LANES
    n_blocks = bsz * nt
    return pl.pallas_call(
        _combine_kernel,
        grid=(bsz, nt),
        in_specs=[pl.BlockSpec((1, tm, d), lambda b, i: (b, i, 0)),
                  pl.BlockSpec((tm, LANES), lambda b, i: (b * nt + i, 0)),
                  pl.BlockSpec((1, 6, d), lambda b, i: (b, 0, 0)),
                  pl.BlockSpec((1, d), lambda b, i: (0, 0)),
                  pl.BlockSpec((tm * rt, LANES), lambda b, i: (b * nt + i, 0)),
                  pl.BlockSpec((tm * rt, LANES), lambda b, i: (n_blocks + b * nt + i, 0))],
        out_specs=pl.BlockSpec((1, tm, d), lambda b, i: (b, i, 0)),
        out_shape=jax.ShapeDtypeStruct(x2.shape, F32),
        compiler_params=_params("parallel", "parallel"),
        name="moe_combine_norm",
    )(x2, ri, mods, fg, y2, y2)


def _routing(flat_e, n_tok):
    n2 = flat_e.shape[0]
    experts = jnp.arange(N_EXPERTS, dtype=jnp.int32)
    counts = jnp.sum((flat_e[:, None] == experts[None, :]).astype(jnp.int32), axis=0)
    padded = (counts + MOE_TILE - 1) // MOE_TILE * MOE_TILE
    pends = jnp.cumsum(padded)
    n_tiles = n2 // MOE_TILE + N_EXPERTS + 1
    idx_bits = max(n2, MOE_TILE).bit_length()
    r = jnp.arange(MOE_TILE, dtype=jnp.int32)
    pad_e = jnp.where(r[None, :] < (padded - counts)[:, None], experts[:, None], N_EXPERTS)
    pad_e = jnp.concatenate([pad_e.reshape(-1), jnp.full((MOE_TILE,), N_EXPERTS, jnp.int32)])
    keys = jnp.concatenate([
        (flat_e << (idx_bits + 1)) | jnp.arange(n2, dtype=jnp.int32),
        (pad_e << (idx_bits + 1)) | (1 << idx_bits) | jnp.tile(r, N_EXPERTS + 1)])
    keys = jnp.sort(keys)
    valid = ((keys >> idx_bits) & 1) == 0
    f = keys & ((1 << idx_bits) - 1)
    slot = jnp.arange(n_tiles * MOE_TILE, dtype=jnp.int32)
    slot_tok = jnp.where(valid, f % n_tok, 0).astype(jnp.int32)
    slot_dst = jnp.where(valid, f, n2 + slot % MOE_TILE).astype(jnp.int32)
    prev_dst = jnp.concatenate([n2 + jnp.arange(MOE_TILE, dtype=jnp.int32), slot_dst[:-MOE_TILE]])
    tile0 = jnp.arange(n_tiles, dtype=jnp.int32) * MOE_TILE
    tile_e = jnp.minimum(jnp.sum((tile0[:, None] >= pends[None, :]).astype(jnp.int32), axis=1), N_EXPERTS - 1)
    n_used = (pends[-1] // MOE_TILE).astype(jnp.int32).reshape(1)
    return slot_tok, prev_dst, tile_e.astype(jnp.int32), n_used


def kernel(x, c, ctx, c_ctx, ada_w, ada_b, norm1_g, norm2_g, pool_w, pool_scale, ffn_w_gu, ffn_w_down,
           lru_w_in, lru_conv_w, lru_conv_b, lru_w_r, lru_b_r, lru_w_i, lru_b_i, lru_lambda, lru_w_out,
           moe_w_router, moe_w_gu, moe_w_down, final_g):
    bsz, seq, d = x.shape
    assert ada_w.shape[0] == 2 and seq % TOK_TILE == 0 and TOK_TILE % POOL_TILE == 0 and ctx.shape[1] == POOL_TILE
    assert d == SUBLANES * LANES
    d_rnn = lru_w_out.shape[1]
    heads = d_rnn // LRU_BLOCK
    gc = d // len(POOL_WINDOWS)
    z_row = bsz

    n_rows = (bsz + 1 + SUBLANES - 1) // SUBLANES * SUBLANES
    cc = jnp.concatenate([c, c_ctx[None, :], jnp.zeros((n_rows - bsz - 1, d), F32)], axis=0)
    n_grp = pool_w.shape[1]
    mods, wgu0, wd0, pw0 = _adaln(cc, ada_w, ada_b, cast=(ffn_w_gu[0], ffn_w_down[0], pool_w[0].reshape(n_grp * gc, gc)))
    mods = mods.reshape(2, n_rows, 6, d)
    x_row = lambda b: b
    ctx_row = lambda b: z_row

    band_x, invc_x = _pool_tables(GRID_W, gc)
    band_z, invc_z = _pool_tables(POOL_TILE, gc)
    l0 = (norm1_g[0:1], norm2_g[0:1])
    l0w = (pw0.reshape(n_grp, gc, gc), pool_scale[0:1], wgu0, wd0)
    n_exp, _, d_gu = moe_w_gu.shape[1:]
    d_exp = moe_w_down.shape[2]
    x1, wgu_e = _layer0(x, mods[0], x_row, *l0, band_x, invc_x, *l0w, name="layer0_latent",
                        cast=(moe_w_gu[0].reshape(n_exp * d, d_gu),))
    z1, w_in, w_out = _layer0(ctx, mods[0], ctx_row, *l0, band_z, invc_z, *l0w, name="layer0_context",
                              cast=(lru_w_in[0], lru_w_out[0]))

    gate_x, ux = _inproj(x1, mods[1], x_row, norm1_g[1:2], w_in, True, name="lru_inproj_latent")
    (uz,) = _inproj(z1, mods[1], ctx_row, norm1_g[1:2], w_in[:, d_rnn:], False, name="lru_inproj_context")
    wcat = (0.5 * jnp.concatenate([lru_w_r[0, 0], lru_w_i[0, 0], lru_w_r[0, 1], lru_w_i[0, 1]], axis=-1)).astype(BF16)
    bcat = 0.5 * jnp.concatenate([v.reshape(heads, 1, LRU_BLOCK)
                                  for v in (lru_b_r[0, 0], lru_b_i[0, 0], lru_b_r[0, 1], lru_b_i[0, 1])], axis=-1)
    y, wd_e = _lru(uz, ux, gate_x, lru_conv_w[0], lru_conv_b[0:1], wcat, bcat, lru_lambda[0],
                   cast=(moe_w_down[0].reshape(n_exp * d_exp, d),))
    wgu_e = wgu_e.reshape(n_exp, d, d_gu)
    wd_e = wd_e.reshape(n_exp, d_exp, d)

    wr_hi = moe_w_router[0].astype(BF16)
    wr_lo = (moe_w_router[0] - wr_hi.astype(F32)).astype(BF16)
    wr2 = jnp.pad(jnp.concatenate([wr_hi, wr_lo], axis=1), ((0, 0), (0, LANES - 2 * N_EXPERTS)))
    x2, h2t, ri, rt = _post(y, x1, mods[1], norm2_g[1:2], w_out, wr2)

    n_tok = bsz * seq
    flat_e = rt[0:2].reshape(-1).astype(jnp.int32)
    slot_tok, prev_dst, tile_e, n_used = _routing(flat_e, n_tok)
    y2 = _moe(tile_e, n_used, slot_tok, prev_dst, h2t, wgu_e, wd_e, n_tok)
    return _combine(x2, ri, mods[1], final_g[None, :], y2)
```

```python
import functools

import numpy as np
import jax
import jax.numpy as jnp
from jax import lax
from jax.experimental import pallas as pl
from jax.experimental.pallas import tpu as pltpu

F32 = jnp.float32
BF16 = jnp.bfloat16

GRID_W = 64
POOL_WINDOWS = (2, 4, 8, 16)
LRU_BLOCK = 128
LRU_C = 8.0
N_EXPERTS = 8
EPS = 1e-6
LOG2_E = 1.4426950408889634

LANES = 128
SUBLANES = 8
VMEM_LIMIT = 56 * 1024 * 1024

POOL_TILE = 256
TOK_TILE = 512
INPROJ_TILE = 1024
FF_CHUNK = 256
MOE_TILE = 1024
MOE_FCHUNK = 1792
MOE_SUBCHUNK = 256
DMA_UNROLL = 8
POST_TILE = 1024
POST_SUB = 256
COMBINE_TILE = 1024
LRU_SEGS = 64


def _dot(a, b):
    return jnp.dot(a, b, preferred_element_type=F32)


def _params(*sem):
    return pltpu.CompilerParams(dimension_semantics=sem, vmem_limit_bytes=VMEM_LIMIT)


def _resident(shape):
    return pl.BlockSpec(shape, lambda *_: (0,) * len(shape), pipeline_mode=pl.Buffered(1))


def _passenger_specs(cast, grid):
    steps = grid[0] * grid[1]
    n = max(k for k in range(1, steps + 1) if all(w.shape[0] % (2 * SUBLANES * k) == 0 for w in cast))
    chunk = lambda a, b: (jnp.minimum(a * grid[1] + b, n - 1), 0)
    specs = [pl.BlockSpec((w.shape[0] // n, w.shape[1]), chunk) for w in cast]
    return specs, [jax.ShapeDtypeStruct(w.shape, BF16) for w in cast]


def _run_passengers(srcs, dsts):
    for src, dst in zip(srcs, dsts):
        dst[...] = src[...].astype(dst.dtype)


def _norm_mod(x, g, shift, scale):
    ms = jnp.mean(x * x, axis=-1, keepdims=True)
    return (x * lax.rsqrt(ms + EPS)) * (g * (1.0 + scale)) + shift


def _split_bf16(v):
    hi = v.astype(BF16)
    lo = (v - hi.astype(F32)).astype(BF16)
    return hi, lo


def _ada_kernel(cc_ref, w_ref, b_ref, *refs):
    n_cast = len(refs) // 2
    o_ref = refs[n_cast]
    _run_passengers(refs[:n_cast], refs[n_cast + 1:])
    s = cc_ref[...]
    s = s * jax.nn.sigmoid(s)
    o_ref[0] = _dot(s.astype(BF16), w_ref[0].astype(BF16)) + b_ref[0]


def _adaln(cc, ada_w, ada_b, cast=()):
    n_layers, d, n = ada_w.shape
    r = cc.shape[0]
    tn = 1536
    grid = (n_layers, n // tn)
    cast_specs, cast_shapes = _passenger_specs(cast, grid)
    return pl.pallas_call(
        _ada_kernel,
        grid=grid,
        in_specs=[pl.BlockSpec((r, d), lambda l, j: (0, 0)),
                  pl.BlockSpec((1, d, tn), lambda l, j: (l, 0, j)),
                  pl.BlockSpec((1, 1, tn), lambda l, j: (l, 0, j))] + cast_specs,
        out_specs=[pl.BlockSpec((1, r, tn), lambda l, j: (l, 0, j))] + cast_specs,
        out_shape=[jax.ShapeDtypeStruct((n_layers, r, n), F32)] + cast_shapes,
        compiler_params=_params("arbitrary", "arbitrary"),
        name="adaln",
    )(cc, ada_w, ada_b.reshape(n_layers, 1, n), *cast)


def _pool_tables(seg_len, gc):
    t = np.arange(POOL_TILE)
    seg, pos = t // seg_len, t % seg_len
    band = np.zeros((len(POOL_WINDOWS), POOL_TILE, POOL_TILE), np.float32)
    invc = np.zeros((len(POOL_WINDOWS), POOL_TILE, gc), np.float32)
    for gi, win in enumerate(POOL_WINDOWS):
        lo = np.clip(pos - win // 2, 0, seg_len)
        hi = np.clip(pos - win // 2 + win, 0, seg_len)
        inside = (pos[None, :] >= lo[:, None]) & (pos[None, :] < hi[:, None])
        band[gi] = (seg[:, None] == seg[None, :]) & inside
        invc[gi] = (1.0 / (hi - lo))[:, None]
    return jnp.asarray(band, BF16), jnp.asarray(invc, F32)


def _layer0_kernel(x_ref, mod_ref, g1_ref, g2_ref, band_ref, invc_ref, pw_ref, ps_ref,
                   wgu_ref, wd_ref, *refs, d_ff):
    n_cast = (len(refs) - 2) // 2
    o_ref, act_ref = refs[n_cast], refs[-1]
    _run_passengers(refs[:n_cast], refs[n_cast + 1:-1])
    m = mod_ref[0]
    x = x_ref[0]
    tm, d = x.shape
    n_groups = band_ref.shape[0]
    gc = d // n_groups
    h = _norm_mod(x, g1_ref[...], m[0:1], m[1:2])
    rows = []
    for s in range(tm // POOL_TILE):
        hs = h[s * POOL_TILE:(s + 1) * POOL_TILE]
        cols = []
        for gi in range(n_groups):
            hg = hs[:, gi * gc:(gi + 1) * gc]
            hi, lo = _split_bf16(hg)
            band = band_ref[gi]
            pooled = (_dot(band, hi) + _dot(band, lo)) * invc_ref[gi] - hg
            cols.append(_dot(pooled.astype(BF16), pw_ref[gi]))
        rows.append(jnp.concatenate(cols, axis=1))
    y = rows[0] if len(rows) == 1 else jnp.concatenate(rows, axis=0)
    x1 = x + m[2:3] * (y * ps_ref[...])
    h2 = _norm_mod(x1, g2_ref[...], m[3:4], m[4:5]).astype(BF16)
    for c in range(d_ff // FF_CHUNK):
        g = _dot(h2, wgu_ref[:, c * FF_CHUNK:(c + 1) * FF_CHUNK])
        u = _dot(h2, wgu_ref[:, d_ff + c * FF_CHUNK:d_ff + (c + 1) * FF_CHUNK])
        act_ref[:, c * FF_CHUNK:(c + 1) * FF_CHUNK] = ((g * jax.nn.sigmoid(g)) * u).astype(BF16)
    o_ref[0] = x1 + m[5:6] * _dot(act_ref[...], wd_ref[...])


def _layer0(x, mods, mod_row, g1, g2, band, invc, pw, ps, wgu, wd, name, cast=()):
    bsz, t, d = x.shape
    d_ff = wd.shape[0]
    tm = min(TOK_TILE, t)
    grid = (bsz, t // tm)
    cast_specs, cast_shapes = _passenger_specs(cast, grid)
    return pl.pallas_call(
        functools.partial(_layer0_kernel, d_ff=d_ff),
        grid=grid,
        in_specs=[pl.BlockSpec((1, tm, d), lambda b, i: (b, i, 0)),
                  pl.BlockSpec((1, 6, d), lambda b, i: (mod_row(b), 0, 0)),
                  _resident((1, d)), _resident((1, d)),
                  _resident(band.shape), _resident(invc.shape), _resident(pw.shape), _resident((1, d)),
                  _resident(wgu.shape), _resident(wd.shape)] + cast_specs,
        out_specs=[pl.BlockSpec((1, tm, d), lambda b, i: (b, i, 0))] + cast_specs,
        out_shape=[jax.ShapeDtypeStruct(x.shape, F32)] + cast_shapes,
        scratch_shapes=[pltpu.VMEM((tm, d_ff), BF16)],
        compiler_params=_params("arbitrary", "arbitrary"),
        name=name,
    )(x, mods, g1, g2, band, invc, pw, ps, wgu, wd, *cast)


def _inproj_kernel(x_ref, mod_ref, g1_ref, w_ref, *out_refs, with_gate):
    m = mod_ref[0]
    h = _norm_mod(x_ref[0], g1_ref[...], m[0:1], m[1:2]).astype(BF16)
    y = _dot(h, w_ref[...])
    n = out_refs[0].shape[-1]
    if with_gate:
        out_refs[0][0] = jax.nn.gelu(y[:, :n])
        out_refs[1][0] = y[:, n:]
    else:
        out_refs[0][0] = y


def _inproj(x, mods, mod_row, g1, w, with_gate, name):
    bsz, t, d = x.shape
    tm = min(INPROJ_TILE, t)
    n_out = 2 if with_gate else 1
    n = w.shape[1] // n_out
    out_shape = jax.ShapeDtypeStruct((bsz, t, n), F32)
    out_spec = pl.BlockSpec((1, tm, n), lambda b, i: (b, i, 0))
    return pl.pallas_call(
        functools.partial(_inproj_kernel, with_gate=with_gate),
        grid=(bsz, t // tm),
        in_specs=[pl.BlockSpec((1, tm, d), lambda b, i: (b, i, 0)),
                  pl.BlockSpec((1, 6, d), lambda b, i: (mod_row(b), 0, 0)),
                  _resident((1, d)), _resident(w.shape)],
        out_specs=[out_spec] * n_out,
        out_shape=[out_shape] * n_out,
        compiler_params=_params("parallel", "parallel"),
        name=name,
    )(x, mods, g1, w)


def _scan8(a, b, row, reverse):
    for s in (1, 2, 4):
        if reverse:
            keep = row < SUBLANES - s
            shift = SUBLANES - s
        else:
            keep = row >= s
            shift = s
        a_sh = jnp.where(keep, pltpu.roll(a, shift, 0), 1.0)
        b_sh = jnp.where(keep, pltpu.roll(b, shift, 0), 0.0)
        b = a * b_sh + b
        a = a * a_sh
    return a, b


def _segment_scan(a_ref, b_ref, h_ref, row8, seg_len, reverse):
    nv = LRU_SEGS // SUBLANES

    def rows(k, j):
        jj = seg_len - 1 - j if reverse else j
        return pl.ds(k * SUBLANES * seg_len + jj, SUBLANES, stride=seg_len)

    zero = jnp.zeros((SUBLANES, LANES), F32)
    h = [zero] * nv
    p = [jnp.ones((SUBLANES, LANES), F32)] * nv
    for j in range(seg_len):
        for k in range(nv):
            a = a_ref[rows(k, j), :]
            h[k] = a * h[k] + b_ref[rows(k, j), :]
            p[k] = a * p[k]
    carry = zero
    start = [None] * nv
    for k in (range(nv - 1, -1, -1) if reverse else range(nv)):
        pk, hk = _scan8(p[k], h[k], row8, reverse)
        ends = hk + pk * carry
        if reverse:
            start[k] = jnp.where(row8 == SUBLANES - 1, carry, pltpu.roll(ends, SUBLANES - 1, 0))
            carry = jnp.broadcast_to(ends[0:1], ends.shape)
        else:
            start[k] = jnp.where(row8 == 0, carry, pltpu.roll(ends, 1, 0))
            carry = jnp.broadcast_to(ends[SUBLANES - 1:SUBLANES], ends.shape)
    h = start
    for j in range(seg_len):
        for k in range(nv):
            h[k] = a_ref[rows(k, j), :] * h[k] + b_ref[rows(k, j), :]
            h_ref[rows(k, j), :] = h[k]


def _lru_kernel(uz_ref, ux_ref, gate_ref, cw_ref, cb_ref, wc_ref, bc_ref, lam_ref, *refs, seg_len):
    n_cast = (len(refs) - 8) // 2
    cast_in, (o_ref, *cast_out) = refs[:n_cast], refs[n_cast:2 * n_cast + 1]
    af_ref, bf_ref, ab_ref, bb_ref, hf_ref, hb_ref, pad_ref = refs[2 * n_cast + 1:]
    _run_passengers(cast_in, cast_out)
    lz = uz_ref.shape[1]
    lx = ux_ref.shape[1]
    cw = cw_ref[...]
    cb = cb_ref[...]
    wc = wc_ref[0]
    bc = bc_ref[0]
    neg_lam = -lam_ref[...]
    softplus = jnp.maximum(neg_lam, 0.0) + jnp.log1p(jnp.exp(-jnp.abs(neg_lam)))
    kh = (-0.5 * LRU_C * LOG2_E) * softplus
    zeros8 = jnp.zeros((SUBLANES, LANES), F32)

    def prep(src_ref, off_f, off_b):
        n = src_ref.shape[1]
        pad_ref[pl.ds(0, SUBLANES), :] = zeros8
        pad_ref[pl.ds(SUBLANES, n), :] = src_ref[0]
        pad_ref[pl.ds(SUBLANES + n, SUBLANES), :] = zeros8
        u = cb + cw[2:3] * src_ref[0]
        for k in (0, 1, 3):
            u = u + cw[k:k + 1] * pad_ref[pl.ds(SUBLANES + k - 2, n), :]
        t = jnp.tanh(_dot(u.astype(BF16), wc) + bc)
        hu = 0.5 * u
        for d, (a_ref, b_ref, off) in enumerate(((af_ref, bf_ref, off_f), (ab_ref, bb_ref, off_b))):
            t_r = t[:, 2 * d * LRU_BLOCK:(2 * d + 1) * LRU_BLOCK]
            t_i = t[:, (2 * d + 1) * LRU_BLOCK:(2 * d + 2) * LRU_BLOCK]
            a = jnp.exp2(kh[d:d + 1] * t_r + kh[d:d + 1])
            y = 1.0 - a * a
            root = jnp.where(y > 0.0, y * lax.rsqrt(y), 0.0)
            a_ref[pl.ds(off, n), :] = a
            b_ref[pl.ds(off, n), :] = root * (hu * t_i + hu)

    prep(uz_ref, 0, lx)
    prep(ux_ref, lz, 0)
    row8 = lax.broadcasted_iota(jnp.int32, (SUBLANES, LANES), 0)
    _segment_scan(af_ref, bf_ref, hf_ref, row8, seg_len, False)
    _segment_scan(ab_ref, bb_ref, hb_ref, row8, seg_len, True)
    hsum = hf_ref[pl.ds(lz, lx), :] + hb_ref[pl.ds(0, lx), :]
    o_ref[0] = (hsum * gate_ref[0]).astype(BF16)


def _lru(uz, ux, gate, conv_w, conv_b, wcat, bcat, lam, cast=()):
    bsz, lz, d_rnn = uz.shape
    lx = ux.shape[1]
    heads = d_rnn // LRU_BLOCK
    cast_specs, cast_shapes = _passenger_specs(cast, (bsz, heads))
    seg_len = (lz + lx) // LRU_SEGS
    assert seg_len * LRU_SEGS == lz + lx and seg_len % SUBLANES == SUBLANES // 2
    seq = lambda n: pl.BlockSpec((1, n, LRU_BLOCK), lambda b, h: (b, 0, h))
    scratch = pltpu.VMEM((lz + lx, LRU_BLOCK), F32)
    return pl.pallas_call(
        functools.partial(_lru_kernel, seg_len=seg_len),
        grid=(bsz, heads),
        in_specs=[seq(lz), seq(lx), seq(lx),
                  pl.BlockSpec((conv_w.shape[0], LRU_BLOCK), lambda b, h: (0, h)),
                  pl.BlockSpec((1, LRU_BLOCK), lambda b, h: (0, h)),
                  pl.BlockSpec((1, LRU_BLOCK, 4 * LRU_BLOCK), lambda b, h: (h, 0, 0)),
                  pl.BlockSpec((1, 1, 4 * LRU_BLOCK), lambda b, h: (h, 0, 0)),
                  pl.BlockSpec((2, LRU_BLOCK), lambda b, h: (0, h))] + cast_specs,
        out_specs=[seq(lx)] + cast_specs,
        out_shape=[jax.ShapeDtypeStruct((bsz, lx, d_rnn), BF16)] + cast_shapes,
        scratch_shapes=[scratch] * 6 + [pltpu.VMEM((lx + 2 * SUBLANES, LRU_BLOCK), F32)],
        compiler_params=_params("arbitrary", "arbitrary"),
        name="rglru",
    )(uz, ux, gate, conv_w, conv_b, wcat, bcat, lam, *cast)


def _store_row_tiles(ref, v):
    tm, d = v.shape
    for c in range(d // LANES):
        ref[pl.ds(c, tm, stride=d // LANES), :] = v[:, c * LANES:(c + 1) * LANES]


def _load_row_tiles(ref, tm, d):
    return [ref[pl.ds(c, tm, stride=d // LANES), :] for c in range(d // LANES)]


def _post_kernel(y_ref, x_ref, mod_ref, g2_ref, wo_ref, wr2_ref, x2_ref, h2t_ref, ri_ref, rt_ref):
    m = mod_ref[0]
    tm = x_ref.shape[1]
    rt = x_ref.shape[2] // LANES
    for q in range(tm // POST_SUB):
        rows = pl.ds(q * POST_SUB, POST_SUB)
        x2 = x_ref[0, rows, :] + m[2:3] * _dot(y_ref[0, rows, :], wo_ref[...])
        x2_ref[0, rows, :] = x2
        h2 = _norm_mod(x2, g2_ref[...], m[3:4], m[4:5])
        for c in range(rt):
            h2t_ref[pl.ds(q * POST_SUB * rt + c, POST_SUB, stride=rt), :] = h2[:, c * LANES:(c + 1) * LANES]
        hi, lo = _split_bf16(h2)
        both = _dot(hi, wr2_ref[...]) + _dot(lo, wr2_ref[...])
        logits = both + pltpu.roll(both, LANES - N_EXPERTS, 1)
        lane = lax.broadcasted_iota(jnp.int32, logits.shape, 1)
        logits = jnp.where(lane < N_EXPERTS, logits, -1e30)
        e = jnp.exp(logits - jnp.max(logits, axis=-1, keepdims=True))
        p = e / jnp.sum(e, axis=-1, keepdims=True)
        p1 = jnp.max(p, axis=-1, keepdims=True)
        i1 = jnp.min(jnp.where(p == p1, lane, LANES), axis=-1, keepdims=True)
        rest = jnp.where(lane == i1, -1.0, p)
        p2 = jnp.max(rest, axis=-1, keepdims=True)
        i2 = jnp.min(jnp.where(rest == p2, lane, LANES), axis=-1, keepdims=True)
        den = p1 + p2
        ri = jnp.where(lane == 0, i1.astype(F32),
                       jnp.where(lane == 1, i2.astype(F32),
                                 jnp.where(lane == 2, p1 / den,
                                           jnp.where(lane == 3, p2 / den, 0.0))))
        ri_ref[rows, :] = ri
        rt_ref[:, rows] = jnp.transpose(ri)[0:SUBLANES, :]


def _post(y, x, mods, g2, wo, wr2):
    bsz, t, d = x.shape
    d_rnn = y.shape[-1]
    tm = min(POST_TILE, t)
    nt = t // tm
    rt = d // LANES
    return pl.pallas_call(
        _post_kernel,
        grid=(bsz, nt),
        in_specs=[pl.BlockSpec((1, tm, d_rnn), lambda b, i: (b, i, 0)),
                  pl.BlockSpec((1, tm, d), lambda b, i: (b, i, 0)),
                  pl.BlockSpec((1, 6, d), lambda b, i: (b, 0, 0)),
                  _resident((1, d)), _resident(wo.shape), _resident(wr2.shape)],
        out_specs=[pl.BlockSpec((1, tm, d), lambda b, i: (b, i, 0)),
                   pl.BlockSpec((tm * rt, LANES), lambda b, i: (b * nt + i, 0)),
                   pl.BlockSpec((tm, LANES), lambda b, i: (b * nt + i, 0)),
                   pl.BlockSpec((SUBLANES, tm), lambda b, i: (0, b * nt + i))],
        out_shape=[jax.ShapeDtypeStruct(x.shape, F32),
                   jax.ShapeDtypeStruct((bsz * t * rt, LANES), F32),
                   jax.ShapeDtypeStruct((bsz * t, LANES), F32),
                   jax.ShapeDtypeStruct((SUBLANES, bsz * t), F32)],
        compiler_params=_params("parallel", "parallel"),
        name="outproj_router",
    )(y, x, mods, g2, wo, wr2)


def _moe_kernel(te_ref, nu_ref, first_ref, nxt_ref, prev_ref, h2t_ref, wg_ref, wu_ref, wd_ref, y2_ref,
                xs_buf, xb_ref, act_ref, acc_ref, out_buf, gsem, ssem, *, n_steps):
    i = pl.program_id(0)
    j = pl.program_id(1)
    tm, d = xb_ref.shape
    tf = act_ref.shape[1]
    rt = d // LANES
    nu = nu_ref[0]
    used = i < nu
    slot = i % 2

    def gather_row(idx_ref, buf_slot, r, priority=0):
        src_row = pl.multiple_of(idx_ref[r] * rt, rt)
        dst_row = pl.multiple_of(r * rt, rt)
        pltpu.make_async_copy(h2t_ref.at[pl.ds(src_row, rt), :],
                              xs_buf.at[buf_slot, pl.ds(dst_row, rt), :],
                              gsem.at[buf_slot]).start(priority=priority)

    def scatter_row(r, priority=0):
        src_row = pl.multiple_of(r * rt, rt)
        dst_row = pl.multiple_of(prev_ref[r] * rt, rt)
        pltpu.make_async_copy(out_buf.at[1 - slot, pl.ds(src_row, rt), :],
                              y2_ref.at[pl.ds(dst_row, rt), :], ssem).start(priority=priority)

    def wait_scatter():
        pltpu.make_async_copy(out_buf.at[0], out_buf.at[0], ssem).wait()

    @pl.when(jnp.logical_and(i == 0, j == 0))
    def _():
        out_buf[...] = jnp.zeros(out_buf.shape, out_buf.dtype)

        def issue(r, carry):
            gather_row(first_ref, 0, r)
            return carry
        lax.fori_loop(0, tm, issue, 0, unroll=DMA_UNROLL)

    @pl.when(jnp.logical_and(j == 0, i <= nu))
    def _():
        for s in range(2):
            @pl.when(slot == s)
            def _(s=s):
                pltpu.make_async_copy(xs_buf.at[s], xs_buf.at[s], gsem.at[s]).wait()

                @pl.when(used)
                def _():
                    for c in range(rt):
                        xb_ref[:, c * LANES:(c + 1) * LANES] = (
                            xs_buf[s, pl.ds(c, tm, stride=rt), :].astype(BF16))

        @pl.when(i == nu)
        def _():
            def issue(r, carry):
                scatter_row(r)
                return carry
            lax.fori_loop(0, tm, issue, 0, unroll=DMA_UNROLL)
            wait_scatter()

    def compute(step):
        xb = xb_ref[...]
        n_sub = tf // MOE_SUBCHUNK
        per = -(-tm // (2 * n_sub))

        def issue_rows(k):
            if step == 0:
                for r in range(k * per, min((k + 1) * per, tm)):
                    gather_row(nxt_ref, 1 - slot, r, priority=r % 2)
                    scatter_row(r, priority=(r + 1) % 2)

        for c in range(n_sub):
            cs = slice(c * MOE_SUBCHUNK, (c + 1) * MOE_SUBCHUNK)
            g = _dot(xb, wg_ref[0, :, cs])
            issue_rows(2 * c)
            u = _dot(xb, wu_ref[0, :, cs])
            issue_rows(2 * c + 1)
            act_ref[:, cs] = ((g * jax.nn.sigmoid(g)) * u).astype(BF16)
        part = _dot(act_ref[...], wd_ref[0])
        if step < n_steps - 1:
            acc_ref[...] = part if step == 0 else acc_ref[...] + part
        else:
            total = part if step == 0 else acc_ref[...] + part
            for c in range(rt):
                out_buf[slot, pl.ds(c, tm, stride=rt), :] = total[:, c * LANES:(c + 1) * LANES]
            wait_scatter()

    for step in range(n_steps):
        @pl.when(jnp.logical_and(used, j == step))
        def _(step=step):
            compute(step)


def _moe(tile_e, n_used, slot_tok, prev_dst, h2t, wgu, wd, n_tok):
    d = wgu.shape[1]
    d_e = wd.shape[1]
    rt = d // LANES
    tm = MOE_TILE
    n_tiles = slot_tok.shape[0] // tm
    nj = d_e // MOE_FCHUNK

    def jf(i, j, nu):
        return jnp.where(i < nu[0], j, nj - 1)

    smem_tile = lambda f: pl.BlockSpec((tm,), f, memory_space=pltpu.SMEM)
    return pl.pallas_call(
        functools.partial(_moe_kernel, n_steps=nj),
        grid_spec=pltpu.PrefetchScalarGridSpec(
            num_scalar_prefetch=2,
            grid=(n_tiles, nj),
            in_specs=[smem_tile(lambda i, j, te, nu: (0,)),
                      smem_tile(lambda i, j, te, nu: (jnp.minimum(i + 1, n_tiles - 1),)),
                      smem_tile(lambda i, j, te, nu: (i,)),
                      pl.BlockSpec(memory_space=pl.ANY),
                      pl.BlockSpec((1, d, MOE_FCHUNK), lambda i, j, te, nu: (te[i], 0, jf(i, j, nu))),
                      pl.BlockSpec((1, d, MOE_FCHUNK), lambda i, j, te, nu: (te[i], 0, nj + jf(i, j, nu))),
                      pl.BlockSpec((1, MOE_FCHUNK, d), lambda i, j, te, nu: (te[i], jf(i, j, nu), 0))],
            out_specs=pl.BlockSpec(memory_space=pl.ANY),
            scratch_shapes=[pltpu.VMEM((2, tm * rt, LANES), F32),
                            pltpu.VMEM((tm, d), BF16),
                            pltpu.VMEM((tm, MOE_FCHUNK), BF16),
                            pltpu.VMEM((tm, d), F32),
                            pltpu.VMEM((2, tm * rt, LANES), F32),
                            pltpu.SemaphoreType.DMA((2,)),
                            pltpu.SemaphoreType.DMA]),
        out_shape=jax.ShapeDtypeStruct(((2 * n_tok + tm) * rt, LANES), F32),
        compiler_params=_params("arbitrary", "arbitrary"),
        name="moe_experts",
    )(tile_e, n_used, slot_tok, slot_tok, prev_dst, h2t, wgu, wgu, wd)


def _combine_kernel(x_ref, ri_ref, mod_ref, fg_ref, ya_ref, yb_ref, o_ref):
    tm, d = x_ref.shape[1], x_ref.shape[2]
    ri = ri_ref[...]
    w1 = ri[:, 2:3]
    w2 = ri[:, 3:4]
    ya = _load_row_tiles(ya_ref, tm, d)
    yb = _load_row_tiles(yb_ref, tm, d)
    moe = jnp.concatenate([w1 * a + w2 * b for a, b in zip(ya, yb)], axis=1)
    x3 = x_ref[0] + mod_ref[0][5:6] * moe
    ms = jnp.mean(x3 * x3, axis=-1, keepdims=True)
    o_ref[0] = (x3 * lax.rsqrt(ms + EPS)) * fg_ref[...]


def _combine(x2, ri, mods, fg, y2):
    bsz, t, d = x2.shape
    tm = min(COMBINE_TILE, t)
    nt = t // tm
    rt = d // LANES
    n_blocks = bsz * nt
    return pl.pallas_call(
        _combine_kernel,
        grid=(bsz, nt),
        in_specs=[pl.BlockSpec((1, tm, d), lambda b, i: (b, i, 0)),
                  pl.BlockSpec((tm, LANES), lambda b, i: (b * nt + i, 0)),
                  pl.BlockSpec((1, 6, d), lambda b, i: (b, 0, 0)),
                  pl.BlockSpec((1, d), lambda b, i: (0, 0)),
                  pl.BlockSpec((tm * rt, LANES), lambda b, i: (b * nt + i, 0)),
                  pl.BlockSpec((tm * rt, LANES), lambda b, i: (n_blocks + b * nt + i, 0))],
        out_specs=pl.BlockSpec((1, tm, d), lambda b, i: (b, i, 0)),
        out_shape=jax.ShapeDtypeStruct(x2.shape, F32),
        compiler_params=_params("parallel", "parallel"),
        name="moe_combine_norm",
    )(x2, ri, mods, fg, y2, y2)


def _routing(flat_e, n_tok):
    n2 = flat_e.shape[0]
    experts = jnp.arange(N_EXPERTS, dtype=jnp.int32)
    counts = jnp.sum((flat_e[:, None] == experts[None, :]).astype(jnp.int32), axis=0)
    padded = (counts + MOE_TILE - 1) // MOE_TILE * MOE_TILE
    pends = jnp.cumsum(padded)
    n_tiles = n2 // MOE_TILE + N_EXPERTS + 1
    idx_bits = max(n2, MOE_TILE).bit_length()
    r = jnp.arange(MOE_TILE, dtype=jnp.int32)
    pad_e = jnp.where(r[None, :] < (padded - counts)[:, None], experts[:, None], N_EXPERTS)
    pad_e = jnp.concatenate([pad_e.reshape(-1), jnp.full((MOE_TILE,), N_EXPERTS, jnp.int32)])
    keys = jnp.concatenate([
        (flat_e << (idx_bits + 1)) | jnp.arange(n2, dtype=jnp.int32),
        (pad_e << (idx_bits + 1)) | (1 << idx_bits) | jnp.tile(r, N_EXPERTS + 1)])
    keys = jnp.sort(keys, stable=False)
    valid = ((keys >> idx_bits) & 1) == 0
    f = keys & ((1 << idx_bits) - 1)
    slot = jnp.arange(n_tiles * MOE_TILE, dtype=jnp.int32)
    slot_tok = jnp.where(valid, f % n_tok, 0).astype(jnp.int32)
    slot_dst = jnp.where(valid, f, n2 + slot % MOE_TILE).astype(jnp.int32)
    prev_dst = jnp.concatenate([n2 + jnp.arange(MOE_TILE, dtype=jnp.int32), slot_dst[:-MOE_TILE]])
    tile0 = jnp.arange(n_tiles, dtype=jnp.int32) * MOE_TILE
    tile_e = jnp.minimum(jnp.sum((tile0[:, None] >= pends[None, :]).astype(jnp.int32), axis=1), N_EXPERTS - 1)
    n_used = (pends[-1] // MOE_TILE).astype(jnp.int32).reshape(1)
    return slot_tok, prev_dst, tile_e.astype(jnp.int32), n_used


def kernel(x, c, ctx, c_ctx, ada_w, ada_b, norm1_g, norm2_g, pool_w, pool_scale, ffn_w_gu, ffn_w_down,
           lru_w_in, lru_conv_w, lru_conv_b, lru_w_r, lru_b_r, lru_w_i, lru_b_i, lru_lambda, lru_w_out,
           moe_w_router, moe_w_gu, moe_w_down, final_g):
    bsz, seq, d = x.shape
    assert ada_w.shape[0] == 2 and seq % TOK_TILE == 0 and TOK_TILE % POOL_TILE == 0 and ctx.shape[1] == POOL_TILE
    assert d == SUBLANES * LANES
    d_rnn = lru_w_out.shape[1]
    heads = d_rnn // LRU_BLOCK
    gc = d // len(POOL_WINDOWS)
    z_row = bsz

    n_rows = (bsz + 1 + SUBLANES - 1) // SUBLANES * SUBLANES
    cc = jnp.concatenate([c, c_ctx[None, :], jnp.zeros((n_rows - bsz - 1, d), F32)], axis=0)
    n_grp = pool_w.shape[1]
    mods, wgu0, wd0, pw0 = _adaln(cc, ada_w, ada_b, cast=(ffn_w_gu[0], ffn_w_down[0], pool_w[0].reshape(n_grp * gc, gc)))
    mods = mods.reshape(2, n_rows, 6, d)
    x_row = lambda b: b
    ctx_row = lambda b: z_row

    band_x, invc_x = _pool_tables(GRID_W, gc)
    band_z, invc_z = _pool_tables(POOL_TILE, gc)
    l0 = (norm1_g[0:1], norm2_g[0:1])
    l0w = (pw0.reshape(n_grp, gc, gc), pool_scale[0:1], wgu0, wd0)
    n_exp, _, d_gu = moe_w_gu.shape[1:]
    d_exp = moe_w_down.shape[2]
    x1, wgu_e = _layer0(x, mods[0], x_row, *l0, band_x, invc_x, *l0w, name="layer0_latent",
                        cast=(moe_w_gu[0].reshape(n_exp * d, d_gu),))
    z1, w_in, w_out = _layer0(ctx, mods[0], ctx_row, *l0, band_z, invc_z, *l0w, name="layer0_context",
                              cast=(lru_w_in[0], lru_w_out[0]))

    gate_x, ux = _inproj(x1, mods[1], x_row, norm1_g[1:2], w_in, True, name="lru_inproj_latent")
    (uz,) = _inproj(z1, mods[1], ctx_row, norm1_g[1:2], w_in[:, d_rnn:], False, name="lru_inproj_context")
    wcat = (0.5 * jnp.concatenate([lru_w_r[0, 0], lru_w_i[0, 0], lru_w_r[0, 1], lru_w_i[0, 1]], axis=-1)).astype(BF16)
    bcat = 0.5 * jnp.concatenate([v.reshape(heads, 1, LRU_BLOCK)
                                  for v in (lru_b_r[0, 0], lru_b_i[0, 0], lru_b_r[0, 1], lru_b_i[0, 1])], axis=-1)
    y, wd_e = _lru(uz, ux, gate_x, lru_conv_w[0], lru_conv_b[0:1], wcat, bcat, lru_lambda[0],
                   cast=(moe_w_down[0].reshape(n_exp * d_exp, d),))
    wgu_e = wgu_e.reshape(n_exp, d, d_gu)
    wd_e = wd_e.reshape(n_exp, d_exp, d)

    wr_hi = moe_w_router[0].astype(BF16)
    wr_lo = (moe_w_router[0] - wr_hi.astype(F32)).astype(BF16)
    wr2 = jnp.pad(jnp.concatenate([wr_hi, wr_lo], axis=1), ((0, 0), (0, LANES - 2 * N_EXPERTS)))
    x2, h2t, ri, rt = _post(y, x1, mods[1], norm2_g[1:2], w_out, wr2)

    n_tok = bsz * seq
    flat_e = rt[0:2].reshape(-1).astype(jnp.int32)
    slot_tok, prev_dst, tile_e, n_used = _routing(flat_e, n_tok)
    y2 = _moe(tile_e, n_used, slot_tok, prev_dst, h2t, wgu_e, wd_e, n_tok)
    return _combine(x2, ri, mods[1], final_g[None, :], y2)
```

```python
import functools

import numpy as np
import jax
import jax.numpy as jnp
from jax import lax
from jax.experimental import pallas as pl
from jax.experimental.pallas import tpu as pltpu

F32 = jnp.float32
BF16 = jnp.bfloat16

GRID_W = 64
POOL_WINDOWS = (2, 4, 8, 16)
LRU_BLOCK = 128
LRU_C = 8.0
N_EXPERTS = 8
EPS = 1e-6
LOG2_E = 1.4426950408889634

LANES = 128
SUBLANES = 8
VMEM_LIMIT = 56 * 1024 * 1024

POOL_TILE = 256
TOK_TILE = 512
INPROJ_TILE = 1024
INPROJ_SUB = 256
FF_CHUNK = 256
MOE_TILE = 1024
MOE_FCHUNK = 1792
MOE_SUBCHUNK = 256
DMA_UNROLL = 8
POST_TILE = 1024
POST_SUB = 256
COMBINE_TILE = 1024
LRU_SEGS = 64


def _dot(a, b):
    return jnp.dot(a, b, preferred_element_type=F32)


def _params(*sem):
    return pltpu.CompilerParams(dimension_semantics=sem, vmem_limit_bytes=VMEM_LIMIT)


def _resident(shape):
    return pl.BlockSpec(shape, lambda *_: (0,) * len(shape), pipeline_mode=pl.Buffered(1))


def _passenger_specs(cast, grid):
    steps = grid[0] * grid[1]
    n = max(k for k in range(1, steps + 1) if all(w.shape[0] % (2 * SUBLANES * k) == 0 for w in cast))
    chunk = lambda a, b: (jnp.minimum(a * grid[1] + b, n - 1), 0)
    specs = [pl.BlockSpec((w.shape[0] // n, w.shape[1]), chunk) for w in cast]
    return specs, [jax.ShapeDtypeStruct(w.shape, BF16) for w in cast]


def _run_passengers(srcs, dsts):
    for src, dst in zip(srcs, dsts):
        dst[...] = src[...].astype(dst.dtype)


def _norm_mod(x, g, shift, scale):
    ms = jnp.mean(x * x, axis=-1, keepdims=True)
    return (x * lax.rsqrt(ms + EPS)) * (g * (1.0 + scale)) + shift


def _split_bf16(v):
    hi = v.astype(BF16)
    lo = (v - hi.astype(F32)).astype(BF16)
    return hi, lo


def _ada_kernel(cc_ref, w_ref, b_ref, *refs):
    n_cast = len(refs) // 2
    o_ref = refs[n_cast]
    _run_passengers(refs[:n_cast], refs[n_cast + 1:])
    s = cc_ref[...]
    s = s * jax.nn.sigmoid(s)
    o_ref[0] = _dot(s.astype(BF16), w_ref[0].astype(BF16)) + b_ref[0]


def _adaln(cc, ada_w, ada_b, cast=()):
    n_layers, d, n = ada_w.shape
    r = cc.shape[0]
    tn = 1536
    grid = (n_layers, n // tn)
    cast_specs, cast_shapes = _passenger_specs(cast, grid)
    return pl.pallas_call(
        _ada_kernel,
        grid=grid,
        in_specs=[pl.BlockSpec((r, d), lambda l, j: (0, 0)),
                  pl.BlockSpec((1, d, tn), lambda l, j: (l, 0, j)),
                  pl.BlockSpec((1, 1, tn), lambda l, j: (l, 0, j))] + cast_specs,
        out_specs=[pl.BlockSpec((1, r, tn), lambda l, j: (l, 0, j))] + cast_specs,
        out_shape=[jax.ShapeDtypeStruct((n_layers, r, n), F32)] + cast_shapes,
        compiler_params=_params("arbitrary", "arbitrary"),
        name="adaln",
    )(cc, ada_w, ada_b.reshape(n_layers, 1, n), *cast)


def _pool_tables(seg_len, gc):
    t = np.arange(POOL_TILE)
    seg, pos = t // seg_len, t % seg_len
    band = np.zeros((len(POOL_WINDOWS), POOL_TILE, POOL_TILE), np.float32)
    invc = np.zeros((len(POOL_WINDOWS), POOL_TILE, gc), np.float32)
    for gi, win in enumerate(POOL_WINDOWS):
        lo = np.clip(pos - win // 2, 0, seg_len)
        hi = np.clip(pos - win // 2 + win, 0, seg_len)
        inside = (pos[None, :] >= lo[:, None]) & (pos[None, :] < hi[:, None])
        band[gi] = (seg[:, None] == seg[None, :]) & inside
        invc[gi] = (1.0 / (hi - lo))[:, None]
    return jnp.asarray(band, BF16), jnp.asarray(invc, F32)


def _layer0_kernel(x_ref, mod_ref, g1_ref, g2_ref, band_ref, invc_ref, pw_ref, ps_ref,
                   wgu_ref, wd_ref, *refs, d_ff):
    n_cast = (len(refs) - 2) // 2
    o_ref, act_ref = refs[n_cast], refs[-1]
    _run_passengers(refs[:n_cast], refs[n_cast + 1:-1])
    m = mod_ref[0]
    x = x_ref[0]
    tm, d = x.shape
    n_groups = band_ref.shape[0]
    gc = d // n_groups
    h = _norm_mod(x, g1_ref[...], m[0:1], m[1:2])
    rows = []
    for s in range(tm // POOL_TILE):
        hs = h[s * POOL_TILE:(s + 1) * POOL_TILE]
        cols = []
        for gi in range(n_groups):
            hg = hs[:, gi * gc:(gi + 1) * gc]
            hi, lo = _split_bf16(hg)
            band = band_ref[gi]
            pooled = (_dot(band, hi) + _dot(band, lo)) * invc_ref[gi] - hg
            cols.append(_dot(pooled.astype(BF16), pw_ref[gi]))
        rows.append(jnp.concatenate(cols, axis=1))
    y = rows[0] if len(rows) == 1 else jnp.concatenate(rows, axis=0)
    x1 = x + m[2:3] * (y * ps_ref[...])
    h2 = _norm_mod(x1, g2_ref[...], m[3:4], m[4:5]).astype(BF16)
    for c in range(d_ff // FF_CHUNK):
        g = _dot(h2, wgu_ref[:, c * FF_CHUNK:(c + 1) * FF_CHUNK])
        u = _dot(h2, wgu_ref[:, d_ff + c * FF_CHUNK:d_ff + (c + 1) * FF_CHUNK])
        act_ref[:, c * FF_CHUNK:(c + 1) * FF_CHUNK] = ((g * jax.nn.sigmoid(g)) * u).astype(BF16)
    o_ref[0] = x1 + m[5:6] * _dot(act_ref[...], wd_ref[...])


def _layer0(x, mods, mod_row, g1, g2, band, invc, pw, ps, wgu, wd, name, cast=()):
    bsz, t, d = x.shape
    d_ff = wd.shape[0]
    tm = min(TOK_TILE, t)
    grid = (bsz, t // tm)
    cast_specs, cast_shapes = _passenger_specs(cast, grid)
    return pl.pallas_call(
        functools.partial(_layer0_kernel, d_ff=d_ff),
        grid=grid,
        in_specs=[pl.BlockSpec((1, tm, d), lambda b, i: (b, i, 0)),
                  pl.BlockSpec((1, 6, d), lambda b, i: (mod_row(b), 0, 0)),
                  _resident((1, d)), _resident((1, d)),
                  _resident(band.shape), _resident(invc.shape), _resident(pw.shape), _resident((1, d)),
                  _resident(wgu.shape), _resident(wd.shape)] + cast_specs,
        out_specs=[pl.BlockSpec((1, tm, d), lambda b, i: (b, i, 0))] + cast_specs,
        out_shape=[jax.ShapeDtypeStruct(x.shape, F32)] + cast_shapes,
        scratch_shapes=[pltpu.VMEM((tm, d_ff), BF16)],
        compiler_params=_params("arbitrary", "arbitrary"),
        name=name,
    )(x, mods, g1, g2, band, invc, pw, ps, wgu, wd, *cast)


def _inproj_kernel(x_ref, mod_ref, g1_ref, w_ref, *out_refs, with_gate):
    m = mod_ref[0]
    n = out_refs[0].shape[-1]
    tm = x_ref.shape[1]
    sub = min(INPROJ_SUB, tm)
    for q in range(tm // sub):
        rows = pl.ds(q * sub, sub)
        h = _norm_mod(x_ref[0, rows, :], g1_ref[...], m[0:1], m[1:2]).astype(BF16)
        y = _dot(h, w_ref[...])
        if with_gate:
            out_refs[0][0, rows, :] = jax.nn.gelu(y[:, :n])
            out_refs[1][0, rows, :] = y[:, n:]
        else:
            out_refs[0][0, rows, :] = y


def _inproj(x, mods, mod_row, g1, w, with_gate, name):
    bsz, t, d = x.shape
    tm = min(INPROJ_TILE, t)
    n_out = 2 if with_gate else 1
    n = w.shape[1] // n_out
    out_shape = jax.ShapeDtypeStruct((bsz, t, n), F32)
    out_spec = pl.BlockSpec((1, tm, n), lambda b, i: (b, i, 0))
    return pl.pallas_call(
        functools.partial(_inproj_kernel, with_gate=with_gate),
        grid=(bsz, t // tm),
        in_specs=[pl.BlockSpec((1, tm, d), lambda b, i: (b, i, 0)),
                  pl.BlockSpec((1, 6, d), lambda b, i: (mod_row(b), 0, 0)),
                  _resident((1, d)), _resident(w.shape)],
        out_specs=[out_spec] * n_out,
        out_shape=[out_shape] * n_out,
        compiler_params=_params("parallel", "parallel"),
        name=name,
    )(x, mods, g1, w)


def _scan8(a, b, row, reverse):
    for s in (1, 2, 4):
        if reverse:
            keep = row < SUBLANES - s
            shift = SUBLANES - s
        else:
            keep = row >= s
            shift = s
        a_sh = jnp.where(keep, pltpu.roll(a, shift, 0), 1.0)
        b_sh = jnp.where(keep, pltpu.roll(b, shift, 0), 0.0)
        b = a * b_sh + b
        a = a * a_sh
    return a, b


def _segment_scan(a_ref, b_ref, h_ref, row8, seg_len, reverse):
    nv = LRU_SEGS // SUBLANES

    def rows(k, j):
        jj = seg_len - 1 - j if reverse else j
        return pl.ds(k * SUBLANES * seg_len + jj, SUBLANES, stride=seg_len)

    zero = jnp.zeros((SUBLANES, LANES), F32)
    h = [zero] * nv
    p = [jnp.ones((SUBLANES, LANES), F32)] * nv
    for j in range(seg_len):
        for k in range(nv):
            a = a_ref[rows(k, j), :]
            h[k] = a * h[k] + b_ref[rows(k, j), :]
            p[k] = a * p[k]
    carry = zero
    start = [None] * nv
    for k in (range(nv - 1, -1, -1) if reverse else range(nv)):
        pk, hk = _scan8(p[k], h[k], row8, reverse)
        ends = hk + pk * carry
        if reverse:
            start[k] = jnp.where(row8 == SUBLANES - 1, carry, pltpu.roll(ends, SUBLANES - 1, 0))
            carry = jnp.broadcast_to(ends[0:1], ends.shape)
        else:
            start[k] = jnp.where(row8 == 0, carry, pltpu.roll(ends, 1, 0))
            carry = jnp.broadcast_to(ends[SUBLANES - 1:SUBLANES], ends.shape)
    h = start
    for j in range(seg_len):
        for k in range(nv):
            h[k] = a_ref[rows(k, j), :] * h[k] + b_ref[rows(k, j), :]
            h_ref[rows(k, j), :] = h[k]


def _lru_kernel(uz_ref, ux_ref, gate_ref, cw_ref, cb_ref, wc_ref, bc_ref, lam_ref, *refs, seg_len):
    n_cast = (len(refs) - 8) // 2
    cast_in, (o_ref, *cast_out) = refs[:n_cast], refs[n_cast:2 * n_cast + 1]
    af_ref, bf_ref, ab_ref, bb_ref, hf_ref, hb_ref, pad_ref = refs[2 * n_cast + 1:]
    _run_passengers(cast_in, cast_out)
    lz = uz_ref.shape[1]
    lx = ux_ref.shape[1]
    cw = cw_ref[...]
    cb = cb_ref[...]
    wc = wc_ref[0]
    bc = bc_ref[0]
    neg_lam = -lam_ref[...]
    softplus = jnp.maximum(neg_lam, 0.0) + jnp.log1p(jnp.exp(-jnp.abs(neg_lam)))
    kh = (-0.5 * LRU_C * LOG2_E) * softplus
    zeros8 = jnp.zeros((SUBLANES, LANES), F32)

    def prep(src_ref, off_f, off_b):
        n = src_ref.shape[1]
        pad_ref[pl.ds(0, SUBLANES), :] = zeros8
        pad_ref[pl.ds(SUBLANES, n), :] = src_ref[0]
        pad_ref[pl.ds(SUBLANES + n, SUBLANES), :] = zeros8
        u = cb + cw[2:3] * src_ref[0]
        for k in (0, 1, 3):
            u = u + cw[k:k + 1] * pad_ref[pl.ds(SUBLANES + k - 2, n), :]
        t = jnp.tanh(_dot(u.astype(BF16), wc) + bc)
        hu = 0.5 * u
        for d, (a_ref, b_ref, off) in enumerate(((af_ref, bf_ref, off_f), (ab_ref, bb_ref, off_b))):
            t_r = t[:, 2 * d * LRU_BLOCK:(2 * d + 1) * LRU_BLOCK]
            t_i = t[:, (2 * d + 1) * LRU_BLOCK:(2 * d + 2) * LRU_BLOCK]
            a = jnp.exp2(kh[d:d + 1] * t_r + kh[d:d + 1])
            y = 1.0 - a * a
            root = jnp.where(y > 0.0, y * lax.rsqrt(y), 0.0)
            a_ref[pl.ds(off, n), :] = a
            b_ref[pl.ds(off, n), :] = root * (hu * t_i + hu)

    prep(uz_ref, 0, lx)
    prep(ux_ref, lz, 0)
    row8 = lax.broadcasted_iota(jnp.int32, (SUBLANES, LANES), 0)
    _segment_scan(af_ref, bf_ref, hf_ref, row8, seg_len, False)
    _segment_scan(ab_ref, bb_ref, hb_ref, row8, seg_len, True)
    hsum = hf_ref[pl.ds(lz, lx), :] + hb_ref[pl.ds(0, lx), :]
    o_ref[0] = (hsum * gate_ref[0]).astype(BF16)


def _lru(uz, ux, gate, conv_w, conv_b, wcat, bcat, lam, cast=()):
    bsz, lz, d_rnn = uz.shape
    lx = ux.shape[1]
    heads = d_rnn // LRU_BLOCK
    cast_specs, cast_shapes = _passenger_specs(cast, (bsz, heads))
    seg_len = (lz + lx) // LRU_SEGS
    assert seg_len * LRU_SEGS == lz + lx and seg_len % SUBLANES == SUBLANES // 2
    seq = lambda n: pl.BlockSpec((1, n, LRU_BLOCK), lambda b, h: (b, 0, h))
    scratch = pltpu.VMEM((lz + lx, LRU_BLOCK), F32)
    return pl.pallas_call(
        functools.partial(_lru_kernel, seg_len=seg_len),
        grid=(bsz, heads),
        in_specs=[seq(lz), seq(lx), seq(lx),
                  pl.BlockSpec((conv_w.shape[0], LRU_BLOCK), lambda b, h: (0, h)),
                  pl.BlockSpec((1, LRU_BLOCK), lambda b, h: (0, h)),
                  pl.BlockSpec((1, LRU_BLOCK, 4 * LRU_BLOCK), lambda b, h: (h, 0, 0)),
                  pl.BlockSpec((1, 1, 4 * LRU_BLOCK), lambda b, h: (h, 0, 0)),
                  pl.BlockSpec((2, LRU_BLOCK), lambda b, h: (0, h))] + cast_specs,
        out_specs=[seq(lx)] + cast_specs,
        out_shape=[jax.ShapeDtypeStruct((bsz, lx, d_rnn), BF16)] + cast_shapes,
        scratch_shapes=[scratch] * 6 + [pltpu.VMEM((lx + 2 * SUBLANES, LRU_BLOCK), F32)],
        compiler_params=_params("arbitrary", "arbitrary"),
        name="rglru",
    )(uz, ux, gate, conv_w, conv_b, wcat, bcat, lam, *cast)


def _store_row_tiles(ref, v):
    tm, d = v.shape
    for c in range(d // LANES):
        ref[pl.ds(c, tm, stride=d // LANES), :] = v[:, c * LANES:(c + 1) * LANES]


def _load_row_tiles(ref, tm, d):
    return [ref[pl.ds(c, tm, stride=d // LANES), :] for c in range(d // LANES)]


def _post_kernel(y_ref, x_ref, mod_ref, g2_ref, wo_ref, wr2_ref, x2_ref, h2t_ref, ri_ref, rt_ref):
    m = mod_ref[0]
    tm = x_ref.shape[1]
    rt = x_ref.shape[2] // LANES
    for q in range(tm // POST_SUB):
        rows = pl.ds(q * POST_SUB, POST_SUB)
        x2 = x_ref[0, rows, :] + m[2:3] * _dot(y_ref[0, rows, :], wo_ref[...])
        x2_ref[0, rows, :] = x2
        h2 = _norm_mod(x2, g2_ref[...], m[3:4], m[4:5])
        for c in range(rt):
            h2t_ref[pl.ds(q * POST_SUB * rt + c, POST_SUB, stride=rt), :] = h2[:, c * LANES:(c + 1) * LANES]
        hi, lo = _split_bf16(h2)
        both = _dot(hi, wr2_ref[...]) + _dot(lo, wr2_ref[...])
        logits = both + pltpu.roll(both, LANES - N_EXPERTS, 1)
        lane = lax.broadcasted_iota(jnp.int32, logits.shape, 1)
        logits = jnp.where(lane < N_EXPERTS, logits, -1e30)
        e = jnp.exp(logits - jnp.max(logits, axis=-1, keepdims=True))
        p = e / jnp.sum(e, axis=-1, keepdims=True)
        p1 = jnp.max(p, axis=-1, keepdims=True)
        i1 = jnp.min(jnp.where(p == p1, lane, LANES), axis=-1, keepdims=True)
        rest = jnp.where(lane == i1, -1.0, p)
        p2 = jnp.max(rest, axis=-1, keepdims=True)
        i2 = jnp.min(jnp.where(rest == p2, lane, LANES), axis=-1, keepdims=True)
        den = p1 + p2
        ri = jnp.where(lane == 0, i1.astype(F32),
                       jnp.where(lane == 1, i2.astype(F32),
                                 jnp.where(lane == 2, p1 / den,
                                           jnp.where(lane == 3, p2 / den, 0.0))))
        ri_ref[rows, :] = ri
        rt_ref[:, rows] = jnp.transpose(ri)[0:SUBLANES, :]


def _post(y, x, mods, g2, wo, wr2):
    bsz, t, d = x.shape
    d_rnn = y.shape[-1]
    tm = min(POST_TILE, t)
    nt = t // tm
    rt = d // LANES
    return pl.pallas_call(
        _post_kernel,
        grid=(bsz, nt),
        in_specs=[pl.BlockSpec((1, tm, d_rnn), lambda b, i: (b, i, 0)),
                  pl.BlockSpec((1, tm, d), lambda b, i: (b, i, 0)),
                  pl.BlockSpec((1, 6, d), lambda b, i: (b, 0, 0)),
                  _resident((1, d)), _resident(wo.shape), _resident(wr2.shape)],
        out_specs=[pl.BlockSpec((1, tm, d), lambda b, i: (b, i, 0)),
                   pl.BlockSpec((tm * rt, LANES), lambda b, i: (b * nt + i, 0)),
                   pl.BlockSpec((tm, LANES), lambda b, i: (b * nt + i, 0)),
                   pl.BlockSpec((SUBLANES, tm), lambda b, i: (0, b * nt + i))],
        out_shape=[jax.ShapeDtypeStruct(x.shape, F32),
                   jax.ShapeDtypeStruct((bsz * t * rt, LANES), F32),
                   jax.ShapeDtypeStruct((bsz * t, LANES), F32),
                   jax.ShapeDtypeStruct((SUBLANES, bsz * t), F32)],
        compiler_params=_params("parallel", "parallel"),
        name="outproj_router",
    )(y, x, mods, g2, wo, wr2)


def _moe_kernel(te_ref, nu_ref, first_ref, nxt_ref, prev_ref, h2t_ref, wg_ref, wu_ref, wd_ref, y2_ref,
                xs_buf, xb_ref, act_ref, acc_ref, out_buf, gsem, ssem, *, n_steps):
    i = pl.program_id(0)
    j = pl.program_id(1)
    tm, d = xb_ref.shape
    tf = act_ref.shape[1]
    rt = d // LANES
    nu = nu_ref[0]
    used = i < nu
    slot = i % 2

    def gather_row(idx_ref, buf_slot, r, priority=0):
        src_row = pl.multiple_of(idx_ref[r] * rt, rt)
        dst_row = pl.multiple_of(r * rt, rt)
        pltpu.make_async_copy(h2t_ref.at[pl.ds(src_row, rt), :],
                              xs_buf.at[buf_slot, pl.ds(dst_row, rt), :],
                              gsem.at[buf_slot]).start(priority=priority)

    def scatter_row(r, priority=0):
        src_row = pl.multiple_of(r * rt, rt)
        dst_row = pl.multiple_of(prev_ref[r] * rt, rt)
        pltpu.make_async_copy(out_buf.at[1 - slot, pl.ds(src_row, rt), :],
                              y2_ref.at[pl.ds(dst_row, rt), :], ssem).start(priority=priority)

    def wait_scatter():
        pltpu.make_async_copy(out_buf.at[0], out_buf.at[0], ssem).wait()

    @pl.when(jnp.logical_and(i == 0, j == 0))
    def _():
        out_buf[...] = jnp.zeros(out_buf.shape, out_buf.dtype)

        def issue(r, carry):
            gather_row(first_ref, 0, r)
            return carry
        lax.fori_loop(0, tm, issue, 0, unroll=DMA_UNROLL)

    @pl.when(jnp.logical_and(j == 0, i <= nu))
    def _():
        for s in range(2):
            @pl.when(slot == s)
            def _(s=s):
                pltpu.make_async_copy(xs_buf.at[s], xs_buf.at[s], gsem.at[s]).wait()

                @pl.when(used)
                def _():
                    for c in range(rt):
                        xb_ref[:, c * LANES:(c + 1) * LANES] = (
                            xs_buf[s, pl.ds(c, tm, stride=rt), :].astype(BF16))

        @pl.when(i == nu)
        def _():
            def issue(r, carry):
                scatter_row(r)
                return carry
            lax.fori_loop(0, tm, issue, 0, unroll=DMA_UNROLL)
            wait_scatter()

    def compute(step):
        xb = xb_ref[...]
        n_sub = tf // MOE_SUBCHUNK
        per = -(-tm // (2 * n_sub))

        def issue_rows(k):
            if step == 0:
                for r in range(k * per, min((k + 1) * per, tm)):
                    gather_row(nxt_ref, 1 - slot, r, priority=0)
                    scatter_row(r, priority=1)

        for c in range(n_sub):
            cs = slice(c * MOE_SUBCHUNK, (c + 1) * MOE_SUBCHUNK)
            g = _dot(xb, wg_ref[0, :, cs])
            issue_rows(2 * c)
            u = _dot(xb, wu_ref[0, :, cs])
            issue_rows(2 * c + 1)
            act_ref[:, cs] = ((g * jax.nn.sigmoid(g)) * u).astype(BF16)
        part = _dot(act_ref[...], wd_ref[0])
        if step < n_steps - 1:
            acc_ref[...] = part if step == 0 else acc_ref[...] + part
        else:
            total = part if step == 0 else acc_ref[...] + part
            for c in range(rt):
                out_buf[slot, pl.ds(c, tm, stride=rt), :] = total[:, c * LANES:(c + 1) * LANES]
            wait_scatter()

    for step in range(n_steps):
        @pl.when(jnp.logical_and(used, j == step))
        def _(step=step):
            compute(step)


def _moe(tile_e, n_used, slot_tok, prev_dst, h2t, wgu, wd, n_tok):
    d = wgu.shape[1]
    d_e = wd.shape[1]
    rt = d // LANES
    tm = MOE_TILE
    n_tiles = slot_tok.shape[0] // tm
    nj = d_e // MOE_FCHUNK

    def jf(i, j, nu):
        return jnp.where(i < nu[0], j, nj - 1)

    smem_tile = lambda f: pl.BlockSpec((tm,), f, memory_space=pltpu.SMEM)
    return pl.pallas_call(
        functools.partial(_moe_kernel, n_steps=nj),
        grid_spec=pltpu.PrefetchScalarGridSpec(
            num_scalar_prefetch=2,
            grid=(n_tiles, nj),
            in_specs=[smem_tile(lambda i, j, te, nu: (0,)),
                      smem_tile(lambda i, j, te, nu: (jnp.minimum(i + 1, n_tiles - 1),)),
                      smem_tile(lambda i, j, te, nu: (i,)),
                      pl.BlockSpec(memory_space=pl.ANY),
                      pl.BlockSpec((1, d, MOE_FCHUNK), lambda i, j, te, nu: (te[i], 0, jf(i, j, nu))),
                      pl.BlockSpec((1, d, MOE_FCHUNK), lambda i, j, te, nu: (te[i], 0, nj + jf(i, j, nu))),
                      pl.BlockSpec((1, MOE_FCHUNK, d), lambda i, j, te, nu: (te[i], jf(i, j, nu), 0))],
            out_specs=pl.BlockSpec(memory_space=pl.ANY),
            scratch_shapes=[pltpu.VMEM((2, tm * rt, LANES), F32),
                            pltpu.VMEM((tm, d), BF16),
                            pltpu.VMEM((tm, MOE_FCHUNK), BF16),
                            pltpu.VMEM((tm, d), F32),
                            pltpu.VMEM((2, tm * rt, LANES), F32),
                            pltpu.SemaphoreType.DMA((2,)),
                            pltpu.SemaphoreType.DMA]),
        out_shape=jax.ShapeDtypeStruct(((2 * n_tok + tm) * rt, LANES), F32),
        compiler_params=_params("arbitrary", "arbitrary"),
        name="moe_experts",
    )(tile_e, n_used, slot_tok, slot_tok, prev_dst, h2t, wgu, wgu, wd)


def _combine_kernel(x_ref, ri_ref, mod_ref, fg_ref, ya_ref, yb_ref, o_ref):
    tm, d = x_ref.shape[1], x_ref.shape[2]
    ri = ri_ref[...]
    w1 = ri[:, 2:3]
    w2 = ri[:, 3:4]
    ya = _load_row_tiles(ya_ref, tm, d)
    yb = _load_row_tiles(yb_ref, tm, d)
    moe = jnp.concatenate([w1 * a + w2 * b for a, b in zip(ya, yb)], axis=1)
    x3 = x_ref[0] + mod_ref[0][5:6] * moe
    ms = jnp.mean(x3 * x3, axis=-1, keepdims=True)
    o_ref[0] = (x3 * lax.rsqrt(ms + EPS)) * fg_ref[...]


def _combine(x2, ri, mods, fg, y2):
    bsz, t, d = x2.shape
    tm = min(COMBINE_TILE, t)
    nt = t // tm
    rt = d // LANES
    n_blocks = bsz * nt
    return pl.pallas_call(
        _combine_kernel,
        grid=(bsz, nt),
        in_specs=[pl.BlockSpec((1, tm, d), lambda b, i: (b, i, 0)),
                  pl.BlockSpec((tm, LANES), lambda b, i: (b * nt + i, 0)),
                  pl.BlockSpec((1, 6, d), lambda b, i: (b, 0, 0)),
                  pl.BlockSpec((1, d), lambda b, i: (0, 0)),
                  pl.BlockSpec((tm * rt, LANES), lambda b, i: (b * nt + i, 0)),
                  pl.BlockSpec((tm * rt, LANES), lambda b, i: (n_blocks + b * nt + i, 0))],
        out_specs=pl.BlockSpec((1, tm, d), lambda b, i: (b, i, 0)),
        out_shape=jax.ShapeDtypeStruct(x2.shape, F32),
        compiler_params=_params("parallel", "parallel"),
        name="moe_combine_norm",
    )(x2, ri, mods, fg, y2, y2)


def _routing(flat_e, n_tok):
    n2 = flat_e.shape[0]
    experts = jnp.arange(N_EXPERTS, dtype=jnp.int32)
    counts = jnp.sum((flat_e[:, None] == experts[None, :]).astype(jnp.int32), axis=0)
    padded = (counts + MOE_TILE - 1) // MOE_TILE * MOE_TILE
    pends = jnp.cumsum(padded)
    n_tiles = n2 // MOE_TILE + N_EXPERTS + 1
    idx_bits = max(n2, MOE_TILE).bit_length()
    r = jnp.arange(MOE_TILE, dtype=jnp.int32)
    pad_e = jnp.where(r[None, :] < (padded - counts)[:, None], experts[:, None], N_EXPERTS)
    pad_e = jnp.concatenate([pad_e.reshape(-1), jnp.full((MOE_TILE,), N_EXPERTS, jnp.int32)])
    keys = jnp.concatenate([
        (flat_e << (idx_bits + 1)) | jnp.arange(n2, dtype=jnp.int32),
        (pad_e << (idx_bits + 1)) | (1 << idx_bits) | jnp.tile(r, N_EXPERTS + 1)])
    keys = jnp.sort(keys, stable=False)
    valid = ((keys >> idx_bits) & 1) == 0
    f = keys & ((1 << idx_bits) - 1)
    slot = jnp.arange(n_tiles * MOE_TILE, dtype=jnp.int32)
    slot_tok = jnp.where(valid, f % n_tok, 0).astype(jnp.int32)
    slot_dst = jnp.where(valid, f, n2 + slot % MOE_TILE).astype(jnp.int32)
    prev_dst = jnp.concatenate([n2 + jnp.arange(MOE_TILE, dtype=jnp.int32), slot_dst[:-MOE_TILE]])
    tile0 = jnp.arange(n_tiles, dtype=jnp.int32) * MOE_TILE
    tile_e = jnp.minimum(jnp.sum((tile0[:, None] >= pends[None, :]).astype(jnp.int32), axis=1), N_EXPERTS - 1)
    n_used = (pends[-1] // MOE_TILE).astype(jnp.int32).reshape(1)
    return slot_tok, prev_dst, tile_e.astype(jnp.int32), n_used


def kernel(x, c, ctx, c_ctx, ada_w, ada_b, norm1_g, norm2_g, pool_w, pool_scale, ffn_w_gu, ffn_w_down,
           lru_w_in, lru_conv_w, lru_conv_b, lru_w_r, lru_b_r, lru_w_i, lru_b_i, lru_lambda, lru_w_out,
           moe_w_router, moe_w_gu, moe_w_down, final_g):
    bsz, seq, d = x.shape
    assert ada_w.shape[0] == 2 and seq % TOK_TILE == 0 and TOK_TILE % POOL_TILE == 0 and ctx.shape[1] == POOL_TILE
    assert d == SUBLANES * LANES
    d_rnn = lru_w_out.shape[1]
    heads = d_rnn // LRU_BLOCK
    gc = d // len(POOL_WINDOWS)
    z_row = bsz

    n_rows = (bsz + 1 + SUBLANES - 1) // SUBLANES * SUBLANES
    cc = jnp.concatenate([c, c_ctx[None, :], jnp.zeros((n_rows - bsz - 1, d), F32)], axis=0)
    n_grp = pool_w.shape[1]
    mods, wgu0, wd0, pw0 = _adaln(cc, ada_w, ada_b, cast=(ffn_w_gu[0], ffn_w_down[0], pool_w[0].reshape(n_grp * gc, gc)))
    mods = mods.reshape(2, n_rows, 6, d)
    x_row = lambda b: b
    ctx_row = lambda b: z_row

    band_x, invc_x = _pool_tables(GRID_W, gc)
    band_z, invc_z = _pool_tables(POOL_TILE, gc)
    l0 = (norm1_g[0:1], norm2_g[0:1])
    l0w = (pw0.reshape(n_grp, gc, gc), pool_scale[0:1], wgu0, wd0)
    n_exp, _, d_gu = moe_w_gu.shape[1:]
    d_exp = moe_w_down.shape[2]
    x1, wgu_e = _layer0(x, mods[0], x_row, *l0, band_x, invc_x, *l0w, name="layer0_latent",
                        cast=(moe_w_gu[0].reshape(n_exp * d, d_gu),))
    z1, w_in, w_out = _layer0(ctx, mods[0], ctx_row, *l0, band_z, invc_z, *l0w, name="layer0_context",
                              cast=(lru_w_in[0], lru_w_out[0]))

    gate_x, ux = _inproj(x1, mods[1], x_row, norm1_g[1:2], w_in, True, name="lru_inproj_latent")
    (uz,) = _inproj(z1, mods[1], ctx_row, norm1_g[1:2], w_in[:, d_rnn:], False, name="lru_inproj_context")
    wcat = (0.5 * jnp.concatenate([lru_w_r[0, 0], lru_w_i[0, 0], lru_w_r[0, 1], lru_w_i[0, 1]], axis=-1)).astype(BF16)
    bcat = 0.5 * jnp.concatenate([v.reshape(heads, 1, LRU_BLOCK)
                                  for v in (lru_b_r[0, 0], lru_b_i[0, 0], lru_b_r[0, 1], lru_b_i[0, 1])], axis=-1)
    y, wd_e = _lru(uz, ux, gate_x, lru_conv_w[0], lru_conv_b[0:1], wcat, bcat, lru_lambda[0],
                   cast=(moe_w_down[0].reshape(n_exp * d_exp, d),))
    wgu_e = wgu_e.reshape(n_exp, d, d_gu)
    wd_e = wd_e.reshape(n_exp, d_exp, d)

    wr_hi = moe_w_router[0].astype(BF16)
    wr_lo = (moe_w_router[0] - wr_hi.astype(F32)).astype(BF16)
    wr2 = jnp.pad(jnp.concatenate([wr_hi, wr_lo], axis=1), ((0, 0), (0, LANES - 2 * N_EXPERTS)))
    x2, h2t, ri, rt = _post(y, x1, mods[1], norm2_g[1:2], w_out, wr2)

    n_tok = bsz * seq
    flat_e = rt[0:2].reshape(-1).astype(jnp.int32)
    slot_tok, prev_dst, tile_e, n_used = _routing(flat_e, n_tok)
    y2 = _moe(tile_e, n_used, slot_tok, prev_dst, h2t, wgu_e, wd_e, n_tok)
    return _combine(x2, ri, mods[1], final_g[None, :], y2)
```
